```python
import math
import jax
import jax.numpy as jnp
from jax import lax
import numpy as np

D_MODEL = 1024
BATCH = 8
SEQ = 2048
DEPTH = 4

GRID_W = 64
CTX_LEN = 256
BLK = 128
WINDOW = 128
ROPE_BASE = 10000.0
EPS = 1e-6

HD_A = 64
N_HA = 6
N_KVA = 2
G_A = N_HA // N_KVA
D_INNER = 384
HD_S = 64
N_HS = D_INNER // HD_S
N_GROUPS = 2
D_STATE = 128
CONV_K = 5
CONV_DIM = D_INNER + 2 * N_GROUPS * D_STATE
CHUNK = 128
N_HC = 4
HD_C = 32
D_C = N_HC * 2 * HD_C

D_MIX = N_HA * HD_A + D_INNER + D_C
IN_SPLITS = (N_HA * HD_A, N_KVA * HD_A, N_KVA * HD_A, D_INNER, CONV_DIM, N_HS, N_HS, D_C, D_C, D_C)
D_IN = sum(IN_SPLITS)
IN_OFFSETS = tuple(int(o) for o in np.cumsum(IN_SPLITS)[:-1])

D_FF = 2816
N_EXP = 8
TOP_K = 2
N_DENSE = (DEPTH + 1) // 2
N_MOE = DEPTH // 2
ALPHA = (2 * DEPTH) ** 0.25
BETA = (8 * DEPTH) ** -0.25

kernel_name = "hymba_style_hybrid_dit_block"


def _layernorm(x):
    xf = x.astype(jnp.float32)
    mu = jnp.mean(xf, -1, keepdims=True)
    var = jnp.mean(jnp.square(xf - mu), -1, keepdims=True)
    return ((xf - mu) * lax.rsqrt(var + EPS)).astype(x.dtype)


def _rmsnorm(x, w):
    xf = x.astype(jnp.float32)
    return (xf * lax.rsqrt(jnp.mean(xf * xf, -1, keepdims=True) + EPS)).astype(x.dtype) * w


def _modulate(x, shift, scale):
    return _layernorm(x) * (1 + scale) + shift


def _post_ln(x, g, b):
    return _layernorm(x) * g + b


def _axial_rope(rows, cols, dim, dtype):
    quarter = dim // 4
    inv = ROPE_BASE ** (-jnp.arange(quarter, dtype=jnp.float32) / quarter)
    ang = jnp.concatenate([rows[:, None] * inv, cols[:, None] * inv], axis=-1)
    return jnp.cos(ang).astype(dtype), jnp.sin(ang).astype(dtype)


def _rope(x, cos, sin):
    x1, x2 = jnp.split(x, 2, axis=-1)
    c, s = cos[:, None, :], sin[:, None, :]
    return jnp.concatenate([x1 * c - x2 * s, x1 * s + x2 * c], axis=-1)


def _window_gqa(q, k, v, qc, kc, vc, sink, cos, sin, with_ctx_out):
    B, S, _ = q.shape
    nb = S // BLK
    scale = HD_A ** -0.5
    q = _rope(q.reshape(B, S, N_HA, HD_A), cos, sin)
    k = _rope(k.reshape(B, S, N_KVA, HD_A), cos, sin)
    v = v.reshape(B, S, N_KVA, HD_A)
    kc = kc.reshape(B, -1, N_KVA, HD_A)
    vc = vc.reshape(B, -1, N_KVA, HD_A)
    n_ctx = kc.shape[1]
    sink_f = sink.astype(jnp.float32).reshape(N_KVA, G_A)
    qb = q.reshape(B, nb, BLK, N_KVA, G_A, HD_A)
    pad = ((0, 0), (BLK, BLK), (0, 0), (0, 0))
    kp, vp = jnp.pad(k, pad), jnp.pad(v, pad)
    kw = jnp.concatenate([kp[:, j * BLK:j * BLK + S].reshape(B, nb, BLK, N_KVA, HD_A) for j in range(3)], axis=2)
    vw = jnp.concatenate([vp[:, j * BLK:j * BLK + S].reshape(B, nb, BLK, N_KVA, HD_A) for j in range(3)], axis=2)
    s_loc = jnp.einsum('bnqhgd,bnkhd->bnhgqk', qb, kw).astype(jnp.float32) * scale
    s_ctx = jnp.einsum('bnqhgd,bkhd->bnhgqk', qb, kc).astype(jnp.float32) * scale
    qpos = jnp.arange(S).reshape(nb, BLK)[:, :, None]
    kpos = (jnp.arange(nb)[:, None] * BLK - BLK + jnp.arange(3 * BLK)[None, :])[:, None, :]
    valid = (jnp.abs(qpos - kpos) <= WINDOW) & (kpos >= 0) & (kpos < S)
    s_loc = jnp.where(valid[None, :, None, None], s_loc, -jnp.inf)
    s_sink = jnp.broadcast_to(sink_f[None, None, :, :, None, None], s_loc.shape[:-1] + (1,))
    p = jax.nn.softmax(jnp.concatenate([s_loc, s_ctx, s_sink], axis=-1), axis=-1).astype(v.dtype)
    y = (jnp.einsum('bnhgqk,bnkhd->bnqhgd', p[..., :3 * BLK], vw)
         + jnp.einsum('bnhgqk,bkhd->bnqhgd', p[..., 3 * BLK:3 * BLK + n_ctx], vc)).reshape(B, S, N_HA * HD_A)
    yc = None
    if with_ctx_out:
        qc = qc.reshape(B, n_ctx, N_KVA, G_A, HD_A)
        sc = jnp.einsum('bqhgd,bkhd->bhgqk', qc, kc).astype(jnp.float32) * scale
        sc_sink = jnp.broadcast_to(sink_f[None, :, :, None, None], sc.shape[:-1] + (1,))
        pc = jax.nn.softmax(jnp.concatenate([sc, sc_sink], axis=-1), axis=-1)[..., :n_ctx].astype(vc.dtype)
        yc = jnp.einsum('bhgqk,bkhd->bqhgd', pc, vc).reshape(B, n_ctx, N_HA * HD_A)
    return y, yc


def _segsum(a):
    T = a.shape[-1]
    cs = jnp.cumsum(a, axis=-1)
    diff = cs[..., :, None] - cs[..., None, :]
    return jnp.where(jnp.tril(jnp.ones((T, T), dtype=bool)), diff, -jnp.inf)


def _ssd_chunked(x, a, b, c, h0):
    Bsz, L, H, P = x.shape
    nc = L // CHUNK
    x = x.reshape(Bsz, nc, CHUNK, H, P)
    b = b.reshape(Bsz, nc, CHUNK, H, -1)
    c = c.reshape(Bsz, nc, CHUNK, H, -1)
    a = jnp.transpose(a.reshape(Bsz, nc, CHUNK, H), (0, 3, 1, 2))
    a_cum = jnp.cumsum(a, axis=-1)
    scores = jnp.einsum('bclhn,bcshn->bhcls', c, b) * jnp.exp(_segsum(a))
    y_diag = jnp.einsum('bhcls,bcshp->bclhp', scores, x)
    decay_states = jnp.exp(a_cum[..., -1:] - a_cum)
    states = jnp.einsum('bclhn,bhcl,bclhp->bchpn', b, decay_states, x)
    states = jnp.concatenate([h0[:, None], states], axis=1)
    chunk_tot = jnp.pad(a_cum[..., -1], ((0, 0), (0, 0), (1, 0)))
    decay_chunk = jnp.exp(_segsum(chunk_tot))
    new_states = jnp.einsum('bhzc,bchpn->bzhpn', decay_chunk, states)
    y_off = jnp.einsum('bclhn,bchpn,bhcl->bclhp', c, new_states[:, :-1], jnp.exp(a_cum))
    return (y_diag + y_off).reshape(Bsz, L, H, P), new_states[:, -1]


def _ssd_final_state(x, a, b):
    a_cum = jnp.cumsum(a, axis=1)
    w = jnp.exp(a_cum[:, -1:] - a_cum)
    return jnp.einsum('blhn,blh,blhp->bhpn', b, w, x)


def _dwconv(u, w, bias):
    out = lax.conv_general_dilated(u, w[:, None, :], (1,), ((CONV_K // 2, CONV_K // 2),),
                                   dimension_numbers=('NWC', 'WIO', 'NWC'), feature_group_count=CONV_DIM)
    return out + bias


def _ssd_inputs(xbc, conv_w, conv_b):
    B, L, _ = xbc.shape
    u = jax.nn.silu(_dwconv(xbc, conv_w, conv_b)).astype(jnp.float32)
    xs, bm, cm = jnp.split(u, [D_INNER, D_INNER + N_GROUPS * D_STATE], axis=-1)
    rep = N_HS // N_GROUPS
    bm = jnp.repeat(bm.reshape(B, L, N_GROUPS, D_STATE), rep, axis=2)
    cm = jnp.repeat(cm.reshape(B, L, N_GROUPS, D_STATE), rep, axis=2)
    return xs.reshape(B, L, N_HS, HD_S), bm, cm


def _ssd_direction(lat, ctx_in, dt_raw, dt_raw_c, dt_bias, a_log, reverse, with_ctx_out):
    xs, bm, cm = lat
    xs_c, bm_c, cm_c = ctx_in
    A = -jnp.exp(a_log.astype(jnp.float32))

    def prep(u, d):
        dt = jax.nn.softplus(d.astype(jnp.float32) + dt_bias.astype(jnp.float32))
        return u * dt[..., None], dt * A

    flip = (lambda t: jnp.flip(t, axis=1)) if reverse else (lambda t: t)
    xc_in, ac = prep(xs_c, dt_raw_c)
    xc_in, ac, bc, cc = flip(xc_in), flip(ac), flip(bm_c), flip(cm_c)
    yc = None
    if with_ctx_out:
        h0 = jnp.zeros((xs_c.shape[0], N_HS, HD_S, D_STATE), jnp.float32)
        yc, h_ctx = _ssd_chunked(xc_in, ac, bc, cc, h0)
        yc = flip(yc)
    else:
        h_ctx = _ssd_final_state(xc_in, ac, bc)
    x_in, a = prep(xs, dt_raw)
    y, _ = _ssd_chunked(flip(x_in), flip(a), flip(bm), flip(cm), h_ctx)
    return flip(y), yc


def _ssd_mixer(z, xbc, dtf, dtb, zc, xbcc, dtfc, dtbc, conv_w, conv_b, dt_bias, a_log, d_skip, norm_w,
               with_ctx_out):
    lat = _ssd_inputs(xbc, conv_w, conv_b)
    ctx_in = _ssd_inputs(xbcc, conv_w, conv_b)
    yf, yfc = _ssd_direction(lat, ctx_in, dtf, dtfc, dt_bias[0], a_log[0], False, with_ctx_out)
    yb, ybc = _ssd_direction(lat, ctx_in, dtb, dtbc, dt_bias[1], a_log[1], True, with_ctx_out)
    d = d_skip.astype(jnp.float32)[:, None]

    def finish(y_sum, xs, zz):
        y = (y_sum + d * xs).reshape(zz.shape).astype(zz.dtype)
        return _rmsnorm(y * jax.nn.silu(zz), norm_w)

    y = finish(yf + yb, lat[0], z)
    yc = finish(yfc + ybc, ctx_in[0], zc) if with_ctx_out else None
    return y, yc


def _diff_core(q, k, v, lam):
    s = jnp.einsum('bqhid,bkhid->bhiqk', q, k).astype(jnp.float32) * HD_C ** -0.5
    p = jax.nn.softmax(s, axis=-1)
    a = (p[:, :, 0] - lam * p[:, :, 1]).astype(v.dtype)
    return jnp.einsum('bhqk,bkhe->bqhe', a, v)


def _diff_attn(q, k, v, qc, kc, vc, lam_q, lam_k, lam_init, norm_w, cos, sin, with_ctx_out):
    B, S, _ = q.shape
    nb = S // BLK
    lq = lam_q.astype(jnp.float32)
    lk = lam_k.astype(jnp.float32)
    lam = jnp.exp(jnp.sum(lq[0] * lk[0])) - jnp.exp(jnp.sum(lq[1] * lk[1])) + lam_init
    q = _rope(q.reshape(B, S, N_HC * 2, HD_C), cos, sin).reshape(B, S, N_HC, 2, HD_C)
    k = _rope(k.reshape(B, S, N_HC * 2, HD_C), cos, sin).reshape(B, S, N_HC, 2, HD_C)
    v = v.reshape(B, S, N_HC, 2 * HD_C)
    n_ctx = kc.shape[1]
    kc = kc.reshape(B, n_ctx, N_HC, 2, HD_C)
    vc = vc.reshape(B, n_ctx, N_HC, 2 * HD_C)
    k_all = jnp.concatenate([k, kc], axis=1)
    v_all = jnp.concatenate([v, vc], axis=1)
    qb = jnp.moveaxis(q.reshape(B, nb, BLK, N_HC, 2, HD_C), 1, 0)
    o = lax.map(lambda qblk: _diff_core(qblk, k_all, v_all, lam), qb)
    o = jnp.moveaxis(o, 0, 1).reshape(B, S, N_HC, 2 * HD_C)
    y = (_rmsnorm(o, norm_w) * (1 - lam_init)).reshape(B, S, D_C)
    yc = None
    if with_ctx_out:
        oc = _diff_core(qc.reshape(B, n_ctx, N_HC, 2, HD_C), kc, vc, lam)
        yc = (_rmsnorm(oc, norm_w) * (1 - lam_init)).reshape(B, n_ctx, D_C)
    return y, yc


def _swiglu(h, w_gu, w_down):
    g, u = jnp.split(h @ w_gu, 2, axis=-1)
    return (jax.nn.silu(g) * u) @ w_down


def _moe(h, w_r, b_r, we_gu, we_down):
    logits = (h @ w_r + b_r).astype(jnp.float32)
    top_v, top_i = lax.top_k(logits, TOP_K)
    top_w = jax.nn.softmax(top_v, axis=-1)
    gates = jnp.sum(jax.nn.one_hot(top_i, N_EXP, dtype=jnp.float32) * top_w[..., None], axis=-2).astype(h.dtype)
    y = jnp.zeros_like(h)
    for e in range(N_EXP):
        y = y + gates[..., e:e + 1] * _swiglu(h, we_gu[e], we_down[e])
    return y


def setup_inputs(seed: int = 0) -> dict:
    key = jax.random.key(seed)
    ks = jax.random.split(key, 28)
    f32 = jnp.float32

    def nrm(i, shape, scale):
        return jax.random.normal(ks[i], shape, f32) * scale

    dt0 = jnp.exp(jax.random.uniform(ks[10], (DEPTH, 2, N_HS), f32, math.log(1e-3), math.log(1e-1)))
    return {
        "x": nrm(0, (BATCH, SEQ, D_MODEL), 1.0),
        "c": nrm(1, (BATCH, D_MODEL), 1.0),
        "ctx": nrm(2, (BATCH, CTX_LEN, D_MODEL), 1.0),
        "c_ctx": nrm(3, (D_MODEL,), 1.0),
        "w_ada": nrm(4, (DEPTH, D_MODEL, 6 * D_MODEL), D_MODEL ** -0.5),
        "b_ada": nrm(5, (DEPTH, 6 * D_MODEL), 0.02),
        "w_in": nrm(6, (DEPTH, D_MODEL, D_IN), D_MODEL ** -0.5),
        "attn_sink": nrm(7, (DEPTH, N_HA), 0.5),
        "conv_w": nrm(8, (DEPTH, CONV_K, CONV_DIM), CONV_K ** -0.5),
        "conv_b": nrm(9, (DEPTH, CONV_DIM), 0.02),
        "dt_bias": dt0 + jnp.log(-jnp.expm1(-dt0)),
        "a_log": jnp.log(jax.random.uniform(ks[11], (DEPTH, 2, N_HS), f32, 1.0, 16.0)),
        "d_skip": 1.0 + nrm(12, (DEPTH, N_HS), 0.1),
        "ssm_norm_w": 1.0 + nrm(13, (DEPTH, D_INNER), 0.02),
        "lam_q": nrm(14, (DEPTH, 2, HD_C), 0.1),
        "lam_k": nrm(15, (DEPTH, 2, HD_C), 0.1),
        "diff_norm_w": 1.0 + nrm(16, (DEPTH, 2 * HD_C), 0.02),
        "w_out": nrm(17, (DEPTH, D_MIX, D_MODEL), D_MIX ** -0.5 * BETA),
        "ln1_g": 1.0 + nrm(18, (DEPTH, D_MODEL), 0.02),
        "ln1_b": nrm(19, (DEPTH, D_MODEL), 0.02),
        "ln2_g": 1.0 + nrm(20, (DEPTH, D_MODEL), 0.02),
        "ln2_b": nrm(21, (DEPTH, D_MODEL), 0.02),
        "ffn_w_gu": nrm(22, (N_DENSE, D_MODEL, 2 * D_FF), D_MODEL ** -0.5),
        "ffn_w_down": nrm(23, (N_DENSE, D_FF, D_MODEL), D_FF ** -0.5 * BETA),
        "router_w": nrm(24, (N_MOE, D_MODEL, N_EXP), D_MODEL ** -0.5),
        "router_b": nrm(25, (N_MOE, N_EXP), 0.01),
        "exp_w_gu": nrm(26, (N_MOE, N_EXP, D_MODEL, 2 * D_FF), D_MODEL ** -0.5),
        "exp_w_down": nrm(27, (N_MOE, N_EXP, D_FF, D_MODEL), D_FF ** -0.5 * BETA),
    }


def reference(x, c, ctx, c_ctx, w_ada, b_ada, w_in, attn_sink, conv_w, conv_b, dt_bias, a_log, d_skip,
              ssm_norm_w, lam_q, lam_k, diff_norm_w, w_out, ln1_g, ln1_b, ln2_g, ln2_b, ffn_w_gu, ffn_w_down,
              router_w, router_b, exp_w_gu, exp_w_down):
    B, S, _ = x.shape
    rows_n = S // GRID_W
    t_row = jnp.repeat(jnp.arange(rows_n, dtype=jnp.float32), GRID_W)
    t_col = jnp.tile(jnp.arange(GRID_W, dtype=jnp.float32), rows_n)
    cos_a, sin_a = _axial_rope(t_row, t_col, HD_A, x.dtype)
    cos_c, sin_c = _axial_rope(t_row, t_col, HD_C, x.dtype)
    c_act = jax.nn.silu(c)
    cc_act = jax.nn.silu(c_ctx)

    for li in range(DEPTH):
        ctx_out = li < DEPTH - 1
        lam_init = 0.8 - 0.6 * math.exp(-0.3 * li)
        if li % 2 == 0:
            ffn = lambda h, j=li // 2: _swiglu(h, ffn_w_gu[j], ffn_w_down[j])
        else:
            ffn = lambda h, j=li // 2: _moe(h, router_w[j], router_b[j], exp_w_gu[j], exp_w_down[j])

        sh1, sc1, g1, sh2, sc2, g2 = jnp.split((c_act @ w_ada[li] + b_ada[li])[:, None, :], 6, axis=-1)
        sh1c, sc1c, g1c, sh2c, sc2c, g2c = jnp.split(cc_act @ w_ada[li] + b_ada[li], 6, axis=-1)

        p = jnp.split(_modulate(x, sh1, sc1) @ w_in[li], IN_OFFSETS, axis=-1)
        pc = jnp.split(_modulate(ctx, sh1c, sc1c) @ w_in[li], IN_OFFSETS, axis=-1)
        ya, yac = _window_gqa(p[0], p[1], p[2], pc[0], pc[1], pc[2], attn_sink[li], cos_a, sin_a, ctx_out)
        yb, ybc = _ssd_mixer(p[3], p[4], p[5], p[6], pc[3], pc[4], pc[5], pc[6], conv_w[li], conv_b[li],
                             dt_bias[li], a_log[li], d_skip[li], ssm_norm_w[li], ctx_out)
        yd, ydc = _diff_attn(p[7], p[8], p[9], pc[7], pc[8], pc[9], lam_q[li], lam_k[li], lam_init,
                             diff_norm_w[li], cos_c, sin_c, ctx_out)
        mix = jnp.concatenate([ya, yb, yd], axis=-1) @ w_out[li]
        x = _post_ln(ALPHA * x + g1 * mix, ln1_g[li], ln1_b[li])
        x = _post_ln(ALPHA * x + g2 * ffn(_modulate(x, sh2, sc2)), ln2_g[li], ln2_b[li])

        if ctx_out:
            mix_c = jnp.concatenate([yac, ybc, ydc], axis=-1) @ w_out[li]
            ctx = _post_ln(ALPHA * ctx + g1c * mix_c, ln1_g[li], ln1_b[li])
            ctx = _post_ln(ALPHA * ctx + g2c * ffn(_modulate(ctx, sh2c, sc2c)), ln2_g[li], ln2_b[li])
    return x
```

```python
import functools
import math

import jax
import jax.numpy as jnp
from jax import lax
from jax.experimental import pallas as pl
from jax.experimental.pallas import tpu as pltpu

F32 = jnp.float32
BF16 = jnp.bfloat16
HIGHEST = lax.Precision.HIGHEST

D_MODEL = 1024
BATCH = 8
SEQ = 2048
DEPTH = 4
GRID_W = 64
CTX_LEN = 256
BLK = 128
WINDOW = 128
ROPE_BASE = 10000.0
EPS = 1e-6
HD_A = 64
N_HA = 6
N_KVA = 2
D_INNER = 384
HD_S = 64
N_HS = 6
D_STATE = 128
CONV_K = 5
CONV_DIM = 896
CHUNK = 128
N_HC = 4
HD_C = 32
D_C = 256
D_FF = 2816
N_EXP = 8
ALPHA = (2 * DEPTH) ** 0.25

LANES = 128
TILE = 256
ROWS = CTX_LEN + SEQ
N_TILES = ROWS // TILE
N_CHUNKS = ROWS // CHUNK
N_CTX_CHUNKS = CTX_LEN // CHUNK
FF_CHUNK = 256
N_FF_CHUNKS = D_FF // FF_CHUNK
D_IN_PAD = 2816
MOD_ROWS = 16
CTX_MOD_ROW = BATCH
VMEM_LIMIT = 56 * 1024 * 1024


def _cparams(sem, vmem=None):
    return pltpu.CompilerParams(dimension_semantics=sem, vmem_limit_bytes=vmem)


def _sigmoid(x):
    return 1.0 / (1.0 + jnp.exp(-x))


def _silu(x):
    return x * _sigmoid(x)


def _ln(x):
    mu = jnp.mean(x, axis=-1, keepdims=True)
    xc = x - mu
    var = jnp.mean(xc * xc, axis=-1, keepdims=True)
    return xc * lax.rsqrt(var + EPS)


def _dot(a, b, precision=None):
    return jnp.dot(a, b, preferred_element_type=F32, precision=precision)


def _dot_nt(a, b):
    return lax.dot_general(a, b, (((1,), (1,)), ((), ())), preferred_element_type=F32)


def _ada_kernel(c_ref, w_ref, b_ref, o_ref):
    act = _silu(c_ref[...])
    o_ref[...] = _dot(act.astype(BF16), w_ref[...].astype(BF16)) + b_ref[...]


def _ada_all(c_all, w_ada, b_ada):
    tn = 1536
    return pl.pallas_call(
        _ada_kernel,
        grid=(DEPTH, 6 * D_MODEL // tn),
        in_specs=[
            pl.BlockSpec((MOD_ROWS, D_MODEL), lambda l, j: (0, 0)),
            pl.BlockSpec((None, D_MODEL, tn), lambda l, j: (l, 0, j)),
            pl.BlockSpec((None, 1, tn), lambda l, j: (l, 0, j)),
        ],
        out_specs=pl.BlockSpec((None, MOD_ROWS, tn), lambda l, j: (l, 0, j)),
        out_shape=jax.ShapeDtypeStruct((DEPTH, MOD_ROWS, 6 * D_MODEL), F32),
        compiler_params=_cparams(("parallel", "parallel")),
        name="ada_ln",
    )(c_all, w_ada, b_ada.reshape(DEPTH, 1, 6 * D_MODEL))


def _mod_spec(li, piece, j0):
    return pl.BlockSpec(
        (None, None, None, 1, D_MODEL),
        lambda b, j: (li, jnp.where(j + j0 == 0, CTX_MOD_ROW, b), piece, 0, 0))


def _tok_spec(width, j0=0):
    return pl.BlockSpec((None, TILE, width), lambda b, j: (b, j + j0, 0))


def _full_spec(shape):
    n = len(shape)
    return pl.BlockSpec(shape, lambda *_: (0,) * n)


def _rope(v, cos, sin_signed, half):
    lane = lax.broadcasted_iota(jnp.int32, (v.shape[0], LANES), 1)
    first = (lane % (2 * half)) < half
    outs = []
    for s in range(v.shape[1] // LANES):
        blk = v[:, s * LANES:(s + 1) * LANES]
        swapped = jnp.where(first, pltpu.roll(blk, LANES - half, 1), pltpu.roll(blk, half, 1))
        outs.append(blk * cos + swapped * sin_signed)
    return jnp.concatenate(outs, axis=1)


def _inproj_kernel(x_ref, sh_ref, sc_ref, w_ref, cosa_ref, sina_ref, cosc_ref, sinc_ref,
                   qa_ref, ka_ref, va_ref, z_ref, xbc_ref, dt_ref, qc_ref, kc_ref, vc_ref):
    h = _ln(x_ref[...]) * (1.0 + sc_ref[...]) + sh_ref[...]
    hb = h.astype(BF16)

    def proj(lo, hi):
        return _dot(hb, w_ref[:, lo:hi])

    cosa, sina = cosa_ref[...], sina_ref[...]
    cosc, sinc = cosc_ref[...], sinc_ref[...]
    qa_ref[...] = (_rope(proj(0, 384), cosa, sina, HD_A // 2) * (HD_A ** -0.5)).astype(BF16)
    ka_ref[...] = _rope(proj(384, 512), cosa, sina, HD_A // 2).astype(BF16)
    va_ref[...] = proj(512, 640).astype(BF16)
    z_ref[...] = proj(640, 1024)
    xbc_ref[...] = proj(1024, 1920)
    qc_ref[...] = (_rope(proj(1920, 2176), cosc, sinc, HD_C // 2) * (HD_C ** -0.5)).astype(BF16)
    kc_ref[...] = _rope(proj(2176, 2432), cosc, sinc, HD_C // 2).astype(BF16)
    vc_ref[...] = proj(2432, 2688).astype(BF16)
    dt_ref[...] = proj(2688, 2816)


def _inproj(li, x_all, mod5, w_re, ropes):
    widths = (384, 128, 128, 384, CONV_DIM, LANES, D_C, D_C, D_C)
    dtypes = (BF16, BF16, BF16, F32, F32, F32, BF16, BF16, BF16)
    rope_spec = pl.BlockSpec((TILE, LANES), lambda b, j: (j, 0))
    return pl.pallas_call(
        _inproj_kernel,
        grid=(BATCH, N_TILES),
        in_specs=[_tok_spec(D_MODEL), _mod_spec(li, 0, 0), _mod_spec(li, 1, 0),
                  _full_spec((D_MODEL, D_IN_PAD)), rope_spec, rope_spec, rope_spec, rope_spec],
        out_specs=[_tok_spec(w) for w in widths],
        out_shape=[jax.ShapeDtypeStruct((BATCH, ROWS, w), d) for w, d in zip(widths, dtypes)],
        compiler_params=_cparams(("parallel", "parallel"), VMEM_LIMIT),
        name=f"inproj_l{li}",
    )(x_all, mod5, mod5, w_re, *ropes)


def _attn_a_kernel(*refs, local):
    if local:
        (sink_ref, q_ref, kp_ref, kc_ref, kn_ref, kx_ref,
         vp_ref, vc_ref, vn_ref, vx_ref, o_ref) = refs
    else:
        sink_ref, q_ref, kx_ref, vx_ref, o_ref = refs
    tq = q_ref.shape[0]
    q = q_ref[...].astype(F32)
    lane_lo = lax.broadcasted_iota(jnp.int32, (tq, LANES), 1) < HD_A
    parts = []
    for s in range(3):
        blk = q[:, s * LANES:(s + 1) * LANES]
        parts.append(jnp.where(lane_lo, blk, 0.0))
        parts.append(jnp.where(lane_lo, 0.0, blk))
    qst = jnp.concatenate(parts, axis=0).astype(BF16)
    if local:
        k = jnp.concatenate([kp_ref[...], kc_ref[...], kn_ref[...], kx_ref[...]], axis=0)
        v = jnp.concatenate([vp_ref[...], vc_ref[...], vn_ref[...], vx_ref[...]], axis=0)
        n = pl.program_id(1)
        r = lax.broadcasted_iota(jnp.int32, (tq, 3 * BLK), 0)
        c = lax.broadcasted_iota(jnp.int32, (tq, 3 * BLK), 1)
        prev_v = jnp.where(n > 0, c - r, -1)
        next_v = jnp.where(n < SEQ // BLK - 1, r - (c - 2 * BLK), -1)
        valid = jnp.where(c < BLK, prev_v, jnp.where(c < 2 * BLK, 0, next_v)) >= 0
    else:
        k, v = kx_ref[...], vx_ref[...]
    s_all = _dot_nt(qst, k)
    probs = []
    for rb in range(6):
        sink = sink_ref[rb]
        sh = s_all[rb * tq:(rb + 1) * tq]
        if local:
            s_loc = jnp.where(valid, sh[:, :3 * BLK], -jnp.inf)
            s_ctx = sh[:, 3 * BLK:]
            m = jnp.maximum(jnp.max(s_loc, axis=-1, keepdims=True),
                            jnp.max(s_ctx, axis=-1, keepdims=True))
            m = jnp.maximum(m, sink)
            p_loc = jnp.exp(s_loc - m)
            p_ctx = jnp.exp(s_ctx - m)
            den = (jnp.sum(p_loc, axis=-1, keepdims=True) + jnp.sum(p_ctx, axis=-1, keepdims=True)
                   + jnp.exp(sink - m))
            p = jnp.concatenate([p_loc, p_ctx], axis=1) / den
        else:
            m = jnp.maximum(jnp.max(sh, axis=-1, keepdims=True), sink)
            p_ctx = jnp.exp(sh - m)
            den = jnp.sum(p_ctx, axis=-1, keepdims=True) + jnp.exp(sink - m)
            p = p_ctx / den
        probs.append(p.astype(BF16))
    o_all = _dot(jnp.concatenate(probs, axis=0), v)
    outs = []
    for s in range(3):
        o0 = o_all[(2 * s) * tq:(2 * s + 1) * tq]
        o1 = o_all[(2 * s + 1) * tq:(2 * s + 2) * tq]
        outs.append(jnp.where(lane_lo, o0, o1))
    o_ref[...] = jnp.concatenate(outs, axis=1).astype(BF16)


def _attn_a(li, qa, ka, va, sink_p, with_ctx):
    nb = SEQ // BLK
    off = CTX_LEN // BLK
    smem = pl.BlockSpec(memory_space=pltpu.SMEM)

    def kv(shift):
        return pl.BlockSpec(
            (None, BLK, LANES), lambda b, n: (b, off + jnp.clip(n + shift, 0, nb - 1), 0))

    ctx_kv = pl.BlockSpec((None, CTX_LEN, LANES), lambda b, n: (b, 0, 0))
    q_spec = pl.BlockSpec((None, BLK, 384), lambda b, n: (b, off + n, 0))
    y_lat = pl.pallas_call(
        functools.partial(_attn_a_kernel, local=True),
        grid=(BATCH, nb),
        in_specs=[smem, q_spec, kv(-1), kv(0), kv(1), ctx_kv, kv(-1), kv(0), kv(1), ctx_kv],
        out_specs=pl.BlockSpec((None, BLK, 384), lambda b, n: (b, n, 0)),
        out_shape=jax.ShapeDtypeStruct((BATCH, SEQ, 384), BF16),
        compiler_params=_cparams(("parallel", "parallel")),
        name=f"attn_a_l{li}",
    )(sink_p, qa, ka, ka, ka, ka, va, va, va, va)
    if not with_ctx:
        return y_lat, None
    ctx_spec384 = pl.BlockSpec((None, CTX_LEN, 384), lambda b: (b, 0, 0))
    ctx_spec128 = pl.BlockSpec((None, CTX_LEN, LANES), lambda b: (b, 0, 0))
    y_ctx = pl.pallas_call(
        functools.partial(_attn_a_kernel, local=False),
        grid=(BATCH,),
        in_specs=[smem, ctx_spec384, ctx_spec128, ctx_spec128],
        out_specs=ctx_spec384,
        out_shape=jax.ShapeDtypeStruct((BATCH, CTX_LEN, 384), BF16),
        compiler_params=_cparams(("parallel",)),
        name=f"attn_a_ctx_l{li}",
    )(sink_p, qa, ka, va)
    return y_lat, y_ctx


def _attn_c_kernel(lq_ref, lk_ref, nw_ref, q_ref, k_ref, v_ref, o_ref, *, lam_init):
    tq = q_ref.shape[0]
    e = jnp.exp(jnp.sum(lq_ref[...] * lk_ref[...], axis=-1, keepdims=True))
    lam = e[0:1] - e[1:2] + lam_init
    q = q_ref[...].astype(F32)
    k = k_ref[...]
    v = v_ref[...]
    lane = lax.broadcasted_iota(jnp.int32, (tq, D_C), 1)
    acc = jnp.zeros((tq, D_C), F32)
    for h in range(N_HC):
        base = h * 2 * HD_C
        in0 = jnp.where(lane >= base, lane - base, 2 * HD_C) < HD_C
        in1 = jnp.where(lane >= base + HD_C, lane - base - HD_C, HD_C) < HD_C
        qst = jnp.concatenate([jnp.where(in0, q, 0.0), jnp.where(in1, q, 0.0)], axis=0).astype(BF16)
        s = _dot_nt(qst, k)
        m = jnp.max(s, axis=-1, keepdims=True)
        p = jnp.exp(s - m)
        r = 1.0 / jnp.sum(p, axis=-1, keepdims=True)
        a = (p[:tq] * r[:tq] - p[tq:] * (r[tq:] * lam)).astype(BF16)
        o = _dot(a, v)
        inh = jnp.where(lane >= base, lane - base, 2 * HD_C) < 2 * HD_C
        oh = jnp.where(inh, o, 0.0)
        ms = jnp.sum(oh * oh, axis=-1, keepdims=True) * (1.0 / (2 * HD_C))
        acc = acc + oh * lax.rsqrt(ms + EPS)
    o_ref[...] = (acc * nw_ref[...] * (1.0 - lam_init)).astype(BF16)


def _attn_c(li, qc, kc, vc, lam_q, lam_k, nw_row, lam_init, with_ctx):
    tq = TILE
    off = CTX_LEN // tq
    small = [_full_spec((2, HD_C)), _full_spec((2, HD_C)), _full_spec((1, D_C))]
    kern = functools.partial(_attn_c_kernel, lam_init=lam_init)
    all_kv = pl.BlockSpec((None, ROWS, D_C), lambda b, n: (b, 0, 0))
    y_lat = pl.pallas_call(
        kern,
        grid=(BATCH, SEQ // tq),
        in_specs=small + [pl.BlockSpec((None, tq, D_C), lambda b, n: (b, off + n, 0)), all_kv, all_kv],
        out_specs=pl.BlockSpec((None, tq, D_C), lambda b, n: (b, n, 0)),
        out_shape=jax.ShapeDtypeStruct((BATCH, SEQ, D_C), BF16),
        compiler_params=_cparams(("parallel", "parallel"), VMEM_LIMIT),
        name=f"attn_c_l{li}",
    )(lam_q, lam_k, nw_row, qc, kc, vc)
    if not with_ctx:
        return y_lat, None
    ctx_spec = pl.BlockSpec((None, CTX_LEN, D_C), lambda b: (b, 0, 0))
    small1 = [_full_spec((2, HD_C)), _full_spec((2, HD_C)), _full_spec((1, D_C))]
    y_ctx = pl.pallas_call(
        kern,
        grid=(BATCH,),
        in_specs=small1 + [ctx_spec, ctx_spec, ctx_spec],
        out_specs=ctx_spec,
        out_shape=jax.ShapeDtypeStruct((BATCH, CTX_LEN, D_C), BF16),
        compiler_params=_cparams(("parallel",)),
        name=f"attn_c_ctx_l{li}",
    )(lam_q, lam_k, nw_row, qc, kc, vc)
    return y_lat, y_ctx


def _ssd_kernel(xbc_ref, z_ref, dt_ref, cw_ref, cb_ref, dtb_ref, alog_ref, dsk_ref, nw_ref,
                ef_ref, eb_ref, y_ref, u_scr, yacc_scr, st_scr):
    cw = cw_ref[...]
    cb = cb_ref[...]
    dsk = dsk_ref[...]
    pad = 8
    for c in range(N_CHUNKS):
        r0 = CHUNK * c
        zeros = jnp.zeros((pad, CONV_DIM), F32)
        if c in (0, N_CTX_CHUNKS):
            blk = jnp.concatenate([zeros, xbc_ref[r0:r0 + CHUNK + pad, :]], axis=0)
        elif c in (N_CTX_CHUNKS - 1, N_CHUNKS - 1):
            blk = jnp.concatenate([xbc_ref[r0 - pad:r0 + CHUNK, :], zeros], axis=0)
        else:
            blk = xbc_ref[r0 - pad:r0 + CHUNK + pad, :]
        acc = None
        for j in range(CONV_K):
            sh = (CONV_K // 2 - j) % (CHUNK + 2 * pad)
            rolled = blk if sh == 0 else pltpu.roll(blk, sh, 0)
            term = rolled[pad:pad + CHUNK, :] * cw[j:j + 1, :]
            acc = term if acc is None else acc + term
        u = _silu(acc + cb)
        u_scr[r0:r0 + CHUNK, :] = u
        yacc_scr[r0:r0 + CHUNK, :] = u[:, :D_INNER] * dsk

    st_scr[...] = jnp.zeros(st_scr.shape, F32)
    ri = lax.broadcasted_iota(jnp.int32, (CHUNK, CHUNK), 0)
    ci = lax.broadcasted_iota(jnp.int32, (CHUNK, CHUNK), 1)
    lower = ri >= ci
    upper = ri <= ci
    tri = (jnp.where(lower, 1.0, 0.0).astype(F32), jnp.where(upper, 1.0, 0.0).astype(F32))
    a_row = -jnp.exp(alog_ref[...])
    dtb = dtb_ref[...]
    lane_lo = ci < HD_S

    def chunk_dir(c, d):
        r0 = pl.multiple_of(c * CHUNK, CHUNK)
        u = u_scr[pl.ds(r0, CHUNK), :]
        xs = u[:, :D_INNER]
        bm = u[:, D_INNER:D_INNER + 2 * D_STATE]
        cm = u[:, D_INNER + 2 * D_STATE:]
        vv = dt_ref[pl.ds(r0, CHUNK), :] + dtb
        dt = jnp.maximum(vv, 0.0) + jnp.log(1.0 + jnp.exp(-jnp.abs(vv)))
        a = dt * a_row
        cum = _dot(tri[d], a, HIGHEST)
        cum_t = cum.T
        tot = cum[CHUNK - 1:CHUNK, :] if d == 0 else cum[0:1, :]
        expand = ef_ref[...] if d == 0 else eb_ref[...]
        dt_x = _dot(dt, expand, HIGHEST)
        dec_out_x = _dot(jnp.exp(cum), expand, HIGHEST)
        dec_st_x = _dot(jnp.exp(tot - cum), expand, HIGHEST)
        etot_x = dec_out_x[CHUNK - 1:CHUNK, :] if d == 0 else dec_out_x[0:1, :]
        xdt = xs * dt_x
        xdt_b = xdt.astype(BF16)
        xst_b = (xdt * dec_st_x).astype(BF16)
        cb_ = [cm[:, g * D_STATE:(g + 1) * D_STATE].astype(BF16) for g in range(2)]
        bb_ = [bm[:, g * D_STATE:(g + 1) * D_STATE].astype(BF16) for g in range(2)]
        bt_ = [bm[:, g * D_STATE:(g + 1) * D_STATE].T.astype(BF16) for g in range(2)]
        cbm = [_dot_nt(cb_[g], bb_[g]) for g in range(2)]
        mask = lower if d == 0 else upper
        for j in range(3):
            sl = slice(j * LANES, (j + 1) * LANES)
            parts = []
            for h in (2 * j, 2 * j + 1):
                hl = h + N_HS * d
                seg = jnp.where(mask, cum[:, hl:hl + 1] - cum_t[hl:hl + 1, :], -jnp.inf)
                sc = (cbm[h // 3] * jnp.exp(seg)).astype(BF16)
                parts.append(_dot(sc, xdt_b[:, sl]))
            y_diag = jnp.where(lane_lo, parts[0], parts[1])
            st = st_scr[d, j]
            st_b = st.astype(BF16)
            g0, g1 = (2 * j) // 3, (2 * j + 1) // 3
            if g0 == g1:
                y_off = _dot(cb_[g0], st_b)
                upd = _dot(bt_[g0], xst_b[:, sl])
            else:
                y_off = jnp.where(lane_lo, _dot(cb_[g0], st_b), _dot(cb_[g1], st_b))
                upd = jnp.where(lane_lo, _dot(bt_[g0], xst_b[:, sl]), _dot(bt_[g1], xst_b[:, sl]))
            yacc_scr[pl.ds(r0, CHUNK), sl] += y_diag + y_off * dec_out_x[:, sl]
            st_scr[d, j] = etot_x[:, sl] * st + upd

    def body(i, carry):
        chunk_dir(i, 0)
        chunk_dir(jnp.where(i < N_CTX_CHUNKS, N_CTX_CHUNKS - 1 - i, N_CHUNKS + N_CTX_CHUNKS - 1 - i), 1)
        return carry

    lax.fori_loop(0, N_CHUNKS, body, 0)

    nw = nw_ref[...]
    for t in range(N_TILES):
        rows = slice(t * TILE, (t + 1) * TILE)
        y = yacc_scr[rows, :] * _silu(z_ref[rows, :])
        ms = jnp.mean(y * y, axis=-1, keepdims=True)
        y_ref[rows, :] = (y * lax.rsqrt(ms + EPS) * nw).astype(BF16)


def _ssd(li, xbc, z, dt, cw, cb, dtb, alog, dsk, nw, ef, eb):
    def seq(width):
        return pl.BlockSpec((None, ROWS, width), lambda b: (b, 0, 0))

    return pl.pallas_call(
        _ssd_kernel,
        grid=(BATCH,),
        in_specs=[seq(CONV_DIM), seq(D_INNER), seq(LANES),
                  _full_spec((8, CONV_DIM)), _full_spec((1, CONV_DIM)),
                  _full_spec((1, LANES)), _full_spec((1, LANES)),
                  _full_spec((1, D_INNER)), _full_spec((1, D_INNER)),
                  _full_spec((LANES, D_INNER)), _full_spec((LANES, D_INNER))],
        out_specs=seq(D_INNER),
        out_shape=jax.ShapeDtypeStruct((BATCH, ROWS, D_INNER), BF16),
        scratch_shapes=[pltpu.VMEM((ROWS, CONV_DIM), F32), pltpu.VMEM((ROWS, D_INNER), F32),
                        pltpu.VMEM((2, 3, D_STATE, LANES), F32)],
        compiler_params=_cparams(("parallel",), VMEM_LIMIT),
        name=f"ssd_l{li}",
    )(xbc, z, dt, cw, cb, dtb, alog, dsk, nw, ef, eb)


def _top2_gates(logits):
    lane = lax.broadcasted_iota(jnp.int32, logits.shape, 1).astype(F32)
    neg = -jnp.inf
    lg = jnp.where(lane < N_EXP, logits, neg)
    m1 = jnp.max(lg, axis=-1, keepdims=True)
    i1 = jnp.min(jnp.where(lg == m1, lane, float(LANES)), axis=-1, keepdims=True)
    lg2 = jnp.where(lane == i1, neg, lg)
    m2 = jnp.max(lg2, axis=-1, keepdims=True)
    i2 = jnp.min(jnp.where(lg2 == m2, lane, float(LANES)), axis=-1, keepdims=True)
    e2 = jnp.exp(m2 - m1)
    w1 = 1.0 / (1.0 + e2)
    w2 = e2 / (1.0 + e2)
    return jnp.where(lane == i1, w1, 0.0) + jnp.where(lane == i2, w2, 0.0)


def _outproj_kernel(*refs, moe, ctx_split):
    if ctx_split:
        ya_l, ya_c, yb_ref, yd_l, yd_c = refs[:5]
        refs = refs[5:]
    else:
        ya_l, yb_ref, yd_l = refs[:3]
        refs = refs[3:]
    x_ref, g1_ref, lg_ref, lb_ref, sh_ref, sc_ref, wo_ref = refs[:7]
    refs = refs[7:]
    if moe:
        wr_ref, br_ref, x1_ref, h2_ref, gates_ref = refs
    else:
        x1_ref, h2_ref = refs
    if ctx_split:
        is_ctx = pl.program_id(1) == 0
        ya = jnp.where(is_ctx, ya_c[...], ya_l[...])
        yd = jnp.where(is_ctx, yd_c[...], yd_l[...])
    else:
        ya, yd = ya_l[...], yd_l[...]
    mix = (_dot(ya, wo_ref[0:384, :]) + _dot(yb_ref[...], wo_ref[384:768, :])
           + _dot(yd, wo_ref[768:1024, :]))
    x1 = _ln(ALPHA * x_ref[...] + g1_ref[...] * mix) * lg_ref[...] + lb_ref[...]
    x1_ref[...] = x1
    h2 = _ln(x1) * (1.0 + sc_ref[...]) + sh_ref[...]
    h2_ref[...] = h2.astype(BF16)
    if moe:
        logits = _dot(h2, wr_ref[...], HIGHEST) + br_ref[...]
        gates_ref[...] = _top2_gates(logits)


def _outproj(li, ya_lat, ya_ctx, yb, yd_lat, yd_ctx, x_all, mod5, lg, lb, wo, moe, wr, br, j0):
    ctx_split = j0 == 0
    nt = N_TILES - j0
    off = CTX_LEN // TILE
    lat = lambda w: pl.BlockSpec((None, TILE, w), lambda b, j: (b, jnp.maximum(j + j0 - off, 0), 0))
    ctx = lambda w: pl.BlockSpec((None, TILE, w), lambda b, j: (b, 0, 0))
    row = _full_spec((1, D_MODEL))
    if ctx_split:
        ins = [ya_lat, ya_ctx, yb, yd_lat, yd_ctx]
        specs = [lat(384), ctx(384), _tok_spec(D_INNER, j0), lat(D_C), ctx(D_C)]
    else:
        ins = [ya_lat, yb, yd_lat]
        specs = [lat(384), _tok_spec(D_INNER, j0), lat(D_C)]
    ins += [x_all, mod5, lg, lb, mod5, mod5, wo]
    specs += [_tok_spec(D_MODEL, j0), _mod_spec(li, 2, j0), row, row,
              _mod_spec(li, 3, j0), _mod_spec(li, 4, j0), _full_spec((D_MODEL, D_MODEL))]
    outs = [jax.ShapeDtypeStruct((BATCH, ROWS, D_MODEL), F32),
            jax.ShapeDtypeStruct((BATCH, ROWS, D_MODEL), BF16)]
    out_specs = [_tok_spec(D_MODEL, j0), _tok_spec(D_MODEL, j0)]
    if moe:
        ins += [wr, br]
        specs += [_full_spec((D_MODEL, LANES)), _full_spec((1, LANES))]
        outs.append(jax.ShapeDtypeStruct((BATCH, ROWS, LANES), F32))
        out_specs.append(_tok_spec(LANES, j0))
    return pl.pallas_call(
        functools.partial(_outproj_kernel, moe=moe, ctx_split=ctx_split),
        grid=(BATCH, nt),
        in_specs=specs,
        out_specs=out_specs,
        out_shape=outs,
        compiler_params=_cparams(("parallel", "parallel"), VMEM_LIMIT),
        name=f"outproj_l{li}",
    )(*ins)


def _swiglu_acc(h2, wgu_ref, wd_ref):
    acc = jnp.zeros((h2.shape[0], D_MODEL), F32)
    for c in range(N_FF_CHUNKS):
        gu = _dot(h2, wgu_ref[c])
        act = (_silu(gu[:, :FF_CHUNK]) * gu[:, FF_CHUNK:]).astype(BF16)
        acc = acc + _dot(act, wd_ref[c])
    return acc


def _ffn_dense_kernel(x1_ref, h2_ref, g2_ref, lg_ref, lb_ref, wgu_ref, wd_ref, o_ref):
    y = _swiglu_acc(h2_ref[...], wgu_ref, wd_ref)
    o_ref[...] = _ln(ALPHA * x1_ref[...] + g2_ref[...] * y) * lg_ref[...] + lb_ref[...]


def _ffn_moe_kernel(x1_ref, h2_ref, gates_ref, g2_ref, lg_ref, lb_ref, wgu_ref, wd_ref, o_ref, acc_scr):
    e = pl.program_id(2)

    @pl.when(e == 0)
    def _():
        acc_scr[...] = jnp.zeros(acc_scr.shape, F32)

    gates = gates_ref[...]
    lane = lax.broadcasted_iota(jnp.int32, gates.shape, 1)
    gate = jnp.sum(jnp.where(lane == e, gates, 0.0), axis=-1, keepdims=True)
    acc_scr[...] += gate * _swiglu_acc(h2_ref[...], wgu_ref, wd_ref)

    @pl.when(e == N_EXP - 1)
    def _():
        o_ref[...] = _ln(ALPHA * x1_ref[...] + g2_ref[...] * acc_scr[...]) * lg_ref[...] + lb_ref[...]


def _ffn(li, x1, h2, gates, mod5, lg, lb, wgu, wd, j0):
    nt = N_TILES - j0
    row = _full_spec((1, D_MODEL))
    out_rows = ROWS if j0 == 0 else SEQ
    if gates is None:
        return pl.pallas_call(
            _ffn_dense_kernel,
            grid=(BATCH, nt),
            in_specs=[_tok_spec(D_MODEL, j0), _tok_spec(D_MODEL, j0), _mod_spec(li, 5, j0), row, row,
                      _full_spec((N_FF_CHUNKS, D_MODEL, 2 * FF_CHUNK)),
                      _full_spec((N_FF_CHUNKS, FF_CHUNK, D_MODEL))],
            out_specs=_tok_spec(D_MODEL),
            out_shape=jax.ShapeDtypeStruct((BATCH, out_rows, D_MODEL), F32),
            compiler_params=_cparams(("parallel", "parallel"), VMEM_LIMIT),
            name=f"ffn_dense_l{li}",
        )(x1, h2, mod5, lg, lb, wgu, wd)
    tok3 = lambda w, jj: pl.BlockSpec((None, TILE, w), lambda b, j, e: (b, j + jj, 0))
    row3 = pl.BlockSpec((1, D_MODEL), lambda b, j, e: (0, 0))
    mod3 = pl.BlockSpec(
        (None, None, None, 1, D_MODEL),
        lambda b, j, e: (li, jnp.where(j + j0 == 0, CTX_MOD_ROW, b), 5, 0, 0))
    return pl.pallas_call(
        _ffn_moe_kernel,
        grid=(BATCH, nt, N_EXP),
        in_specs=[tok3(D_MODEL, j0), tok3(D_MODEL, j0), tok3(LANES, j0), mod3, row3, row3,
                  pl.BlockSpec((None, N_FF_CHUNKS, D_MODEL, 2 * FF_CHUNK), lambda b, j, e: (e, 0, 0, 0)),
                  pl.BlockSpec((None, N_FF_CHUNKS, FF_CHUNK, D_MODEL), lambda b, j, e: (e, 0, 0, 0))],
        out_specs=tok3(D_MODEL, 0),
        out_shape=jax.ShapeDtypeStruct((BATCH, out_rows, D_MODEL), F32),
        scratch_shapes=[pltpu.VMEM((TILE, D_MODEL), F32)],
        compiler_params=_cparams(("parallel", "parallel", "arbitrary"), VMEM_LIMIT),
        name=f"ffn_moe_l{li}",
    )(x1, h2, gates, mod5, lg, lb, wgu, wd)


def _chunk_gu(w_gu):
    lead = w_gu.shape[:-2]
    w = w_gu.astype(BF16).reshape(lead + (D_MODEL, 2, N_FF_CHUNKS, FF_CHUNK))
    n = len(lead)
    w = jnp.transpose(w, tuple(range(n)) + (n + 2, n, n + 1, n + 3))
    return w.reshape(lead + (N_FF_CHUNKS, D_MODEL, 2 * FF_CHUNK))


def _chunk_down(w_down):
    lead = w_down.shape[:-2]
    return w_down.astype(BF16).reshape(lead + (N_FF_CHUNKS, FF_CHUNK, D_MODEL))


def _rope_tables(dim):
    rows_n = SEQ // GRID_W
    t_row = jnp.repeat(jnp.arange(rows_n, dtype=F32), GRID_W)
    t_col = jnp.tile(jnp.arange(GRID_W, dtype=F32), rows_n)
    quarter = dim // 4
    inv = ROPE_BASE ** (-jnp.arange(quarter, dtype=F32) / quarter)
    ang = jnp.concatenate([t_row[:, None] * inv, t_col[:, None] * inv], axis=-1)
    cos, sin = jnp.cos(ang), jnp.sin(ang)
    reps = LANES // dim
    cos_f = jnp.tile(jnp.concatenate([cos, cos], axis=-1), (1, reps))
    sin_s = jnp.tile(jnp.concatenate([-sin, sin], axis=-1), (1, reps))
    cos_f = jnp.concatenate([jnp.ones((CTX_LEN, LANES), F32), cos_f], axis=0)
    sin_s = jnp.concatenate([jnp.zeros((CTX_LEN, LANES), F32), sin_s], axis=0)
    return cos_f, sin_s


def _head_expand(offset):
    rows = jnp.arange(LANES)[:, None]
    cols = jnp.arange(D_INNER)[None, :]
    return (rows == cols // HD_S + offset).astype(F32)


def kernel(x, c, ctx, c_ctx, w_ada, b_ada, w_in, attn_sink, conv_w, conv_b, dt_bias, a_log, d_skip,
           ssm_norm_w, lam_q, lam_k, diff_norm_w, w_out, ln1_g, ln1_b, ln2_g, ln2_b, ffn_w_gu,
           ffn_w_down, router_w, router_b, exp_w_gu, exp_w_down):
    x_all = jnp.concatenate([ctx, x], axis=1)
    c_all = jnp.concatenate(
        [c, c_ctx[None, :], jnp.zeros((MOD_ROWS - BATCH - 1, D_MODEL), F32)], axis=0)
    mod5 = _ada_all(c_all, w_ada, b_ada).reshape(DEPTH, MOD_ROWS, 6, 1, D_MODEL)
    ropes = _rope_tables(HD_A) + _rope_tables(HD_C)
    ef, eb = _head_expand(0), _head_expand(N_HS)

    for li in range(DEPTH):
        ctx_out = li < DEPTH - 1
        j0 = 0 if ctx_out else CTX_LEN // TILE
        lam_init = 0.8 - 0.6 * math.exp(-0.3 * li)
        w = w_in[li]
        wq = w[:, :384].reshape(D_MODEL, N_KVA, 3, HD_A).transpose(0, 2, 1, 3).reshape(D_MODEL, 384)
        w_re = jnp.concatenate(
            [wq, w[:, 384:1920], w[:, 1932:2700], w[:, 1920:1932],
             jnp.zeros((D_MODEL, D_IN_PAD - 2700), F32)], axis=1).astype(BF16)
        qa, ka, va, z, xbc, dt, qc, kc, vc = _inproj(li, x_all, mod5, w_re, ropes)

        sink_p = attn_sink[li].reshape(N_KVA, 3).T.reshape(N_HA)
        ya_lat, ya_ctx = _attn_a(li, qa, ka, va, sink_p, ctx_out)
        nw_c = jnp.tile(diff_norm_w[li], N_HC)[None, :]
        yd_lat, yd_ctx = _attn_c(li, qc, kc, vc, lam_q[li], lam_k[li], nw_c, lam_init, ctx_out)
        cw = jnp.pad(conv_w[li], ((0, 8 - CONV_K), (0, 0)))
        pad12 = lambda a: jnp.pad(a.reshape(1, 2 * N_HS), ((0, 0), (0, LANES - 2 * N_HS)))
        yb = _ssd(li, xbc, z, dt, cw, conv_b[li][None, :], pad12(dt_bias[li]), pad12(a_log[li]),
                  jnp.repeat(d_skip[li], HD_S)[None, :], ssm_norm_w[li][None, :], ef, eb)

        wo = w_out[li]
        wo_a = wo[:384].reshape(N_KVA, 3, HD_A, D_MODEL).transpose(1, 0, 2, 3).reshape(384, D_MODEL)
        wo_re = jnp.concatenate([wo_a, wo[384:]], axis=0).astype(BF16)
        moe = li % 2 == 1
        fj = li // 2
        if moe:
            wr = jnp.pad(router_w[fj], ((0, 0), (0, LANES - N_EXP)))
            br = jnp.pad(router_b[fj], (0, LANES - N_EXP))[None, :]
        else:
            wr = br = None
        res = _outproj(li, ya_lat, ya_ctx, yb, yd_lat, yd_ctx, x_all, mod5, ln1_g[li][None, :],
                       ln1_b[li][None, :], wo_re, moe, wr, br, j0)
        if moe:
            x1, h2, gates = res
            wgu, wd = _chunk_gu(exp_w_gu[fj]), _chunk_down(exp_w_down[fj])
        else:
            x1, h2 = res
            gates = None
            wgu, wd = _chunk_gu(ffn_w_gu[fj]), _chunk_down(ffn_w_down[fj])
        x_all = _ffn(li, x1, h2, gates, mod5, ln2_g[li][None, :], ln2_b[li][None, :], wgu, wd, j0)
    return x_all
```

```python
import functools
import math

import jax
import jax.numpy as jnp
from jax import lax
from jax.experimental import pallas as pl
from jax.experimental.pallas import tpu as pltpu

F32 = jnp.float32
BF16 = jnp.bfloat16
HIGHEST = lax.Precision.HIGHEST

D_MODEL = 1024
BATCH = 8
SEQ = 2048
DEPTH = 4
GRID_W = 64
CTX_LEN = 256
BLK = 128
WINDOW = 128
ROPE_BASE = 10000.0
EPS = 1e-6
HD_A = 64
N_HA = 6
N_KVA = 2
D_INNER = 384
HD_S = 64
N_HS = 6
D_STATE = 128
CONV_K = 5
CONV_DIM = 896
CHUNK = 128
N_HC = 4
HD_C = 32
D_C = 256
D_FF = 2816
N_EXP = 8
ALPHA = (2 * DEPTH) ** 0.25

LANES = 128
TILE = 256
ROWS = CTX_LEN + SEQ
N_TILES = ROWS // TILE
N_CHUNKS = ROWS // CHUNK
N_CTX_CHUNKS = CTX_LEN // CHUNK
FF_CHUNK = 256
N_FF_CHUNKS = D_FF // FF_CHUNK
D_IN_PAD = 2816
MOD_ROWS = 16
CTX_MOD_ROW = BATCH
VMEM_LIMIT = 56 * 1024 * 1024


def _cparams(sem, vmem=None):
    return pltpu.CompilerParams(dimension_semantics=sem, vmem_limit_bytes=vmem)


def _sigmoid(x):
    return 1.0 / (1.0 + jnp.exp(-x))


def _silu(x):
    return x * _sigmoid(x)


def _ln(x):
    mu = jnp.mean(x, axis=-1, keepdims=True)
    xc = x - mu
    var = jnp.mean(xc * xc, axis=-1, keepdims=True)
    return xc * lax.rsqrt(var + EPS)


def _dot(a, b, precision=None):
    return jnp.dot(a, b, preferred_element_type=F32, precision=precision)


def _dot_nt(a, b):
    return lax.dot_general(a, b, (((1,), (1,)), ((), ())), preferred_element_type=F32)


def _ada_kernel(c_ref, w_ref, b_ref, o_ref):
    act = _silu(c_ref[...])
    o_ref[...] = _dot(act.astype(BF16), w_ref[...].astype(BF16)) + b_ref[...]


def _ada_all(c_all, w_ada, b_ada):
    tn = 1536
    return pl.pallas_call(
        _ada_kernel,
        grid=(DEPTH, 6 * D_MODEL // tn),
        in_specs=[
            pl.BlockSpec((MOD_ROWS, D_MODEL), lambda l, j: (0, 0)),
            pl.BlockSpec((None, D_MODEL, tn), lambda l, j: (l, 0, j)),
            pl.BlockSpec((None, 1, tn), lambda l, j: (l, 0, j)),
        ],
        out_specs=pl.BlockSpec((None, MOD_ROWS, tn), lambda l, j: (l, 0, j)),
        out_shape=jax.ShapeDtypeStruct((DEPTH, MOD_ROWS, 6 * D_MODEL), F32),
        compiler_params=_cparams(("parallel", "parallel")),
        name="ada_ln",
    )(c_all, w_ada, b_ada.reshape(DEPTH, 1, 6 * D_MODEL))


def _mod_spec(li, piece, j0):
    return pl.BlockSpec(
        (None, None, None, 1, D_MODEL),
        lambda b, j: (li, jnp.where(j + j0 == 0, CTX_MOD_ROW, b), piece, 0, 0))


def _tok_spec(width, j0=0):
    return pl.BlockSpec((None, TILE, width), lambda b, j: (b, j + j0, 0))


def _full_spec(shape):
    n = len(shape)
    return pl.BlockSpec(shape, lambda *_: (0,) * n)


def _rope(v, cos, sin_signed, half):
    lane = lax.broadcasted_iota(jnp.int32, (v.shape[0], LANES), 1)
    first = (lane % (2 * half)) < half
    outs = []
    for s in range(v.shape[1] // LANES):
        blk = v[:, s * LANES:(s + 1) * LANES]
        swapped = jnp.where(first, pltpu.roll(blk, LANES - half, 1), pltpu.roll(blk, half, 1))
        outs.append(blk * cos + swapped * sin_signed)
    return jnp.concatenate(outs, axis=1)


def _inproj_kernel(x_ref, sh_ref, sc_ref, w_ref, cosa_ref, sina_ref, cosc_ref, sinc_ref,
                   qa_ref, ka_ref, va_ref, z_ref, xbc_ref, dt_ref, qc_ref, kc_ref, vc_ref):
    h = _ln(x_ref[...]) * (1.0 + sc_ref[...]) + sh_ref[...]
    hb = h.astype(BF16)

    def proj(lo, hi):
        return _dot(hb, w_ref[:, lo:hi])

    cosa, sina = cosa_ref[...], sina_ref[...]
    cosc, sinc = cosc_ref[...], sinc_ref[...]
    qa_ref[...] = (_rope(proj(0, 384), cosa, sina, HD_A // 2) * (HD_A ** -0.5)).astype(BF16)
    ka_ref[...] = _rope(proj(384, 512), cosa, sina, HD_A // 2).astype(BF16)
    va_ref[...] = proj(512, 640).astype(BF16)
    z_ref[...] = proj(640, 1024)
    xbc_ref[...] = proj(1024, 1920)
    qc_ref[...] = (_rope(proj(1920, 2176), cosc, sinc, HD_C // 2) * (HD_C ** -0.5)).astype(BF16)
    kc_ref[...] = _rope(proj(2176, 2432), cosc, sinc, HD_C // 2).astype(BF16)
    vc_ref[...] = proj(2432, 2688).astype(BF16)
    dt_ref[...] = proj(2688, 2816)


def _inproj(li, x_all, mod5, w_re, ropes):
    widths = (384, 128, 128, 384, CONV_DIM, LANES, D_C, D_C, D_C)
    dtypes = (BF16, BF16, BF16, F32, F32, F32, BF16, BF16, BF16)
    rope_spec = pl.BlockSpec((TILE, LANES), lambda b, j: (j, 0))
    return pl.pallas_call(
        _inproj_kernel,
        grid=(BATCH, N_TILES),
        in_specs=[_tok_spec(D_MODEL), _mod_spec(li, 0, 0), _mod_spec(li, 1, 0),
                  _full_spec((D_MODEL, D_IN_PAD)), rope_spec, rope_spec, rope_spec, rope_spec],
        out_specs=[_tok_spec(w) for w in widths],
        out_shape=[jax.ShapeDtypeStruct((BATCH, ROWS, w), d) for w, d in zip(widths, dtypes)],
        compiler_params=_cparams(("parallel", "parallel"), VMEM_LIMIT),
        name=f"inproj_l{li}",
    )(x_all, mod5, mod5, w_re, *ropes)


def _attn_a_kernel(*refs, local):
    if local:
        (sink_ref, q_ref, kp_ref, kc_ref, kn_ref, kx_ref,
         vp_ref, vc_ref, vn_ref, vx_ref, o_ref) = refs
    else:
        sink_ref, q_ref, kx_ref, vx_ref, o_ref = refs
    tq = q_ref.shape[0]
    q = q_ref[...].astype(F32)
    lane_lo = lax.broadcasted_iota(jnp.int32, (tq, LANES), 1) < HD_A
    parts = []
    for s in range(3):
        blk = q[:, s * LANES:(s + 1) * LANES]
        parts.append(jnp.where(lane_lo, blk, 0.0))
        parts.append(jnp.where(lane_lo, 0.0, blk))
    qst = jnp.concatenate(parts, axis=0).astype(BF16)
    if local:
        k = jnp.concatenate([kp_ref[...], kc_ref[...], kn_ref[...], kx_ref[...]], axis=0)
        v = jnp.concatenate([vp_ref[...], vc_ref[...], vn_ref[...], vx_ref[...]], axis=0)
        n = pl.program_id(1)
        r = lax.broadcasted_iota(jnp.int32, (tq, 3 * BLK), 0)
        c = lax.broadcasted_iota(jnp.int32, (tq, 3 * BLK), 1)
        prev_v = jnp.where(n > 0, c - r, -1)
        next_v = jnp.where(n < SEQ // BLK - 1, r - (c - 2 * BLK), -1)
        valid = jnp.where(c < BLK, prev_v, jnp.where(c < 2 * BLK, 0, next_v)) >= 0
    else:
        k, v = kx_ref[...], vx_ref[...]
    s_all = _dot_nt(qst, k)
    probs = []
    for rb in range(6):
        sink = sink_ref[rb]
        sh = s_all[rb * tq:(rb + 1) * tq]
        if local:
            s_loc = jnp.where(valid, sh[:, :3 * BLK], -jnp.inf)
            s_ctx = sh[:, 3 * BLK:]
            m = jnp.maximum(jnp.max(s_loc, axis=-1, keepdims=True),
                            jnp.max(s_ctx, axis=-1, keepdims=True))
            m = jnp.maximum(m, sink)
            p_loc = jnp.exp(s_loc - m)
            p_ctx = jnp.exp(s_ctx - m)
            den = (jnp.sum(p_loc, axis=-1, keepdims=True) + jnp.sum(p_ctx, axis=-1, keepdims=True)
                   + jnp.exp(sink - m))
            p = jnp.concatenate([p_loc, p_ctx], axis=1) / den
        else:
            m = jnp.maximum(jnp.max(sh, axis=-1, keepdims=True), sink)
            p_ctx = jnp.exp(sh - m)
            den = jnp.sum(p_ctx, axis=-1, keepdims=True) + jnp.exp(sink - m)
            p = p_ctx / den
        probs.append(p.astype(BF16))
    o_all = _dot(jnp.concatenate(probs, axis=0), v)
    outs = []
    for s in range(3):
        o0 = o_all[(2 * s) * tq:(2 * s + 1) * tq]
        o1 = o_all[(2 * s + 1) * tq:(2 * s + 2) * tq]
        outs.append(jnp.where(lane_lo, o0, o1))
    o_ref[...] = jnp.concatenate(outs, axis=1).astype(BF16)


def _attn_a(li, qa, ka, va, sink_p, with_ctx):
    nb = SEQ // BLK
    off = CTX_LEN // BLK
    smem = pl.BlockSpec(memory_space=pltpu.SMEM)

    def kv(shift):
        return pl.BlockSpec(
            (None, BLK, LANES), lambda b, n: (b, off + jnp.clip(n + shift, 0, nb - 1), 0))

    ctx_kv = pl.BlockSpec((None, CTX_LEN, LANES), lambda b, n: (b, 0, 0))
    q_spec = pl.BlockSpec((None, BLK, 384), lambda b, n: (b, off + n, 0))
    y_lat = pl.pallas_call(
        functools.partial(_attn_a_kernel, local=True),
        grid=(BATCH, nb),
        in_specs=[smem, q_spec, kv(-1), kv(0), kv(1), ctx_kv, kv(-1), kv(0), kv(1), ctx_kv],
        out_specs=pl.BlockSpec((None, BLK, 384), lambda b, n: (b, n, 0)),
        out_shape=jax.ShapeDtypeStruct((BATCH, SEQ, 384), BF16),
        compiler_params=_cparams(("parallel", "parallel")),
        name=f"attn_a_l{li}",
    )(sink_p, qa, ka, ka, ka, ka, va, va, va, va)
    if not with_ctx:
        return y_lat, None
    ctx_spec384 = pl.BlockSpec((None, CTX_LEN, 384), lambda b: (b, 0, 0))
    ctx_spec128 = pl.BlockSpec((None, CTX_LEN, LANES), lambda b: (b, 0, 0))
    y_ctx = pl.pallas_call(
        functools.partial(_attn_a_kernel, local=False),
        grid=(BATCH,),
        in_specs=[smem, ctx_spec384, ctx_spec128, ctx_spec128],
        out_specs=ctx_spec384,
        out_shape=jax.ShapeDtypeStruct((BATCH, CTX_LEN, 384), BF16),
        compiler_params=_cparams(("parallel",)),
        name=f"attn_a_ctx_l{li}",
    )(sink_p, qa, ka, va)
    return y_lat, y_ctx


def _attn_c_kernel(lq_ref, lk_ref, nw_ref, q_ref, k_ref, v_ref, o_ref, *, lam_init):
    tq = q_ref.shape[0]
    e = jnp.exp(jnp.sum(lq_ref[...] * lk_ref[...], axis=-1, keepdims=True))
    lam = e[0:1] - e[1:2] + lam_init
    q = q_ref[...].astype(F32)
    k = k_ref[...]
    v = v_ref[...]
    lane = lax.broadcasted_iota(jnp.int32, (tq, D_C), 1)
    acc = jnp.zeros((tq, D_C), F32)
    for h in range(N_HC):
        base = h * 2 * HD_C
        in0 = jnp.where(lane >= base, lane - base, 2 * HD_C) < HD_C
        in1 = jnp.where(lane >= base + HD_C, lane - base - HD_C, HD_C) < HD_C
        qst = jnp.concatenate([jnp.where(in0, q, 0.0), jnp.where(in1, q, 0.0)], axis=0).astype(BF16)
        s = _dot_nt(qst, k)
        m = jnp.max(s, axis=-1, keepdims=True)
        p = jnp.exp(s - m)
        r = 1.0 / jnp.sum(p, axis=-1, keepdims=True)
        a = (p[:tq] * r[:tq] - p[tq:] * (r[tq:] * lam)).astype(BF16)
        o = _dot(a, v)
        inh = jnp.where(lane >= base, lane - base, 2 * HD_C) < 2 * HD_C
        oh = jnp.where(inh, o, 0.0)
        ms = jnp.sum(oh * oh, axis=-1, keepdims=True) * (1.0 / (2 * HD_C))
        acc = acc + oh * lax.rsqrt(ms + EPS)
    o_ref[...] = (acc * nw_ref[...] * (1.0 - lam_init)).astype(BF16)


def _attn_c(li, qc, kc, vc, lam_q, lam_k, nw_row, lam_init, with_ctx):
    tq = TILE
    off = CTX_LEN // tq
    small = [_full_spec((2, HD_C)), _full_spec((2, HD_C)), _full_spec((1, D_C))]
    kern = functools.partial(_attn_c_kernel, lam_init=lam_init)
    all_kv = pl.BlockSpec((None, ROWS, D_C), lambda b, n: (b, 0, 0))
    y_lat = pl.pallas_call(
        kern,
        grid=(BATCH, SEQ // tq),
        in_specs=small + [pl.BlockSpec((None, tq, D_C), lambda b, n: (b, off + n, 0)), all_kv, all_kv],
        out_specs=pl.BlockSpec((None, tq, D_C), lambda b, n: (b, n, 0)),
        out_shape=jax.ShapeDtypeStruct((BATCH, SEQ, D_C), BF16),
        compiler_params=_cparams(("parallel", "parallel"), VMEM_LIMIT),
        name=f"attn_c_l{li}",
    )(lam_q, lam_k, nw_row, qc, kc, vc)
    if not with_ctx:
        return y_lat, None
    ctx_spec = pl.BlockSpec((None, CTX_LEN, D_C), lambda b: (b, 0, 0))
    small1 = [_full_spec((2, HD_C)), _full_spec((2, HD_C)), _full_spec((1, D_C))]
    y_ctx = pl.pallas_call(
        kern,
        grid=(BATCH,),
        in_specs=small1 + [ctx_spec, ctx_spec, ctx_spec],
        out_specs=ctx_spec,
        out_shape=jax.ShapeDtypeStruct((BATCH, CTX_LEN, D_C), BF16),
        compiler_params=_cparams(("parallel",)),
        name=f"attn_c_ctx_l{li}",
    )(lam_q, lam_k, nw_row, qc, kc, vc)
    return y_lat, y_ctx


def _ssd_kernel(xbc_ref, z_ref, dt_ref, cw_ref, cb_ref, dtb_ref, alog_ref, dsk_ref, nw_ref,
                ef_ref, eb_ref, y_ref, u_scr, yacc_scr, st_scr):
    cw = cw_ref[...]
    cb = cb_ref[...]
    dsk = dsk_ref[...]
    pad = 8
    for c in range(N_CHUNKS):
        r0 = CHUNK * c
        zeros = jnp.zeros((pad, CONV_DIM), F32)
        if c in (0, N_CTX_CHUNKS):
            blk = jnp.concatenate([zeros, xbc_ref[r0:r0 + CHUNK + pad, :]], axis=0)
        elif c in (N_CTX_CHUNKS - 1, N_CHUNKS - 1):
            blk = jnp.concatenate([xbc_ref[r0 - pad:r0 + CHUNK, :], zeros], axis=0)
        else:
            blk = xbc_ref[r0 - pad:r0 + CHUNK + pad, :]
        acc = None
        for j in range(CONV_K):
            sh = (CONV_K // 2 - j) % (CHUNK + 2 * pad)
            rolled = blk if sh == 0 else pltpu.roll(blk, sh, 0)
            term = rolled[pad:pad + CHUNK, :] * cw[j:j + 1, :]
            acc = term if acc is None else acc + term
        u = _silu(acc + cb)
        u_scr[r0:r0 + CHUNK, :] = u
        yacc_scr[r0:r0 + CHUNK, :] = u[:, :D_INNER] * dsk

    st_scr[...] = jnp.zeros(st_scr.shape, F32)
    ri = lax.broadcasted_iota(jnp.int32, (CHUNK, CHUNK), 0)
    ci = lax.broadcasted_iota(jnp.int32, (CHUNK, CHUNK), 1)
    lower = ri >= ci
    upper = ri <= ci
    tri = (jnp.where(lower, 1.0, 0.0).astype(F32), jnp.where(upper, 1.0, 0.0).astype(F32))
    a_row = -jnp.exp(alog_ref[...])
    dtb = dtb_ref[...]
    lane_lo = ci < HD_S

    def chunk_dir(c, d):
        r0 = pl.multiple_of(c * CHUNK, CHUNK)
        u = u_scr[pl.ds(r0, CHUNK), :]
        xs = u[:, :D_INNER]
        bm = u[:, D_INNER:D_INNER + 2 * D_STATE]
        cm = u[:, D_INNER + 2 * D_STATE:]
        vv = dt_ref[pl.ds(r0, CHUNK), :] + dtb
        dt = jnp.maximum(vv, 0.0) + jnp.log(1.0 + jnp.exp(-jnp.abs(vv)))
        a = dt * a_row
        cum = _dot(tri[d], a, HIGHEST)
        cum_t = cum.T
        tot = cum[CHUNK - 1:CHUNK, :] if d == 0 else cum[0:1, :]
        expand = ef_ref[...] if d == 0 else eb_ref[...]
        dt_x = _dot(dt, expand, HIGHEST)
        dec_out_x = _dot(jnp.exp(cum), expand, HIGHEST)
        dec_st_x = _dot(jnp.exp(tot - cum), expand, HIGHEST)
        etot_x = dec_out_x[CHUNK - 1:CHUNK, :] if d == 0 else dec_out_x[0:1, :]
        xdt = xs * dt_x
        xdt_b = xdt.astype(BF16)
        xst_b = (xdt * dec_st_x).astype(BF16)
        cb_ = [cm[:, g * D_STATE:(g + 1) * D_STATE].astype(BF16) for g in range(2)]
        bb_ = [bm[:, g * D_STATE:(g + 1) * D_STATE].astype(BF16) for g in range(2)]
        bt_ = [bm[:, g * D_STATE:(g + 1) * D_STATE].T.astype(BF16) for g in range(2)]
        cbm = [_dot_nt(cb_[g], bb_[g]) for g in range(2)]
        mask = lower if d == 0 else upper
        for j in range(3):
            sl = slice(j * LANES, (j + 1) * LANES)
            parts = []
            for h in (2 * j, 2 * j + 1):
                hl = h + N_HS * d
                seg = jnp.where(mask, cum[:, hl:hl + 1] - cum_t[hl:hl + 1, :], -jnp.inf)
                sc = (cbm[h // 3] * jnp.exp(seg)).astype(BF16)
                parts.append(_dot(sc, xdt_b[:, sl]))
            y_diag = jnp.where(lane_lo, parts[0], parts[1])
            st = st_scr[d, j]
            st_b = st.astype(BF16)
            g0, g1 = (2 * j) // 3, (2 * j + 1) // 3
            if g0 == g1:
                y_off = _dot(cb_[g0], st_b)
                upd = _dot(bt_[g0], xst_b[:, sl])
            else:
                y_off = jnp.where(lane_lo, _dot(cb_[g0], st_b), _dot(cb_[g1], st_b))
                upd = jnp.where(lane_lo, _dot(bt_[g0], xst_b[:, sl]), _dot(bt_[g1], xst_b[:, sl]))
            yacc_scr[pl.ds(r0, CHUNK), sl] += y_diag + y_off * dec_out_x[:, sl]
            st_scr[d, j] = etot_x[:, sl] * st + upd

    def body(i, carry):
        chunk_dir(i, 0)
        chunk_dir(jnp.where(i < N_CTX_CHUNKS, N_CTX_CHUNKS - 1 - i, N_CHUNKS + N_CTX_CHUNKS - 1 - i), 1)
        return carry

    lax.fori_loop(0, N_CHUNKS, body, 0)

    nw = nw_ref[...]
    for t in range(N_TILES):
        rows = slice(t * TILE, (t + 1) * TILE)
        y = yacc_scr[rows, :] * _silu(z_ref[rows, :])
        ms = jnp.mean(y * y, axis=-1, keepdims=True)
        y_ref[rows, :] = (y * lax.rsqrt(ms + EPS) * nw).astype(BF16)


def _ssd(li, xbc, z, dt, cw, cb, dtb, alog, dsk, nw, ef, eb):
    def seq(width):
        return pl.BlockSpec((None, ROWS, width), lambda b: (b, 0, 0))

    return pl.pallas_call(
        _ssd_kernel,
        grid=(BATCH,),
        in_specs=[seq(CONV_DIM), seq(D_INNER), seq(LANES),
                  _full_spec((8, CONV_DIM)), _full_spec((1, CONV_DIM)),
                  _full_spec((1, LANES)), _full_spec((1, LANES)),
                  _full_spec((1, D_INNER)), _full_spec((1, D_INNER)),
                  _full_spec((LANES, D_INNER)), _full_spec((LANES, D_INNER))],
        out_specs=seq(D_INNER),
        out_shape=jax.ShapeDtypeStruct((BATCH, ROWS, D_INNER), BF16),
        scratch_shapes=[pltpu.VMEM((ROWS, CONV_DIM), F32), pltpu.VMEM((ROWS, D_INNER), F32),
                        pltpu.VMEM((2, 3, D_STATE, LANES), F32)],
        compiler_params=_cparams(("parallel",), VMEM_LIMIT),
        name=f"ssd_l{li}",
    )(xbc, z, dt, cw, cb, dtb, alog, dsk, nw, ef, eb)


def _top2_gates(logits):
    lane = lax.broadcasted_iota(jnp.int32, logits.shape, 1).astype(F32)
    neg = -jnp.inf
    lg = jnp.where(lane < N_EXP, logits, neg)
    m1 = jnp.max(lg, axis=-1, keepdims=True)
    i1 = jnp.min(jnp.where(lg == m1, lane, float(LANES)), axis=-1, keepdims=True)
    lg2 = jnp.where(lane == i1, neg, lg)
    m2 = jnp.max(lg2, axis=-1, keepdims=True)
    i2 = jnp.min(jnp.where(lg2 == m2, lane, float(LANES)), axis=-1, keepdims=True)
    e2 = jnp.exp(m2 - m1)
    return lane, i1, i2, 1.0 / (1.0 + e2), e2 / (1.0 + e2)


R_E1, R_E2, R_W1, R_W2, R_RANK1, R_RANK2 = range(6)


def _route_record(logits, cnt_scr):
    lane, i1, i2, w1, w2 = _top2_gates(logits)
    rows = logits.shape[0]
    hit1, hit2 = lane == i1, lane == i2
    onehot = jnp.where(hit1, 1.0, 0.0) + jnp.where(hit2, 1.0, 0.0)
    ri = lax.broadcasted_iota(jnp.int32, (rows, rows), 0)
    ci = lax.broadcasted_iota(jnp.int32, (rows, rows), 1)
    earlier = jnp.where(ri > ci, 1.0, 0.0).astype(BF16)
    before = _dot(earlier, onehot.astype(BF16)) + cnt_scr[...]
    rank1 = jnp.sum(jnp.where(hit1, before, 0.0), axis=-1, keepdims=True)
    rank2 = jnp.sum(jnp.where(hit2, before, 0.0), axis=-1, keepdims=True)
    cnt_scr[...] += jnp.sum(onehot, axis=0, keepdims=True)
    rec = jnp.zeros(logits.shape, F32)
    for idx, val in ((R_E1, i1), (R_E2, i2), (R_W1, w1), (R_W2, w2), (R_RANK1, rank1), (R_RANK2, rank2)):
        rec = jnp.where(lane == float(idx), val, rec)
    return rec


def _outproj_kernel(*refs, moe, ctx_split):
    if ctx_split:
        ya_l, ya_c, yb_ref, yd_l, yd_c = refs[:5]
        refs = refs[5:]
    else:
        ya_l, yb_ref, yd_l = refs[:3]
        refs = refs[3:]
    x_ref, g1_ref, lg_ref, lb_ref, sh_ref, sc_ref, wo_ref = refs[:7]
    refs = refs[7:]
    if moe:
        wr_ref, br_ref, x1_ref, h2_ref, route_ref, counts_ref, cnt_scr = refs

        @pl.when((pl.program_id(0) == 0) & (pl.program_id(1) == 0))
        def _():
            cnt_scr[...] = jnp.zeros(cnt_scr.shape, F32)
    else:
        x1_ref, h2_ref = refs
    if ctx_split:
        is_ctx = pl.program_id(1) == 0
        ya = jnp.where(is_ctx, ya_c[...], ya_l[...])
        yd = jnp.where(is_ctx, yd_c[...], yd_l[...])
    else:
        ya, yd = ya_l[...], yd_l[...]
    mix = (_dot(ya, wo_ref[0:384, :]) + _dot(yb_ref[...], wo_ref[384:768, :])
           + _dot(yd, wo_ref[768:1024, :]))
    x1 = _ln(ALPHA * x_ref[...] + g1_ref[...] * mix) * lg_ref[...] + lb_ref[...]
    x1_ref[...] = x1
    h2 = _ln(x1) * (1.0 + sc_ref[...]) + sh_ref[...]
    h2_ref[...] = h2.astype(h2_ref.dtype)
    if moe:
        logits = _dot(h2, wr_ref[...], HIGHEST) + br_ref[...]
        route_ref[...] = _route_record(logits, cnt_scr)
        counts_ref[...] = cnt_scr[...]


def _outproj(li, ya_lat, ya_ctx, yb, yd_lat, yd_ctx, x_all, mod5, lg, lb, wo, moe, wr, br, j0):
    ctx_split = j0 == 0
    nt = N_TILES - j0
    off = CTX_LEN // TILE
    lat = lambda w: pl.BlockSpec((None, TILE, w), lambda b, j: (b, jnp.maximum(j + j0 - off, 0), 0))
    ctx = lambda w: pl.BlockSpec((None, TILE, w), lambda b, j: (b, 0, 0))
    row = _full_spec((1, D_MODEL))
    if ctx_split:
        ins = [ya_lat, ya_ctx, yb, yd_lat, yd_ctx]
        specs = [lat(384), ctx(384), _tok_spec(D_INNER, j0), lat(D_C), ctx(D_C)]
    else:
        ins = [ya_lat, yb, yd_lat]
        specs = [lat(384), _tok_spec(D_INNER, j0), lat(D_C)]
    ins += [x_all, mod5, lg, lb, mod5, mod5, wo]
    specs += [_tok_spec(D_MODEL, j0), _mod_spec(li, 2, j0), row, row,
              _mod_spec(li, 3, j0), _mod_spec(li, 4, j0), _full_spec((D_MODEL, D_MODEL))]
    outs = [jax.ShapeDtypeStruct((BATCH, nt * TILE, D_MODEL), F32),
            jax.ShapeDtypeStruct((BATCH, nt * TILE, D_MODEL), F32 if moe else BF16)]
    out_specs = [_tok_spec(D_MODEL), _tok_spec(D_MODEL)]
    scratch = []
    if moe:
        ins += [wr, br]
        specs += [_full_spec((D_MODEL, LANES)), _full_spec((1, LANES))]
        outs += [jax.ShapeDtypeStruct((BATCH, nt * TILE, LANES), F32),
                 jax.ShapeDtypeStruct((1, LANES), F32)]
        out_specs += [_tok_spec(LANES), _full_spec((1, LANES))]
        scratch = [pltpu.VMEM((1, LANES), F32)]
    sem = ("arbitrary", "arbitrary") if moe else ("parallel", "parallel")
    return pl.pallas_call(
        functools.partial(_outproj_kernel, moe=moe, ctx_split=ctx_split),
        grid=(BATCH, nt),
        in_specs=specs,
        out_specs=out_specs,
        out_shape=outs,
        scratch_shapes=scratch,
        compiler_params=_cparams(sem, VMEM_LIMIT),
        name=f"outproj_l{li}",
    )(*ins)


def _swiglu_acc(h2, wgu_ref, wd_ref):
    acc = jnp.zeros((h2.shape[0], D_MODEL), F32)
    for c in range(N_FF_CHUNKS):
        gu = _dot(h2, wgu_ref[c])
        act = (_silu(gu[:, :FF_CHUNK]) * gu[:, FF_CHUNK:]).astype(BF16)
        acc = acc + _dot(act, wd_ref[c])
    return acc


def _ffn_dense_kernel(x1_ref, h2_ref, g2_ref, lg_ref, lb_ref, wgu_ref, wd_ref, o_ref):
    y = _swiglu_acc(h2_ref[...], wgu_ref, wd_ref)
    o_ref[...] = _ln(ALPHA * x1_ref[...] + g2_ref[...] * y) * lg_ref[...] + lb_ref[...]


def _ffn_dense(li, x1, h2, mod5, lg, lb, wgu, wd):
    row = _full_spec((1, D_MODEL))
    return pl.pallas_call(
        _ffn_dense_kernel,
        grid=(BATCH, N_TILES),
        in_specs=[_tok_spec(D_MODEL), _tok_spec(D_MODEL), _mod_spec(li, 5, 0), row, row,
                  _full_spec((N_FF_CHUNKS, D_MODEL, 2 * FF_CHUNK)),
                  _full_spec((N_FF_CHUNKS, FF_CHUNK, D_MODEL))],
        out_specs=_tok_spec(D_MODEL),
        out_shape=jax.ShapeDtypeStruct((BATCH, ROWS, D_MODEL), F32),
        compiler_params=_cparams(("parallel", "parallel"), VMEM_LIMIT),
        name=f"ffn_dense_l{li}",
    )(x1, h2, mod5, lg, lb, wgu, wd)


EXP_TILE = 256


def _slot_row_copy(src_ref, src_row, dst_ref, dst_row, sem):
    return pltpu.make_async_copy(src_ref.at[pl.ds(src_row, 1), :], dst_ref.at[pl.ds(dst_row, 1), :], sem)


def _dispatch_kernel(pos_ref, h2_ref, xs_in_ref, xs_ref, sem):
    del xs_in_ref
    base = (pl.program_id(0) * pl.num_programs(1) + pl.program_id(1)) * (2 * TILE)

    def issue(r, carry):
        for k in range(2):
            _slot_row_copy(h2_ref, r, xs_ref, pos_ref[base + 2 * r + k], sem).start()
        return carry

    lax.fori_loop(0, TILE, issue, 0, unroll=8)

    def drain(r, carry):
        for k in range(2):
            _slot_row_copy(h2_ref, r, xs_ref, pos_ref[base + 2 * r + k], sem).wait()
        return carry

    lax.fori_loop(0, TILE, drain, 0, unroll=8)


def _grouped_kernel(tile_e_ref, n_used_ref, xs_ref, wgu_ref, wd_ref, ys_ref):
    del tile_e_ref
    live = pl.program_id(0) < n_used_ref[0]

    @pl.when(live)
    def _():
        ys_ref[...] = _swiglu_acc(xs_ref[...].astype(BF16), wgu_ref, wd_ref)

    @pl.when(jnp.logical_not(live))
    def _():
        ys_ref[...] = jnp.zeros(ys_ref.shape, F32)


def _combine_kernel(pos_ref, ys_ref, route_ref, x1_ref, g2_ref, lg_ref, lb_ref, o_ref, ybuf, sem):
    base = (pl.program_id(0) * pl.num_programs(1) + pl.program_id(1)) * (2 * TILE)

    def issue(r, carry):
        for k in range(2):
            _slot_row_copy(ys_ref, pos_ref[base + 2 * r + k], ybuf.at[k], r, sem).start()
        return carry

    lax.fori_loop(0, TILE, issue, 0, unroll=8)

    def drain(r, carry):
        for k in range(2):
            _slot_row_copy(ys_ref, pos_ref[base + 2 * r + k], ybuf.at[k], r, sem).wait()
        return carry

    lax.fori_loop(0, TILE, drain, 0, unroll=8)
    route = route_ref[...]
    lane = lax.broadcasted_iota(jnp.int32, route.shape, 1)
    w1 = jnp.sum(jnp.where(lane == R_W1, route, 0.0), axis=-1, keepdims=True)
    w2 = jnp.sum(jnp.where(lane == R_W2, route, 0.0), axis=-1, keepdims=True)
    y = w1 * ybuf[0] + w2 * ybuf[1]
    o_ref[...] = _ln(ALPHA * x1_ref[...] + g2_ref[...] * y) * lg_ref[...] + lb_ref[...]


def _ffn_routed(li, x1, h2, route, counts, mod5, lg, lb, wgu, wd, j0):
    nt = N_TILES - j0
    n_tok = BATCH * nt * TILE
    n_tiles = 2 * n_tok // EXP_TILE + N_EXP
    n_slots = n_tiles * EXP_TILE
    cnt = counts[0, :N_EXP].astype(jnp.int32)
    padded = (cnt + EXP_TILE - 1) // EXP_TILE * EXP_TILE
    ends = jnp.cumsum(padded)
    offs = ends - padded
    n_used = ends[-1:] // EXP_TILE
    tile_start = jnp.minimum(jnp.arange(n_tiles), n_used - 1) * EXP_TILE
    tile_e = jnp.sum((tile_start[:, None] >= ends[None, :]).astype(jnp.int32), axis=1)
    rec = route.reshape(n_tok, LANES)
    experts = rec[:, R_E1:R_E2 + 1].astype(jnp.int32)
    ranks = rec[:, R_RANK1:R_RANK2 + 1].astype(jnp.int32)
    off_of = jnp.sum(jnp.where(experts[..., None] == jnp.arange(N_EXP), offs, 0), axis=-1)
    pos = (off_of + ranks).reshape(2 * n_tok)

    h2_flat = h2.reshape(n_tok, D_MODEL)
    any_spec = pl.BlockSpec(memory_space=pl.ANY)
    xs = pl.pallas_call(
        _dispatch_kernel,
        grid_spec=pltpu.PrefetchScalarGridSpec(
            num_scalar_prefetch=1,
            grid=(BATCH, nt),
            in_specs=[pl.BlockSpec((TILE, D_MODEL), lambda b, j, pos: (b * nt + j, 0)), any_spec],
            out_specs=any_spec,
            scratch_shapes=[pltpu.SemaphoreType.DMA(())]),
        out_shape=jax.ShapeDtypeStruct((n_slots, D_MODEL), F32),
        input_output_aliases={2: 0},
        compiler_params=_cparams(("arbitrary", "arbitrary")),
        name=f"moe_dispatch_l{li}",
    )(pos, h2_flat, jnp.zeros((n_slots, D_MODEL), F32))

    used = lambda i, te, nu: jnp.minimum(i, nu[0] - 1)
    ys = pl.pallas_call(
        _grouped_kernel,
        grid_spec=pltpu.PrefetchScalarGridSpec(
            num_scalar_prefetch=2,
            grid=(n_tiles,),
            in_specs=[
                pl.BlockSpec((EXP_TILE, D_MODEL), lambda i, te, nu: (used(i, te, nu), 0)),
                pl.BlockSpec((None, N_FF_CHUNKS, D_MODEL, 2 * FF_CHUNK), lambda i, te, nu: (te[i], 0, 0, 0)),
                pl.BlockSpec((None, N_FF_CHUNKS, FF_CHUNK, D_MODEL), lambda i, te, nu: (te[i], 0, 0, 0))],
            out_specs=pl.BlockSpec((EXP_TILE, D_MODEL), lambda i, te, nu: (i, 0))),
        out_shape=jax.ShapeDtypeStruct((n_slots, D_MODEL), F32),
        compiler_params=_cparams(("arbitrary",), VMEM_LIMIT),
        name=f"moe_grouped_l{li}",
    )(tile_e, n_used, xs, wgu, wd)

    row = pl.BlockSpec((1, D_MODEL), lambda b, j, pos: (0, 0))
    tok = lambda w: pl.BlockSpec((None, TILE, w), lambda b, j, pos: (b, j, 0))
    mod = pl.BlockSpec((None, None, None, 1, D_MODEL),
                       lambda b, j, pos: (li, jnp.where(j + j0 == 0, CTX_MOD_ROW, b), 5, 0, 0))
    return pl.pallas_call(
        _combine_kernel,
        grid_spec=pltpu.PrefetchScalarGridSpec(
            num_scalar_prefetch=1,
            grid=(BATCH, nt),
            in_specs=[any_spec, tok(LANES), tok(D_MODEL), mod, row, row],
            out_specs=tok(D_MODEL),
            scratch_shapes=[pltpu.VMEM((2, TILE, D_MODEL), F32), pltpu.SemaphoreType.DMA(())]),
        out_shape=jax.ShapeDtypeStruct((BATCH, nt * TILE, D_MODEL), F32),
        compiler_params=_cparams(("arbitrary", "arbitrary"), VMEM_LIMIT),
        name=f"moe_combine_l{li}",
    )(pos, ys, route, x1, mod5, lg, lb)


def _chunk_gu(w_gu):
    lead = w_gu.shape[:-2]
    w = w_gu.astype(BF16).reshape(lead + (D_MODEL, 2, N_FF_CHUNKS, FF_CHUNK))
    n = len(lead)
    w = jnp.transpose(w, tuple(range(n)) + (n + 2, n, n + 1, n + 3))
    return w.reshape(lead + (N_FF_CHUNKS, D_MODEL, 2 * FF_CHUNK))


def _chunk_down(w_down):
    lead = w_down.shape[:-2]
    return w_down.astype(BF16).reshape(lead + (N_FF_CHUNKS, FF_CHUNK, D_MODEL))


def _rope_tables(dim):
    rows_n = SEQ // GRID_W
    t_row = jnp.repeat(jnp.arange(rows_n, dtype=F32), GRID_W)
    t_col = jnp.tile(jnp.arange(GRID_W, dtype=F32), rows_n)
    quarter = dim // 4
    inv = ROPE_BASE ** (-jnp.arange(quarter, dtype=F32) / quarter)
    ang = jnp.concatenate([t_row[:, None] * inv, t_col[:, None] * inv], axis=-1)
    cos, sin = jnp.cos(ang), jnp.sin(ang)
    reps = LANES // dim
    cos_f = jnp.tile(jnp.concatenate([cos, cos], axis=-1), (1, reps))
    sin_s = jnp.tile(jnp.concatenate([-sin, sin], axis=-1), (1, reps))
    cos_f = jnp.concatenate([jnp.ones((CTX_LEN, LANES), F32), cos_f], axis=0)
    sin_s = jnp.concatenate([jnp.zeros((CTX_LEN, LANES), F32), sin_s], axis=0)
    return cos_f, sin_s


def _head_expand(offset):
    rows = jnp.arange(LANES)[:, None]
    cols = jnp.arange(D_INNER)[None, :]
    return (rows == cols // HD_S + offset).astype(F32)


def kernel(x, c, ctx, c_ctx, w_ada, b_ada, w_in, attn_sink, conv_w, conv_b, dt_bias, a_log, d_skip,
           ssm_norm_w, lam_q, lam_k, diff_norm_w, w_out, ln1_g, ln1_b, ln2_g, ln2_b, ffn_w_gu,
           ffn_w_down, router_w, router_b, exp_w_gu, exp_w_down):
    x_all = jnp.concatenate([ctx, x], axis=1)
    c_all = jnp.concatenate(
        [c, c_ctx[None, :], jnp.zeros((MOD_ROWS - BATCH - 1, D_MODEL), F32)], axis=0)
    mod5 = _ada_all(c_all, w_ada, b_ada).reshape(DEPTH, MOD_ROWS, 6, 1, D_MODEL)
    ropes = _rope_tables(HD_A) + _rope_tables(HD_C)
    ef, eb = _head_expand(0), _head_expand(N_HS)

    for li in range(DEPTH):
        ctx_out = li < DEPTH - 1
        j0 = 0 if ctx_out else CTX_LEN // TILE
        lam_init = 0.8 - 0.6 * math.exp(-0.3 * li)
        w = w_in[li]
        wq = w[:, :384].reshape(D_MODEL, N_KVA, 3, HD_A).transpose(0, 2, 1, 3).reshape(D_MODEL, 384)
        w_re = jnp.concatenate(
            [wq, w[:, 384:1920], w[:, 1932:2700], w[:, 1920:1932],
             jnp.zeros((D_MODEL, D_IN_PAD - 2700), F32)], axis=1).astype(BF16)
        qa, ka, va, z, xbc, dt, qc, kc, vc = _inproj(li, x_all, mod5, w_re, ropes)

        sink_p = attn_sink[li].reshape(N_KVA, 3).T.reshape(N_HA)
        ya_lat, ya_ctx = _attn_a(li, qa, ka, va, sink_p, ctx_out)
        nw_c = jnp.tile(diff_norm_w[li], N_HC)[None, :]
        yd_lat, yd_ctx = _attn_c(li, qc, kc, vc, lam_q[li], lam_k[li], nw_c, lam_init, ctx_out)
        cw = jnp.pad(conv_w[li], ((0, 8 - CONV_K), (0, 0)))
        pad12 = lambda a: jnp.pad(a.reshape(1, 2 * N_HS), ((0, 0), (0, LANES - 2 * N_HS)))
        yb = _ssd(li, xbc, z, dt, cw, conv_b[li][None, :], pad12(dt_bias[li]), pad12(a_log[li]),
                  jnp.repeat(d_skip[li], HD_S)[None, :], ssm_norm_w[li][None, :], ef, eb)

        wo = w_out[li]
        wo_a = wo[:384].reshape(N_KVA, 3, HD_A, D_MODEL).transpose(1, 0, 2, 3).reshape(384, D_MODEL)
        wo_re = jnp.concatenate([wo_a, wo[384:]], axis=0).astype(BF16)
        moe = li % 2 == 1
        fj = li // 2
        if moe:
            wr = jnp.pad(router_w[fj], ((0, 0), (0, LANES - N_EXP)))
            br = jnp.pad(router_b[fj], (0, LANES - N_EXP))[None, :]
        else:
            wr = br = None
        res = _outproj(li, ya_lat, ya_ctx, yb, yd_lat, yd_ctx, x_all, mod5, ln1_g[li][None, :],
                       ln1_b[li][None, :], wo_re, moe, wr, br, j0)
        lg2, lb2 = ln2_g[li][None, :], ln2_b[li][None, :]
        if moe:
            x1, h2, route, counts = res
            wgu, wd = _chunk_gu(exp_w_gu[fj]), _chunk_down(exp_w_down[fj])
            x_all = _ffn_routed(li, x1, h2, route, counts, mod5, lg2, lb2, wgu, wd, j0)
        else:
            x1, h2 = res
            wgu, wd = _chunk_gu(ffn_w_gu[fj]), _chunk_down(ffn_w_down[fj])
            x_all = _ffn_dense(li, x1, h2, mod5, lg2, lb2, wgu, wd)
    return x_all
```

```python
import functools
import math

import jax
import jax.numpy as jnp
from jax import lax
from jax.experimental import pallas as pl
from jax.experimental.pallas import tpu as pltpu

F32 = jnp.float32
BF16 = jnp.bfloat16
HIGHEST = lax.Precision.HIGHEST

D_MODEL = 1024
BATCH = 8
SEQ = 2048
DEPTH = 4
GRID_W = 64
CTX_LEN = 256
BLK = 128
WINDOW = 128
ROPE_BASE = 10000.0
EPS = 1e-6
HD_A = 64
N_HA = 6
N_KVA = 2
D_INNER = 384
HD_S = 64
N_HS = 6
D_STATE = 128
CONV_K = 5
CONV_DIM = 896
CHUNK = 128
N_HC = 4
HD_C = 32
D_C = 256
D_FF = 2816
N_EXP = 8
ALPHA = (2 * DEPTH) ** 0.25

LANES = 128
TILE = 256
ROWS = CTX_LEN + SEQ
N_TILES = ROWS // TILE
N_CHUNKS = ROWS // CHUNK
N_CTX_CHUNKS = CTX_LEN // CHUNK
FF_CHUNK = 256
N_FF_CHUNKS = D_FF // FF_CHUNK
D_IN_PAD = 2816
MOD_ROWS = 16
CTX_MOD_ROW = BATCH
VMEM_LIMIT = 56 * 1024 * 1024


def _cparams(sem, vmem=None):
    return pltpu.CompilerParams(dimension_semantics=sem, vmem_limit_bytes=vmem)


def _sigmoid(x):
    return 1.0 / (1.0 + jnp.exp(-x))


def _silu(x):
    return x * _sigmoid(x)


def _ln(x):
    mu = jnp.mean(x, axis=-1, keepdims=True)
    xc = x - mu
    var = jnp.mean(xc * xc, axis=-1, keepdims=True)
    return xc * lax.rsqrt(var + EPS)


def _dot(a, b, precision=None):
    return jnp.dot(a, b, preferred_element_type=F32, precision=precision)


def _split_bf16(x, parts):
    out = []
    for _ in range(parts - 1):
        hi = x.astype(BF16)
        out.append(hi)
        x = x - hi.astype(F32)
    out.append(x.astype(BF16))
    return out


def _dot_nt(a, b):
    return lax.dot_general(a, b, (((1,), (1,)), ((), ())), preferred_element_type=F32)


def _ada_kernel(c_ref, w_ref, b_ref, o_ref):
    act = _silu(c_ref[...])
    o_ref[...] = _dot(act.astype(BF16), w_ref[...].astype(BF16)) + b_ref[...]


def _ada_all(c_all, w_ada, b_ada):
    tn = 1536
    return pl.pallas_call(
        _ada_kernel,
        grid=(DEPTH, 6 * D_MODEL // tn),
        in_specs=[
            pl.BlockSpec((MOD_ROWS, D_MODEL), lambda l, j: (0, 0)),
            pl.BlockSpec((None, D_MODEL, tn), lambda l, j: (l, 0, j)),
            pl.BlockSpec((None, 1, tn), lambda l, j: (l, 0, j)),
        ],
        out_specs=pl.BlockSpec((None, MOD_ROWS, tn), lambda l, j: (l, 0, j)),
        out_shape=jax.ShapeDtypeStruct((DEPTH, MOD_ROWS, 6 * D_MODEL), F32),
        compiler_params=_cparams(("parallel", "parallel")),
        name="ada_ln",
    )(c_all, w_ada, b_ada.reshape(DEPTH, 1, 6 * D_MODEL))


def _mod_spec(li, piece, j0):
    return pl.BlockSpec(
        (None, None, None, 1, D_MODEL),
        lambda b, j: (li, jnp.where(j + j0 == 0, CTX_MOD_ROW, b), piece, 0, 0))


def _tok_spec(width, j0=0):
    return pl.BlockSpec((None, TILE, width), lambda b, j: (b, j + j0, 0))


def _full_spec(shape):
    n = len(shape)
    return pl.BlockSpec(shape, lambda *_: (0,) * n)


def _rope(v, cos, sin_signed, half):
    lane = lax.broadcasted_iota(jnp.int32, (v.shape[0], LANES), 1)
    first = (lane % (2 * half)) < half
    outs = []
    for s in range(v.shape[1] // LANES):
        blk = v[:, s * LANES:(s + 1) * LANES]
        swapped = jnp.where(first, pltpu.roll(blk, LANES - half, 1), pltpu.roll(blk, half, 1))
        outs.append(blk * cos + swapped * sin_signed)
    return jnp.concatenate(outs, axis=1)


def _inproj_kernel(x_ref, sh_ref, sc_ref, w_ref, cosa_ref, sina_ref, cosc_ref, sinc_ref,
                   qa_ref, ka_ref, va_ref, z_ref, xbc_ref, dt_ref, qc_ref, kc_ref, vc_ref):
    h = _ln(x_ref[...]) * (1.0 + sc_ref[...]) + sh_ref[...]
    hb = h.astype(BF16)

    def proj(lo, hi):
        return _dot(hb, w_ref[:, lo:hi])

    cosa, sina = cosa_ref[...], sina_ref[...]
    cosc, sinc = cosc_ref[...], sinc_ref[...]
    qa_ref[...] = (_rope(proj(0, 384), cosa, sina, HD_A // 2) * (HD_A ** -0.5)).astype(BF16)
    ka_ref[...] = _rope(proj(384, 512), cosa, sina, HD_A // 2).astype(BF16)
    va_ref[...] = proj(512, 640).astype(BF16)
    z_ref[...] = proj(640, 1024)
    xbc_ref[...] = proj(1024, 1920)
    qc_ref[...] = (_rope(proj(1920, 2176), cosc, sinc, HD_C // 2) * (HD_C ** -0.5)).astype(BF16)
    kc_ref[...] = _rope(proj(2176, 2432), cosc, sinc, HD_C // 2).astype(BF16)
    vc_ref[...] = proj(2432, 2688).astype(BF16)
    dt_ref[...] = proj(2688, 2816)


def _inproj(li, x_all, mod5, w_re, ropes):
    widths = (384, 128, 128, 384, CONV_DIM, LANES, D_C, D_C, D_C)
    dtypes = (BF16, BF16, BF16, F32, F32, F32, BF16, BF16, BF16)
    rope_spec = pl.BlockSpec((TILE, LANES), lambda b, j: (j, 0))
    return pl.pallas_call(
        _inproj_kernel,
        grid=(BATCH, N_TILES),
        in_specs=[_tok_spec(D_MODEL), _mod_spec(li, 0, 0), _mod_spec(li, 1, 0),
                  _full_spec((D_MODEL, D_IN_PAD)), rope_spec, rope_spec, rope_spec, rope_spec],
        out_specs=[_tok_spec(w) for w in widths],
        out_shape=[jax.ShapeDtypeStruct((BATCH, ROWS, w), d) for w, d in zip(widths, dtypes)],
        compiler_params=_cparams(("parallel", "parallel"), VMEM_LIMIT),
        name=f"inproj_l{li}",
    )(x_all, mod5, mod5, w_re, *ropes)


def _attn_a_kernel(*refs, local):
    if local:
        (sink_ref, q_ref, kp_ref, kc_ref, kn_ref, kx_ref,
         vp_ref, vc_ref, vn_ref, vx_ref, o_ref) = refs
    else:
        sink_ref, q_ref, kx_ref, vx_ref, o_ref = refs
    tq = q_ref.shape[0]
    q = q_ref[...].astype(F32)
    lane_lo = lax.broadcasted_iota(jnp.int32, (tq, LANES), 1) < HD_A
    parts = []
    for s in range(3):
        blk = q[:, s * LANES:(s + 1) * LANES]
        parts.append(jnp.where(lane_lo, blk, 0.0))
        parts.append(jnp.where(lane_lo, 0.0, blk))
    qst = jnp.concatenate(parts, axis=0).astype(BF16)
    if local:
        k = jnp.concatenate([kp_ref[...], kc_ref[...], kn_ref[...], kx_ref[...]], axis=0)
        v = jnp.concatenate([vp_ref[...], vc_ref[...], vn_ref[...], vx_ref[...]], axis=0)
        n = pl.program_id(1)
        r = lax.broadcasted_iota(jnp.int32, (tq, 3 * BLK), 0)
        c = lax.broadcasted_iota(jnp.int32, (tq, 3 * BLK), 1)
        prev_v = jnp.where(n > 0, c - r, -1)
        next_v = jnp.where(n < SEQ // BLK - 1, r - (c - 2 * BLK), -1)
        valid = jnp.where(c < BLK, prev_v, jnp.where(c < 2 * BLK, 0, next_v)) >= 0
    else:
        k, v = kx_ref[...], vx_ref[...]
    s_all = _dot_nt(qst, k)
    probs = []
    for rb in range(6):
        sink = sink_ref[rb]
        sh = s_all[rb * tq:(rb + 1) * tq]
        if local:
            s_loc = jnp.where(valid, sh[:, :3 * BLK], -jnp.inf)
            s_ctx = sh[:, 3 * BLK:]
            m = jnp.maximum(jnp.max(s_loc, axis=-1, keepdims=True),
                            jnp.max(s_ctx, axis=-1, keepdims=True))
            m = jnp.maximum(m, sink)
            p_loc = jnp.exp(s_loc - m)
            p_ctx = jnp.exp(s_ctx - m)
            den = (jnp.sum(p_loc, axis=-1, keepdims=True) + jnp.sum(p_ctx, axis=-1, keepdims=True)
                   + jnp.exp(sink - m))
            p = jnp.concatenate([p_loc, p_ctx], axis=1) / den
        else:
            m = jnp.maximum(jnp.max(sh, axis=-1, keepdims=True), sink)
            p_ctx = jnp.exp(sh - m)
            den = jnp.sum(p_ctx, axis=-1, keepdims=True) + jnp.exp(sink - m)
            p = p_ctx / den
        probs.append(p.astype(BF16))
    o_all = _dot(jnp.concatenate(probs, axis=0), v)
    outs = []
    for s in range(3):
        o0 = o_all[(2 * s) * tq:(2 * s + 1) * tq]
        o1 = o_all[(2 * s + 1) * tq:(2 * s + 2) * tq]
        outs.append(jnp.where(lane_lo, o0, o1))
    o_ref[...] = jnp.concatenate(outs, axis=1).astype(BF16)


def _attn_a(li, qa, ka, va, sink_p, with_ctx):
    nb = SEQ // BLK
    off = CTX_LEN // BLK
    smem = pl.BlockSpec(memory_space=pltpu.SMEM)

    def kv(shift):
        return pl.BlockSpec(
            (None, BLK, LANES), lambda b, n: (b, off + jnp.clip(n + shift, 0, nb - 1), 0))

    ctx_kv = pl.BlockSpec((None, CTX_LEN, LANES), lambda b, n: (b, 0, 0))
    q_spec = pl.BlockSpec((None, BLK, 384), lambda b, n: (b, off + n, 0))
    y_lat = pl.pallas_call(
        functools.partial(_attn_a_kernel, local=True),
        grid=(BATCH, nb),
        in_specs=[smem, q_spec, kv(-1), kv(0), kv(1), ctx_kv, kv(-1), kv(0), kv(1), ctx_kv],
        out_specs=pl.BlockSpec((None, BLK, 384), lambda b, n: (b, n, 0)),
        out_shape=jax.ShapeDtypeStruct((BATCH, SEQ, 384), BF16),
        compiler_params=_cparams(("parallel", "parallel")),
        name=f"attn_a_l{li}",
    )(sink_p, qa, ka, ka, ka, ka, va, va, va, va)
    if not with_ctx:
        return y_lat, None
    ctx_spec384 = pl.BlockSpec((None, CTX_LEN, 384), lambda b: (b, 0, 0))
    ctx_spec128 = pl.BlockSpec((None, CTX_LEN, LANES), lambda b: (b, 0, 0))
    y_ctx = pl.pallas_call(
        functools.partial(_attn_a_kernel, local=False),
        grid=(BATCH,),
        in_specs=[smem, ctx_spec384, ctx_spec128, ctx_spec128],
        out_specs=ctx_spec384,
        out_shape=jax.ShapeDtypeStruct((BATCH, CTX_LEN, 384), BF16),
        compiler_params=_cparams(("parallel",)),
        name=f"attn_a_ctx_l{li}",
    )(sink_p, qa, ka, va)
    return y_lat, y_ctx


def _attn_c_kernel(lq_ref, lk_ref, nw_ref, q_ref, k_ref, v_ref, o_ref, *, lam_init):
    tq = q_ref.shape[0]
    e = jnp.exp(jnp.sum(lq_ref[...] * lk_ref[...], axis=-1, keepdims=True))
    lam = e[0:1] - e[1:2] + lam_init
    q = q_ref[...].astype(F32)
    k = k_ref[...]
    v = v_ref[...]
    lane = lax.broadcasted_iota(jnp.int32, (tq, D_C), 1)
    acc = jnp.zeros((tq, D_C), F32)
    for h in range(N_HC):
        base = h * 2 * HD_C
        in0 = jnp.where(lane >= base, lane - base, 2 * HD_C) < HD_C
        in1 = jnp.where(lane >= base + HD_C, lane - base - HD_C, HD_C) < HD_C
        qst = jnp.concatenate([jnp.where(in0, q, 0.0), jnp.where(in1, q, 0.0)], axis=0).astype(BF16)
        s = _dot_nt(qst, k)
        m = jnp.max(s, axis=-1, keepdims=True)
        p = jnp.exp(s - m)
        r = 1.0 / jnp.sum(p, axis=-1, keepdims=True)
        a = (p[:tq] * r[:tq] - p[tq:] * (r[tq:] * lam)).astype(BF16)
        o = _dot(a, v)
        inh = jnp.where(lane >= base, lane - base, 2 * HD_C) < 2 * HD_C
        oh = jnp.where(inh, o, 0.0)
        ms = jnp.sum(oh * oh, axis=-1, keepdims=True) * (1.0 / (2 * HD_C))
        acc = acc + oh * lax.rsqrt(ms + EPS)
    o_ref[...] = (acc * nw_ref[...] * (1.0 - lam_init)).astype(BF16)


def _attn_c(li, qc, kc, vc, lam_q, lam_k, nw_row, lam_init, with_ctx):
    tq = TILE
    off = CTX_LEN // tq
    small = [_full_spec((2, HD_C)), _full_spec((2, HD_C)), _full_spec((1, D_C))]
    kern = functools.partial(_attn_c_kernel, lam_init=lam_init)
    all_kv = pl.BlockSpec((None, ROWS, D_C), lambda b, n: (b, 0, 0))
    y_lat = pl.pallas_call(
        kern,
        grid=(BATCH, SEQ // tq),
        in_specs=small + [pl.BlockSpec((None, tq, D_C), lambda b, n: (b, off + n, 0)), all_kv, all_kv],
        out_specs=pl.BlockSpec((None, tq, D_C), lambda b, n: (b, n, 0)),
        out_shape=jax.ShapeDtypeStruct((BATCH, SEQ, D_C), BF16),
        compiler_params=_cparams(("parallel", "parallel"), VMEM_LIMIT),
        name=f"attn_c_l{li}",
    )(lam_q, lam_k, nw_row, qc, kc, vc)
    if not with_ctx:
        return y_lat, None
    ctx_spec = pl.BlockSpec((None, CTX_LEN, D_C), lambda b: (b, 0, 0))
    small1 = [_full_spec((2, HD_C)), _full_spec((2, HD_C)), _full_spec((1, D_C))]
    y_ctx = pl.pallas_call(
        kern,
        grid=(BATCH,),
        in_specs=small1 + [ctx_spec, ctx_spec, ctx_spec],
        out_specs=ctx_spec,
        out_shape=jax.ShapeDtypeStruct((BATCH, CTX_LEN, D_C), BF16),
        compiler_params=_cparams(("parallel",)),
        name=f"attn_c_ctx_l{li}",
    )(lam_q, lam_k, nw_row, qc, kc, vc)
    return y_lat, y_ctx


def _ssd_kernel(xbc_ref, z_ref, dt_ref, cw_ref, cb_ref, dtb_ref, alog_ref, dsk_ref, nw_ref,
                ef_ref, eb_ref, y_ref, u_scr, yacc_scr, st_scr):
    cw = cw_ref[...]
    cb = cb_ref[...]
    dsk = dsk_ref[...]
    pad = 8
    for c in range(N_CHUNKS):
        r0 = CHUNK * c
        zeros = jnp.zeros((pad, CONV_DIM), F32)
        if c in (0, N_CTX_CHUNKS):
            blk = jnp.concatenate([zeros, xbc_ref[r0:r0 + CHUNK + pad, :]], axis=0)
        elif c in (N_CTX_CHUNKS - 1, N_CHUNKS - 1):
            blk = jnp.concatenate([xbc_ref[r0 - pad:r0 + CHUNK, :], zeros], axis=0)
        else:
            blk = xbc_ref[r0 - pad:r0 + CHUNK + pad, :]
        acc = None
        for j in range(CONV_K):
            sh = (CONV_K // 2 - j) % (CHUNK + 2 * pad)
            rolled = blk if sh == 0 else pltpu.roll(blk, sh, 0)
            term = rolled[pad:pad + CHUNK, :] * cw[j:j + 1, :]
            acc = term if acc is None else acc + term
        u = _silu(acc + cb)
        u_scr[r0:r0 + CHUNK, :] = u
        yacc_scr[r0:r0 + CHUNK, :] = u[:, :D_INNER] * dsk

    st_scr[...] = jnp.zeros(st_scr.shape, F32)
    ri = lax.broadcasted_iota(jnp.int32, (CHUNK, CHUNK), 0)
    ci = lax.broadcasted_iota(jnp.int32, (CHUNK, CHUNK), 1)
    lower = ri >= ci
    upper = ri <= ci
    tri = (jnp.where(lower, 1.0, 0.0).astype(BF16), jnp.where(upper, 1.0, 0.0).astype(BF16))
    a_row = -jnp.exp(alog_ref[...])
    dtb = dtb_ref[...]
    lane_lo = ci < HD_S

    def chunk_dir(c, d):
        r0 = pl.multiple_of(c * CHUNK, CHUNK)
        u = u_scr[pl.ds(r0, CHUNK), :]
        xs = u[:, :D_INNER]
        bm = u[:, D_INNER:D_INNER + 2 * D_STATE]
        cm = u[:, D_INNER + 2 * D_STATE:]
        vv = dt_ref[pl.ds(r0, CHUNK), :] + dtb
        dt = jnp.maximum(vv, 0.0) + jnp.log(1.0 + jnp.exp(-jnp.abs(vv)))
        a = dt * a_row
        cum = sum(_dot(tri[d], piece) for piece in _split_bf16(a, 3))
        cum_t = cum.T
        tot = cum[CHUNK - 1:CHUNK, :] if d == 0 else cum[0:1, :]
        expand = ef_ref[...] if d == 0 else eb_ref[...]
        per_head = jnp.concatenate([dt, jnp.exp(cum), jnp.exp(tot - cum)], axis=0)
        per_lane = sum(_dot(piece, expand) for piece in _split_bf16(per_head, 2))
        dt_x = per_lane[:CHUNK]
        dec_out_x = per_lane[CHUNK:2 * CHUNK]
        dec_st_x = per_lane[2 * CHUNK:]
        etot_x = dec_out_x[CHUNK - 1:CHUNK, :] if d == 0 else dec_out_x[0:1, :]
        xdt = xs * dt_x
        xdt_b = xdt.astype(BF16)
        xst_b = (xdt * dec_st_x).astype(BF16)
        cb_ = [cm[:, g * D_STATE:(g + 1) * D_STATE].astype(BF16) for g in range(2)]
        bb_ = [bm[:, g * D_STATE:(g + 1) * D_STATE].astype(BF16) for g in range(2)]
        bt_ = [bm[:, g * D_STATE:(g + 1) * D_STATE].T.astype(BF16) for g in range(2)]
        cbm = [_dot_nt(cb_[g], bb_[g]) for g in range(2)]
        mask = lower if d == 0 else upper
        for j in range(3):
            sl = slice(j * LANES, (j + 1) * LANES)
            parts = []
            for h in (2 * j, 2 * j + 1):
                hl = h + N_HS * d
                seg = jnp.where(mask, cum[:, hl:hl + 1] - cum_t[hl:hl + 1, :], -jnp.inf)
                sc = (cbm[h // 3] * jnp.exp(seg)).astype(BF16)
                parts.append(_dot(sc, xdt_b[:, sl]))
            y_diag = jnp.where(lane_lo, parts[0], parts[1])
            st = st_scr[d, j]
            st_b = st.astype(BF16)
            g0, g1 = (2 * j) // 3, (2 * j + 1) // 3
            if g0 == g1:
                y_off = _dot(cb_[g0], st_b)
                upd = _dot(bt_[g0], xst_b[:, sl])
            else:
                y_off = jnp.where(lane_lo, _dot(cb_[g0], st_b), _dot(cb_[g1], st_b))
                upd = jnp.where(lane_lo, _dot(bt_[g0], xst_b[:, sl]), _dot(bt_[g1], xst_b[:, sl]))
            yacc_scr[pl.ds(r0, CHUNK), sl] += y_diag + y_off * dec_out_x[:, sl]
            st_scr[d, j] = etot_x[:, sl] * st + upd

    def body(i, carry):
        chunk_dir(i, 0)
        chunk_dir(jnp.where(i < N_CTX_CHUNKS, N_CTX_CHUNKS - 1 - i, N_CHUNKS + N_CTX_CHUNKS - 1 - i), 1)
        return carry

    lax.fori_loop(0, N_CHUNKS, body, 0)

    nw = nw_ref[...]
    for t in range(N_TILES):
        rows = slice(t * TILE, (t + 1) * TILE)
        y = yacc_scr[rows, :] * _silu(z_ref[rows, :])
        ms = jnp.mean(y * y, axis=-1, keepdims=True)
        y_ref[rows, :] = (y * lax.rsqrt(ms + EPS) * nw).astype(BF16)


def _ssd(li, xbc, z, dt, cw, cb, dtb, alog, dsk, nw, ef, eb):
    def seq(width):
        return pl.BlockSpec((None, ROWS, width), lambda b: (b, 0, 0))

    return pl.pallas_call(
        _ssd_kernel,
        grid=(BATCH,),
        in_specs=[seq(CONV_DIM), seq(D_INNER), seq(LANES),
                  _full_spec((8, CONV_DIM)), _full_spec((1, CONV_DIM)),
                  _full_spec((1, LANES)), _full_spec((1, LANES)),
                  _full_spec((1, D_INNER)), _full_spec((1, D_INNER)),
                  _full_spec((LANES, D_INNER)), _full_spec((LANES, D_INNER))],
        out_specs=seq(D_INNER),
        out_shape=jax.ShapeDtypeStruct((BATCH, ROWS, D_INNER), BF16),
        scratch_shapes=[pltpu.VMEM((ROWS, CONV_DIM), F32), pltpu.VMEM((ROWS, D_INNER), F32),
                        pltpu.VMEM((2, 3, D_STATE, LANES), F32)],
        compiler_params=_cparams(("parallel",), VMEM_LIMIT),
        name=f"ssd_l{li}",
    )(xbc, z, dt, cw, cb, dtb, alog, dsk, nw, ef, eb)


def _top2_gates(logits):
    lane = lax.broadcasted_iota(jnp.int32, logits.shape, 1).astype(F32)
    neg = -jnp.inf
    lg = jnp.where(lane < N_EXP, logits, neg)
    m1 = jnp.max(lg, axis=-1, keepdims=True)
    i1 = jnp.min(jnp.where(lg == m1, lane, float(LANES)), axis=-1, keepdims=True)
    lg2 = jnp.where(lane == i1, neg, lg)
    m2 = jnp.max(lg2, axis=-1, keepdims=True)
    i2 = jnp.min(jnp.where(lg2 == m2, lane, float(LANES)), axis=-1, keepdims=True)
    e2 = jnp.exp(m2 - m1)
    return lane, i1, i2, 1.0 / (1.0 + e2), e2 / (1.0 + e2)


R_E1, R_E2, R_W1, R_W2, R_RANK1, R_RANK2 = range(6)


def _route_record(logits, cnt_scr):
    lane, i1, i2, w1, w2 = _top2_gates(logits)
    rows = logits.shape[0]
    hit1, hit2 = lane == i1, lane == i2
    onehot = jnp.where(hit1, 1.0, 0.0) + jnp.where(hit2, 1.0, 0.0)
    ri = lax.broadcasted_iota(jnp.int32, (rows, rows), 0)
    ci = lax.broadcasted_iota(jnp.int32, (rows, rows), 1)
    earlier = jnp.where(ri > ci, 1.0, 0.0).astype(BF16)
    before = _dot(earlier, onehot.astype(BF16)) + cnt_scr[...]
    rank1 = jnp.sum(jnp.where(hit1, before, 0.0), axis=-1, keepdims=True)
    rank2 = jnp.sum(jnp.where(hit2, before, 0.0), axis=-1, keepdims=True)
    cnt_scr[...] += jnp.sum(onehot, axis=0, keepdims=True)
    rec = jnp.zeros(logits.shape, F32)
    for idx, val in ((R_E1, i1), (R_E2, i2), (R_W1, w1), (R_W2, w2), (R_RANK1, rank1), (R_RANK2, rank2)):
        rec = jnp.where(lane == float(idx), val, rec)
    return rec


def _outproj_kernel(*refs, moe, ctx_split):
    if ctx_split:
        ya_l, ya_c, yb_ref, yd_l, yd_c = refs[:5]
        refs = refs[5:]
    else:
        ya_l, yb_ref, yd_l = refs[:3]
        refs = refs[3:]
    x_ref, g1_ref, lg_ref, lb_ref, sh_ref, sc_ref, wo_ref = refs[:7]
    refs = refs[7:]
    if moe:
        wr_ref, br_ref, x1_ref, h2_ref, route_ref, counts_ref, cnt_scr = refs

        @pl.when((pl.program_id(0) == 0) & (pl.program_id(1) == 0))
        def _():
            cnt_scr[...] = jnp.zeros(cnt_scr.shape, F32)
    else:
        x1_ref, h2_ref = refs
    if ctx_split:
        is_ctx = pl.program_id(1) == 0
        ya = jnp.where(is_ctx, ya_c[...], ya_l[...])
        yd = jnp.where(is_ctx, yd_c[...], yd_l[...])
    else:
        ya, yd = ya_l[...], yd_l[...]
    mix = (_dot(ya, wo_ref[0:384, :]) + _dot(yb_ref[...], wo_ref[384:768, :])
           + _dot(yd, wo_ref[768:1024, :]))
    x1 = _ln(ALPHA * x_ref[...] + g1_ref[...] * mix) * lg_ref[...] + lb_ref[...]
    x1_ref[...] = x1
    h2 = _ln(x1) * (1.0 + sc_ref[...]) + sh_ref[...]
    h2_ref[...] = h2.astype(h2_ref.dtype)
    if moe:
        h_hi, h_mid = _split_bf16(h2, 2)
        both = _dot(h_hi, wr_ref[...])
        logits = (both[:, :LANES] + both[:, LANES:] + _dot(h_mid, wr_ref[:, :LANES])) + br_ref[...]
        route_ref[...] = _route_record(logits, cnt_scr)
        counts_ref[...] = cnt_scr[...]


def _outproj(li, ya_lat, ya_ctx, yb, yd_lat, yd_ctx, x_all, mod5, lg, lb, wo, moe, wr, br, j0):
    ctx_split = j0 == 0
    nt = N_TILES - j0
    off = CTX_LEN // TILE
    lat = lambda w: pl.BlockSpec((None, TILE, w), lambda b, j: (b, jnp.maximum(j + j0 - off, 0), 0))
    ctx = lambda w: pl.BlockSpec((None, TILE, w), lambda b, j: (b, 0, 0))
    row = _full_spec((1, D_MODEL))
    if ctx_split:
        ins = [ya_lat, ya_ctx, yb, yd_lat, yd_ctx]
        specs = [lat(384), ctx(384), _tok_spec(D_INNER, j0), lat(D_C), ctx(D_C)]
    else:
        ins = [ya_lat, yb, yd_lat]
        specs = [lat(384), _tok_spec(D_INNER, j0), lat(D_C)]
    ins += [x_all, mod5, lg, lb, mod5, mod5, wo]
    specs += [_tok_spec(D_MODEL, j0), _mod_spec(li, 2, j0), row, row,
              _mod_spec(li, 3, j0), _mod_spec(li, 4, j0), _full_spec((D_MODEL, D_MODEL))]
    outs = [jax.ShapeDtypeStruct((BATCH, nt * TILE, D_MODEL), F32),
            jax.ShapeDtypeStruct((BATCH, nt * TILE, D_MODEL), F32 if moe else BF16)]
    out_specs = [_tok_spec(D_MODEL), _tok_spec(D_MODEL)]
    scratch = []
    if moe:
        ins += [wr, br]
        specs += [_full_spec((D_MODEL, 2 * LANES)), _full_spec((1, LANES))]
        outs += [jax.ShapeDtypeStruct((BATCH, nt * TILE, LANES), F32),
                 jax.ShapeDtypeStruct((1, LANES), F32)]
        out_specs += [_tok_spec(LANES), _full_spec((1, LANES))]
        scratch = [pltpu.VMEM((1, LANES), F32)]
    sem = ("arbitrary", "arbitrary") if moe else ("parallel", "parallel")
    return pl.pallas_call(
        functools.partial(_outproj_kernel, moe=moe, ctx_split=ctx_split),
        grid=(BATCH, nt),
        in_specs=specs,
        out_specs=out_specs,
        out_shape=outs,
        scratch_shapes=scratch,
        compiler_params=_cparams(sem, VMEM_LIMIT),
        name=f"outproj_l{li}",
    )(*ins)


def _swiglu_acc(h2, wgu_ref, wd_ref):
    acc = jnp.zeros((h2.shape[0], D_MODEL), F32)
    for c in range(N_FF_CHUNKS):
        cols = slice(c * FF_CHUNK, (c + 1) * FF_CHUNK)
        up_cols = slice(D_FF + c * FF_CHUNK, D_FF + (c + 1) * FF_CHUNK)
        act = (_silu(_dot(h2, wgu_ref[:, cols])) * _dot(h2, wgu_ref[:, up_cols])).astype(BF16)
        acc = acc + _dot(act, wd_ref[cols, :])
    return acc


def _ffn_dense_kernel(x1_ref, h2_ref, g2_ref, lg_ref, lb_ref, wgu_ref, wd_ref, o_ref):
    y = _swiglu_acc(h2_ref[...], wgu_ref, wd_ref)
    o_ref[...] = _ln(ALPHA * x1_ref[...] + g2_ref[...] * y) * lg_ref[...] + lb_ref[...]


def _ffn_dense(li, x1, h2, mod5, lg, lb, wgu, wd):
    row = _full_spec((1, D_MODEL))
    return pl.pallas_call(
        _ffn_dense_kernel,
        grid=(BATCH, N_TILES),
        in_specs=[_tok_spec(D_MODEL), _tok_spec(D_MODEL), _mod_spec(li, 5, 0), row, row,
                  _full_spec((D_MODEL, 2 * D_FF)), _full_spec((D_FF, D_MODEL))],
        out_specs=_tok_spec(D_MODEL),
        out_shape=jax.ShapeDtypeStruct((BATCH, ROWS, D_MODEL), F32),
        compiler_params=_cparams(("parallel", "parallel"), VMEM_LIMIT),
        name=f"ffn_dense_l{li}",
    )(x1, h2, mod5, lg, lb, wgu, wd)


EXP_TILE = 256


def _slot_row_copy(src_ref, src_row, dst_ref, dst_row, sem):
    return pltpu.make_async_copy(src_ref.at[pl.ds(src_row, 1), :], dst_ref.at[pl.ds(dst_row, 1), :], sem)


def _wait_rows(src_ref, dst_ref, sem):
    pltpu.make_async_copy(src_ref, dst_ref, sem).wait()


def _dispatch_kernel(pos_ref, fill_start_ref, fill_rows_ref, h2_ref, xs_ref, zero_scr, sem, zsem):
    first = (pl.program_id(0) == 0) & (pl.program_id(1) == 0)
    sub = 8
    pad_chunks = [1 << s for s in reversed(range(sub.bit_length() - 1, EXP_TILE.bit_length() - 1))]

    def pad_copies(e):
        start, n = fill_start_ref[e], fill_rows_ref[e]
        head = (-start) & (sub - 1)
        for i in range(sub - 1):
            yield i < jnp.minimum(head, n), _slot_row_copy(zero_scr, i, xs_ref, start + i, zsem)
        body = jnp.maximum(n - head, 0)
        for rows in pad_chunks:
            off = pl.multiple_of(start + head + (body & ~(2 * rows - 1)), sub)
            dst = xs_ref.at[pl.ds(off, rows), :]
            yield (body & rows) != 0, pltpu.make_async_copy(zero_scr.at[pl.ds(0, rows), :], dst, zsem)

    @pl.when(first)
    def _():
        zero_scr[...] = jnp.zeros(zero_scr.shape, F32)
        for e in range(fill_rows_ref.shape[0]):
            for cond, copy in pad_copies(e):
                pl.when(cond)(copy.start)

    base = (pl.program_id(0) * pl.num_programs(1) + pl.program_id(1)) * (2 * TILE)

    def issue(r, carry):
        for k in range(2):
            _slot_row_copy(h2_ref, r, xs_ref, pos_ref[base + 2 * r + k], sem).start()
        return carry

    lax.fori_loop(0, TILE, issue, 0, unroll=8)
    for k in range(2):
        _wait_rows(h2_ref, xs_ref.at[pl.ds(0, TILE), :], sem)

    @pl.when(first)
    def _():
        for e in range(fill_rows_ref.shape[0]):
            for cond, copy in pad_copies(e):
                pl.when(cond)(copy.wait)


def _grouped_kernel(tile_e_ref, n_used_ref, xs_ref, wgu_ref, wd_ref, ys_ref):
    del tile_e_ref
    live = pl.program_id(0) < n_used_ref[0]

    @pl.when(live)
    def _():
        ys_ref[...] = _swiglu_acc(xs_ref[...].astype(BF16), wgu_ref, wd_ref)

    @pl.when(jnp.logical_not(live))
    def _():
        ys_ref[...] = jnp.zeros(ys_ref.shape, F32)


def _combine_kernel(pos_ref, ys_ref, route_ref, x1_ref, g2_ref, lg_ref, lb_ref, o_ref, ybuf, sem):
    base = (pl.program_id(0) * pl.num_programs(1) + pl.program_id(1)) * (2 * TILE)

    def issue(r, carry):
        for k in range(2):
            _slot_row_copy(ys_ref, pos_ref[base + 2 * r + k], ybuf.at[k], r, sem).start()
        return carry

    lax.fori_loop(0, TILE, issue, 0, unroll=8)
    for k in range(2):
        _wait_rows(ys_ref.at[pl.ds(0, TILE), :], ybuf.at[k], sem)
    route = route_ref[...]
    lane = lax.broadcasted_iota(jnp.int32, route.shape, 1)
    w1 = jnp.sum(jnp.where(lane == R_W1, route, 0.0), axis=-1, keepdims=True)
    w2 = jnp.sum(jnp.where(lane == R_W2, route, 0.0), axis=-1, keepdims=True)
    y = w1 * ybuf[0] + w2 * ybuf[1]
    o_ref[...] = _ln(ALPHA * x1_ref[...] + g2_ref[...] * y) * lg_ref[...] + lb_ref[...]


def _ffn_routed(li, x1, h2, route, counts, mod5, lg, lb, wgu, wd, j0):
    nt = N_TILES - j0
    n_tok = BATCH * nt * TILE
    n_tiles = 2 * n_tok // EXP_TILE + N_EXP
    n_slots = n_tiles * EXP_TILE
    cnt = counts[0, :N_EXP].astype(jnp.int32)
    padded = (cnt + EXP_TILE - 1) // EXP_TILE * EXP_TILE
    ends = jnp.cumsum(padded)
    offs = ends - padded
    n_used = ends[-1:] // EXP_TILE
    tile_start = jnp.minimum(jnp.arange(n_tiles), n_used - 1) * EXP_TILE
    tile_e = jnp.sum((tile_start[:, None] >= ends[None, :]).astype(jnp.int32), axis=1)
    rec = route.reshape(n_tok, LANES)
    experts = rec[:, R_E1:R_E2 + 1].astype(jnp.int32)
    ranks = rec[:, R_RANK1:R_RANK2 + 1].astype(jnp.int32)
    off_of = jnp.sum(jnp.where(experts[..., None] == jnp.arange(N_EXP), offs, 0), axis=-1)
    pos = (off_of + ranks).reshape(2 * n_tok)

    h2_flat = h2.reshape(n_tok, D_MODEL)
    half = EXP_TILE // 2
    tail_start = n_used * EXP_TILE + jnp.arange(2 * N_EXP) * half
    tail_rows = jnp.where(tail_start < n_slots, half, 0)
    tail_start = jnp.minimum(tail_start, n_slots - half)
    any_spec = pl.BlockSpec(memory_space=pl.ANY)
    xs = pl.pallas_call(
        _dispatch_kernel,
        grid_spec=pltpu.PrefetchScalarGridSpec(
            num_scalar_prefetch=3,
            grid=(BATCH, nt),
            in_specs=[pl.BlockSpec((TILE, D_MODEL), lambda b, j, *_: (b * nt + j, 0))],
            out_specs=any_spec,
            scratch_shapes=[pltpu.VMEM((EXP_TILE // 2, D_MODEL), F32),
                            pltpu.SemaphoreType.DMA(()), pltpu.SemaphoreType.DMA(())]),
        out_shape=jax.ShapeDtypeStruct((n_slots, D_MODEL), F32),
        compiler_params=_cparams(("arbitrary", "arbitrary")),
        name=f"moe_dispatch_l{li}",
    )(pos, jnp.concatenate([offs + cnt, tail_start]), jnp.concatenate([padded - cnt, tail_rows]), h2_flat)

    used = lambda i, te, nu: jnp.minimum(i, nu[0] - 1)
    ys = pl.pallas_call(
        _grouped_kernel,
        grid_spec=pltpu.PrefetchScalarGridSpec(
            num_scalar_prefetch=2,
            grid=(n_tiles,),
            in_specs=[
                pl.BlockSpec((EXP_TILE, D_MODEL), lambda i, te, nu: (used(i, te, nu), 0)),
                pl.BlockSpec((None, D_MODEL, 2 * D_FF), lambda i, te, nu: (te[i], 0, 0)),
                pl.BlockSpec((None, D_FF, D_MODEL), lambda i, te, nu: (te[i], 0, 0))],
            out_specs=pl.BlockSpec((EXP_TILE, D_MODEL), lambda i, te, nu: (i, 0))),
        out_shape=jax.ShapeDtypeStruct((n_slots, D_MODEL), F32),
        compiler_params=_cparams(("arbitrary",), VMEM_LIMIT),
        name=f"moe_grouped_l{li}",
    )(tile_e, n_used, xs, wgu, wd)

    row = pl.BlockSpec((1, D_MODEL), lambda b, j, pos: (0, 0))
    tok = lambda w: pl.BlockSpec((None, TILE, w), lambda b, j, pos: (b, j, 0))
    mod = pl.BlockSpec((None, None, None, 1, D_MODEL),
                       lambda b, j, pos: (li, jnp.where(j + j0 == 0, CTX_MOD_ROW, b), 5, 0, 0))
    return pl.pallas_call(
        _combine_kernel,
        grid_spec=pltpu.PrefetchScalarGridSpec(
            num_scalar_prefetch=1,
            grid=(BATCH, nt),
            in_specs=[any_spec, tok(LANES), tok(D_MODEL), mod, row, row],
            out_specs=tok(D_MODEL),
            scratch_shapes=[pltpu.VMEM((2, TILE, D_MODEL), F32), pltpu.SemaphoreType.DMA(())]),
        out_shape=jax.ShapeDtypeStruct((BATCH, nt * TILE, D_MODEL), F32),
        compiler_params=_cparams(("arbitrary", "arbitrary"), VMEM_LIMIT),
        name=f"moe_combine_l{li}",
    )(pos, ys, route, x1, mod5, lg, lb)


def _rope_tables(dim):
    rows_n = SEQ // GRID_W
    t_row = jnp.repeat(jnp.arange(rows_n, dtype=F32), GRID_W)
    t_col = jnp.tile(jnp.arange(GRID_W, dtype=F32), rows_n)
    quarter = dim // 4
    inv = ROPE_BASE ** (-jnp.arange(quarter, dtype=F32) / quarter)
    ang = jnp.concatenate([t_row[:, None] * inv, t_col[:, None] * inv], axis=-1)
    cos, sin = jnp.cos(ang), jnp.sin(ang)
    reps = LANES // dim
    cos_f = jnp.tile(jnp.concatenate([cos, cos], axis=-1), (1, reps))
    sin_s = jnp.tile(jnp.concatenate([-sin, sin], axis=-1), (1, reps))
    cos_f = jnp.concatenate([jnp.ones((CTX_LEN, LANES), F32), cos_f], axis=0)
    sin_s = jnp.concatenate([jnp.zeros((CTX_LEN, LANES), F32), sin_s], axis=0)
    return cos_f, sin_s


def _head_expand(offset):
    rows = jnp.arange(LANES)[:, None]
    cols = jnp.arange(D_INNER)[None, :]
    return (rows == cols // HD_S + offset).astype(BF16)


def kernel(x, c, ctx, c_ctx, w_ada, b_ada, w_in, attn_sink, conv_w, conv_b, dt_bias, a_log, d_skip,
           ssm_norm_w, lam_q, lam_k, diff_norm_w, w_out, ln1_g, ln1_b, ln2_g, ln2_b, ffn_w_gu,
           ffn_w_down, router_w, router_b, exp_w_gu, exp_w_down):
    x_all = jnp.concatenate([ctx, x], axis=1)
    c_all = jnp.concatenate(
        [c, c_ctx[None, :], jnp.zeros((MOD_ROWS - BATCH - 1, D_MODEL), F32)], axis=0)
    mod5 = _ada_all(c_all, w_ada, b_ada).reshape(DEPTH, MOD_ROWS, 6, 1, D_MODEL)
    ropes = _rope_tables(HD_A) + _rope_tables(HD_C)
    ef, eb = _head_expand(0), _head_expand(N_HS)

    for li in range(DEPTH):
        ctx_out = li < DEPTH - 1
        j0 = 0 if ctx_out else CTX_LEN // TILE
        lam_init = 0.8 - 0.6 * math.exp(-0.3 * li)
        w = w_in[li]
        wq = w[:, :384].reshape(D_MODEL, N_KVA, 3, HD_A).transpose(0, 2, 1, 3).reshape(D_MODEL, 384)
        w_re = jnp.concatenate(
            [wq, w[:, 384:1920], w[:, 1932:2700], w[:, 1920:1932],
             jnp.zeros((D_MODEL, D_IN_PAD - 2700), F32)], axis=1).astype(BF16)
        qa, ka, va, z, xbc, dt, qc, kc, vc = _inproj(li, x_all, mod5, w_re, ropes)

        sink_p = attn_sink[li].reshape(N_KVA, 3).T.reshape(N_HA)
        ya_lat, ya_ctx = _attn_a(li, qa, ka, va, sink_p, ctx_out)
        nw_c = jnp.tile(diff_norm_w[li], N_HC)[None, :]
        yd_lat, yd_ctx = _attn_c(li, qc, kc, vc, lam_q[li], lam_k[li], nw_c, lam_init, ctx_out)
        cw = jnp.pad(conv_w[li], ((0, 8 - CONV_K), (0, 0)))
        pad12 = lambda a: jnp.pad(a.reshape(1, 2 * N_HS), ((0, 0), (0, LANES - 2 * N_HS)))
        yb = _ssd(li, xbc, z, dt, cw, conv_b[li][None, :], pad12(dt_bias[li]), pad12(a_log[li]),
                  jnp.repeat(d_skip[li], HD_S)[None, :], ssm_norm_w[li][None, :], ef, eb)

        wo = w_out[li]
        wo_a = wo[:384].reshape(N_KVA, 3, HD_A, D_MODEL).transpose(1, 0, 2, 3).reshape(384, D_MODEL)
        wo_re = jnp.concatenate([wo_a, wo[384:]], axis=0).astype(BF16)
        moe = li % 2 == 1
        fj = li // 2
        if moe:
            wr_f = jnp.pad(router_w[fj], ((0, 0), (0, LANES - N_EXP)))
            wr_hi = lax.bitcast_convert_type(
                lax.bitcast_convert_type(wr_f, jnp.uint32) & jnp.uint32(0xFFFF0000), F32)
            wr = jnp.concatenate([wr_hi.astype(BF16), (wr_f - wr_hi).astype(BF16)], axis=1)
            br = jnp.pad(router_b[fj], (0, LANES - N_EXP))[None, :]
        else:
            wr = br = None
        res = _outproj(li, ya_lat, ya_ctx, yb, yd_lat, yd_ctx, x_all, mod5, ln1_g[li][None, :],
                       ln1_b[li][None, :], wo_re, moe, wr, br, j0)
        lg2, lb2 = ln2_g[li][None, :], ln2_b[li][None, :]
        if moe:
            x1, h2, route, counts = res
            wgu, wd = exp_w_gu[fj].astype(BF16), exp_w_down[fj].astype(BF16)
            x_all = _ffn_routed(li, x1, h2, route, counts, mod5, lg2, lb2, wgu, wd, j0)
        else:
            x1, h2 = res
            wgu, wd = ffn_w_gu[fj].astype(BF16), ffn_w_down[fj].astype(BF16)
            x_all = _ffn_dense(li, x1, h2, mod5, lg2, lb2, wgu, wd)
    return x_all
```

```python
import functools
import math

import jax
import jax.numpy as jnp
from jax import lax
from jax.experimental import pallas as pl
from jax.experimental.pallas import tpu as pltpu

F32 = jnp.float32
BF16 = jnp.bfloat16

D_MODEL = 1024
BATCH = 8
SEQ = 2048
DEPTH = 4
GRID_W = 64
CTX_LEN = 256
BLK = 128
WINDOW = 128
ROPE_BASE = 10000.0
EPS = 1e-6
LOG2E = 1.4426950408889634
HD_A = 64
N_HA = 6
N_KVA = 2
D_INNER = 384
HD_S = 64
N_HS = 6
D_STATE = 128
CONV_K = 5
CONV_DIM = 896
CHUNK = 128
N_HC = 4
HD_C = 32
D_C = 256
D_FF = 2816
N_EXP = 8
ALPHA = (2 * DEPTH) ** 0.25

LANES = 128
TILE = 256
ROWS = CTX_LEN + SEQ
N_TILES = ROWS // TILE
N_CHUNKS = ROWS // CHUNK
N_CTX_CHUNKS = CTX_LEN // CHUNK
FF_CHUNK = 256
N_FF_CHUNKS = D_FF // FF_CHUNK
D_IN_PAD = 2816
MOD_ROWS = 16
CTX_MOD_ROW = BATCH
VMEM_LIMIT = 56 * 1024 * 1024


def _cparams(sem, vmem=None):
    return pltpu.CompilerParams(dimension_semantics=sem, vmem_limit_bytes=vmem)


def _sigmoid(x):
    return 1.0 / (1.0 + jnp.exp(-x))


def _silu(x):
    return x * _sigmoid(x)


def _ln(x):
    mu = jnp.mean(x, axis=-1, keepdims=True)
    xc = x - mu
    var = jnp.mean(xc * xc, axis=-1, keepdims=True)
    return xc * lax.rsqrt(var + EPS)


def _dot(a, b, precision=None):
    return jnp.dot(a, b, preferred_element_type=F32, precision=precision)


def _split_bf16(x, parts):
    out = []
    for _ in range(parts - 1):
        hi = x.astype(BF16)
        out.append(hi)
        x = x - hi.astype(F32)
    out.append(x.astype(BF16))
    return out


def _dot_nt(a, b):
    return lax.dot_general(a, b, (((1,), (1,)), ((), ())), preferred_element_type=F32)


def _ada_kernel(c_ref, w_ref, b_ref, o_ref):
    act = _silu(c_ref[...])
    o_ref[...] = _dot(act.astype(BF16), w_ref[...].astype(BF16)) + b_ref[...]


def _ada_all(c_all, w_ada, b_ada):
    tn = 1536
    return pl.pallas_call(
        _ada_kernel,
        grid=(DEPTH, 6 * D_MODEL // tn),
        in_specs=[
            pl.BlockSpec((MOD_ROWS, D_MODEL), lambda l, j: (0, 0)),
            pl.BlockSpec((None, D_MODEL, tn), lambda l, j: (l, 0, j)),
            pl.BlockSpec((None, 1, tn), lambda l, j: (l, 0, j)),
        ],
        out_specs=pl.BlockSpec((None, MOD_ROWS, tn), lambda l, j: (l, 0, j)),
        out_shape=jax.ShapeDtypeStruct((DEPTH, MOD_ROWS, 6 * D_MODEL), F32),
        compiler_params=_cparams(("parallel", "parallel")),
        name="ada_ln",
    )(c_all, w_ada, b_ada.reshape(DEPTH, 1, 6 * D_MODEL))


def _mod_spec(li, piece, j0):
    return pl.BlockSpec(
        (None, None, None, 1, D_MODEL),
        lambda b, j: (li, jnp.where(j + j0 == 0, CTX_MOD_ROW, b), piece, 0, 0))


def _tok_spec(width, j0=0):
    return pl.BlockSpec((None, TILE, width), lambda b, j: (b, j + j0, 0))


def _full_spec(shape):
    n = len(shape)
    return pl.BlockSpec(shape, lambda *_: (0,) * n)


def _rope(v, cos, sin_signed, half):
    lane = lax.broadcasted_iota(jnp.int32, (v.shape[0], LANES), 1)
    first = (lane % (2 * half)) < half
    outs = []
    for s in range(v.shape[1] // LANES):
        blk = v[:, s * LANES:(s + 1) * LANES]
        swapped = jnp.where(first, pltpu.roll(blk, LANES - half, 1), pltpu.roll(blk, half, 1))
        outs.append(blk * cos + swapped * sin_signed)
    return jnp.concatenate(outs, axis=1)


def _inproj_kernel(x_ref, sh_ref, sc_ref, w_ref, cosa_ref, sina_ref, cosc_ref, sinc_ref,
                   qa_ref, ka_ref, va_ref, z_ref, xbc_ref, dt_ref, qc_ref, kc_ref, vc_ref):
    h = _ln(x_ref[...]) * (1.0 + sc_ref[...]) + sh_ref[...]
    hb = h.astype(BF16)

    def proj(lo, hi):
        return _dot(hb, w_ref[:, lo:hi])

    cosa, sina = cosa_ref[...], sina_ref[...]
    cosc, sinc = cosc_ref[...], sinc_ref[...]
    qa_ref[...] = (_rope(proj(0, 384), cosa, sina, HD_A // 2) * (HD_A ** -0.5)).astype(BF16)
    ka_ref[...] = _rope(proj(384, 512), cosa, sina, HD_A // 2).astype(BF16)
    va_ref[...] = proj(512, 640).astype(BF16)
    z_ref[...] = proj(640, 1024)
    xbc_ref[...] = proj(1024, 1920)
    qc_ref[...] = (_rope(proj(1920, 2176), cosc, sinc, HD_C // 2) * (HD_C ** -0.5 * LOG2E)).astype(BF16)
    kc_ref[...] = _rope(proj(2176, 2432), cosc, sinc, HD_C // 2).astype(BF16)
    vc = proj(2432, 2688)
    lane = lax.broadcasted_iota(jnp.int32, vc.shape, 1)
    for h in range(N_HC):
        in_head = jnp.where(lane >= h * 2 * HD_C, lane - h * 2 * HD_C, 2 * HD_C) < 2 * HD_C
        vc_ref[h] = jnp.where(in_head, vc, 1.0).astype(BF16)
    dt_ref[...] = proj(2688, 2816)


def _inproj(li, x_all, mod5, w_re, ropes):
    widths = (384, 128, 128, 384, CONV_DIM, LANES, D_C, D_C, D_C)
    dtypes = (BF16, BF16, BF16, F32, F32, F32, BF16, BF16, BF16)
    rope_spec = pl.BlockSpec((TILE, LANES), lambda b, j: (j, 0))
    return pl.pallas_call(
        _inproj_kernel,
        grid=(BATCH, N_TILES),
        in_specs=[_tok_spec(D_MODEL), _mod_spec(li, 0, 0), _mod_spec(li, 1, 0),
                  _full_spec((D_MODEL, D_IN_PAD)), rope_spec, rope_spec, rope_spec, rope_spec],
        out_specs=[_tok_spec(w) for w in widths[:-1]]
        + [pl.BlockSpec((None, N_HC, TILE, D_C), lambda b, j: (b, 0, j, 0))],
        out_shape=[jax.ShapeDtypeStruct((BATCH, ROWS, w), d) for w, d in zip(widths[:-1], dtypes)]
        + [jax.ShapeDtypeStruct((BATCH, N_HC, ROWS, D_C), BF16)],
        compiler_params=_cparams(("parallel", "parallel"), VMEM_LIMIT),
        name=f"inproj_l{li}",
    )(x_all, mod5, mod5, w_re, *ropes)


def _attn_a_kernel(*refs, local):
    if local:
        (sink_ref, q_ref, kp_ref, kc_ref, kn_ref, kx_ref,
         vp_ref, vc_ref, vn_ref, vx_ref, o_ref) = refs
    else:
        sink_ref, q_ref, kx_ref, vx_ref, o_ref = refs
    tq = q_ref.shape[0]
    q = q_ref[...].astype(F32)
    lane_lo = lax.broadcasted_iota(jnp.int32, (tq, LANES), 1) < HD_A
    parts = []
    for s in range(3):
        blk = q[:, s * LANES:(s + 1) * LANES]
        parts.append(jnp.where(lane_lo, blk, 0.0))
        parts.append(jnp.where(lane_lo, 0.0, blk))
    qst = jnp.concatenate(parts, axis=0).astype(BF16)
    if local:
        k = jnp.concatenate([kp_ref[...], kc_ref[...], kn_ref[...], kx_ref[...]], axis=0)
        v = jnp.concatenate([vp_ref[...], vc_ref[...], vn_ref[...], vx_ref[...]], axis=0)
        n = pl.program_id(1)
        r = lax.broadcasted_iota(jnp.int32, (tq, 3 * BLK), 0)
        c = lax.broadcasted_iota(jnp.int32, (tq, 3 * BLK), 1)
        prev_v = jnp.where(n > 0, c - r, -1)
        next_v = jnp.where(n < SEQ // BLK - 1, r - (c - 2 * BLK), -1)
        valid = jnp.where(c < BLK, prev_v, jnp.where(c < 2 * BLK, 0, next_v)) >= 0
    else:
        k, v = kx_ref[...], vx_ref[...]
    s_all = _dot_nt(qst, k)
    probs = []
    for rb in range(6):
        sink = sink_ref[rb]
        sh = s_all[rb * tq:(rb + 1) * tq]
        if local:
            s_loc = jnp.where(valid, sh[:, :3 * BLK], -jnp.inf)
            s_ctx = sh[:, 3 * BLK:]
            m = jnp.maximum(jnp.max(s_loc, axis=-1, keepdims=True),
                            jnp.max(s_ctx, axis=-1, keepdims=True))
            m = jnp.maximum(m, sink)
            p_loc = jnp.exp(s_loc - m)
            p_ctx = jnp.exp(s_ctx - m)
            den = (jnp.sum(p_loc, axis=-1, keepdims=True) + jnp.sum(p_ctx, axis=-1, keepdims=True)
                   + jnp.exp(sink - m))
            p = jnp.concatenate([p_loc, p_ctx], axis=1) / den
        else:
            m = jnp.maximum(jnp.max(sh, axis=-1, keepdims=True), sink)
            p_ctx = jnp.exp(sh - m)
            den = jnp.sum(p_ctx, axis=-1, keepdims=True) + jnp.exp(sink - m)
            p = p_ctx / den
        probs.append(p.astype(BF16))
    o_all = _dot(jnp.concatenate(probs, axis=0), v)
    outs = []
    for s in range(3):
        o0 = o_all[(2 * s) * tq:(2 * s + 1) * tq]
        o1 = o_all[(2 * s + 1) * tq:(2 * s + 2) * tq]
        outs.append(jnp.where(lane_lo, o0, o1))
    o_ref[...] = jnp.concatenate(outs, axis=1).astype(BF16)


def _attn_a(li, qa, ka, va, sink_p, with_ctx):
    nb = SEQ // BLK
    off = CTX_LEN // BLK
    smem = pl.BlockSpec(memory_space=pltpu.SMEM)

    def kv(shift):
        return pl.BlockSpec(
            (None, BLK, LANES), lambda b, n: (b, off + jnp.clip(n + shift, 0, nb - 1), 0))

    ctx_kv = pl.BlockSpec((None, CTX_LEN, LANES), lambda b, n: (b, 0, 0))
    q_spec = pl.BlockSpec((None, BLK, 384), lambda b, n: (b, off + n, 0))
    y_lat = pl.pallas_call(
        functools.partial(_attn_a_kernel, local=True),
        grid=(BATCH, nb),
        in_specs=[smem, q_spec, kv(-1), kv(0), kv(1), ctx_kv, kv(-1), kv(0), kv(1), ctx_kv],
        out_specs=pl.BlockSpec((None, BLK, 384), lambda b, n: (b, n, 0)),
        out_shape=jax.ShapeDtypeStruct((BATCH, SEQ, 384), BF16),
        compiler_params=_cparams(("parallel", "parallel")),
        name=f"attn_a_l{li}",
    )(sink_p, qa, ka, ka, ka, ka, va, va, va, va)
    if not with_ctx:
        return y_lat, None
    ctx_spec384 = pl.BlockSpec((None, CTX_LEN, 384), lambda b: (b, 0, 0))
    ctx_spec128 = pl.BlockSpec((None, CTX_LEN, LANES), lambda b: (b, 0, 0))
    y_ctx = pl.pallas_call(
        functools.partial(_attn_a_kernel, local=False),
        grid=(BATCH,),
        in_specs=[smem, ctx_spec384, ctx_spec128, ctx_spec128],
        out_specs=ctx_spec384,
        out_shape=jax.ShapeDtypeStruct((BATCH, CTX_LEN, 384), BF16),
        compiler_params=_cparams(("parallel",)),
        name=f"attn_a_ctx_l{li}",
    )(sink_p, qa, ka, va)
    return y_lat, y_ctx


def _attn_c_kernel(lq_ref, lk_ref, nw_ref, *refs, lam_init):
    q_refs, (k_ref, v_ref, o_ref) = refs[:-3], refs[-3:]
    tq = o_ref.shape[0]
    e = jnp.exp(jnp.sum(lq_ref[...] * lk_ref[...], axis=-1, keepdims=True))
    lam = e[0:1] - e[1:2] + lam_init
    q = jnp.concatenate([r[...] for r in q_refs], axis=0).astype(F32)
    k = k_ref[...]
    lane = lax.broadcasted_iota(jnp.int32, (tq, D_C), 1)
    lane2 = lax.broadcasted_iota(jnp.int32, (2 * tq, D_C), 1)
    acc = jnp.zeros((tq, D_C), F32)
    for h in range(N_HC):
        base = h * 2 * HD_C
        in0 = jnp.where(lane >= base, lane - base, 2 * HD_C) < HD_C
        in1 = jnp.where(lane >= base + HD_C, lane - base - HD_C, HD_C) < HD_C
        qst = jnp.concatenate([jnp.where(in0, q, 0.0), jnp.where(in1, q, 0.0)], axis=0).astype(BF16)
        s = _dot_nt(qst, k)
        p = jnp.exp2(s - jnp.max(s, axis=-1, keepdims=True)).astype(BF16)
        pv = _dot(p, v_ref[h])
        den = jnp.sum(jnp.where(lane2 == (base + 2 * HD_C) % D_C, pv, 0.0), axis=-1, keepdims=True)
        pv = pv * (1.0 / den)
        o = pv[:tq] - lam * pv[tq:]
        inh = jnp.where(lane >= base, lane - base, 2 * HD_C) < 2 * HD_C
        oh = jnp.where(inh, o, 0.0)
        ms = jnp.sum(oh * oh, axis=-1, keepdims=True) * (1.0 / (2 * HD_C))
        acc = acc + oh * lax.rsqrt(ms + EPS)
    o_ref[...] = (acc * nw_ref[...] * (1.0 - lam_init)).astype(BF16)


def _attn_c(li, qc, kc, vc, lam_q, lam_k, nw_row, lam_init, with_ctx):
    parts = 2
    tq = parts * TILE
    off = CTX_LEN // TILE
    q_specs = [pl.BlockSpec((None, TILE, D_C), lambda b, n, i=i: (b, off + parts * n + i, 0))
               for i in range(parts)]
    small = [_full_spec((2, HD_C)), _full_spec((2, HD_C)), _full_spec((1, D_C))]
    kern = functools.partial(_attn_c_kernel, lam_init=lam_init)
    all_kv = pl.BlockSpec((None, ROWS, D_C), lambda b, n: (b, 0, 0))
    all_v = pl.BlockSpec((None, N_HC, ROWS, D_C), lambda b, n: (b, 0, 0, 0))
    y_lat = pl.pallas_call(
        kern,
        grid=(BATCH, SEQ // tq),
        in_specs=small + q_specs + [all_kv, all_v],
        out_specs=pl.BlockSpec((None, tq, D_C), lambda b, n: (b, n, 0)),
        out_shape=jax.ShapeDtypeStruct((BATCH, SEQ, D_C), BF16),
        compiler_params=_cparams(("parallel", "parallel"), VMEM_LIMIT),
        name=f"attn_c_l{li}",
    )(lam_q, lam_k, nw_row, *([qc] * parts), kc, vc)
    if not with_ctx:
        return y_lat, None
    ctx_spec = pl.BlockSpec((None, CTX_LEN, D_C), lambda b: (b, 0, 0))
    small1 = [_full_spec((2, HD_C)), _full_spec((2, HD_C)), _full_spec((1, D_C))]
    y_ctx = pl.pallas_call(
        kern,
        grid=(BATCH,),
        in_specs=small1 + [ctx_spec, ctx_spec,
                           pl.BlockSpec((None, N_HC, CTX_LEN, D_C), lambda b: (b, 0, 0, 0))],
        out_specs=ctx_spec,
        out_shape=jax.ShapeDtypeStruct((BATCH, CTX_LEN, D_C), BF16),
        compiler_params=_cparams(("parallel",)),
        name=f"attn_c_ctx_l{li}",
    )(lam_q, lam_k, nw_row, qc, kc, vc)
    return y_lat, y_ctx


def _ssd_kernel(xbc_ref, z_ref, dt_ref, cw_ref, cb_ref, dtb_ref, alog_ref, dsk_ref, nw_ref,
                ef_ref, eb_ref, y_ref, u_scr, yacc_scr, st_scr):
    cw = cw_ref[...]
    cb = cb_ref[...]
    dsk = dsk_ref[...]
    pad = 8
    for c in range(N_CHUNKS):
        r0 = CHUNK * c
        zeros = jnp.zeros((pad, CONV_DIM), F32)
        if c in (0, N_CTX_CHUNKS):
            blk = jnp.concatenate([zeros, xbc_ref[r0:r0 + CHUNK + pad, :]], axis=0)
        elif c in (N_CTX_CHUNKS - 1, N_CHUNKS - 1):
            blk = jnp.concatenate([xbc_ref[r0 - pad:r0 + CHUNK, :], zeros], axis=0)
        else:
            blk = xbc_ref[r0 - pad:r0 + CHUNK + pad, :]
        acc = None
        for j in range(CONV_K):
            sh = (CONV_K // 2 - j) % (CHUNK + 2 * pad)
            rolled = blk if sh == 0 else pltpu.roll(blk, sh, 0)
            term = rolled[pad:pad + CHUNK, :] * cw[j:j + 1, :]
            acc = term if acc is None else acc + term
        u = _silu(acc + cb)
        u_scr[r0:r0 + CHUNK, :] = u
        yacc_scr[r0:r0 + CHUNK, :] = u[:, :D_INNER] * dsk

    st_scr[...] = jnp.zeros(st_scr.shape, F32)
    ri = lax.broadcasted_iota(jnp.int32, (CHUNK, CHUNK), 0)
    ci = lax.broadcasted_iota(jnp.int32, (CHUNK, CHUNK), 1)
    lower = ri >= ci
    upper = ri <= ci
    tri = (jnp.where(lower, 1.0, 0.0).astype(BF16), jnp.where(upper, 1.0, 0.0).astype(BF16))
    a_row = -jnp.exp(alog_ref[...])
    dtb = dtb_ref[...]
    lane_lo = ci < HD_S

    def chunk_dir(c, d):
        r0 = pl.multiple_of(c * CHUNK, CHUNK)
        u = u_scr[pl.ds(r0, CHUNK), :]
        xs = u[:, :D_INNER]
        bm = u[:, D_INNER:D_INNER + 2 * D_STATE]
        cm = u[:, D_INNER + 2 * D_STATE:]
        vv = dt_ref[pl.ds(r0, CHUNK), :] + dtb
        dt = jnp.maximum(vv, 0.0) + jnp.log(1.0 + jnp.exp(-jnp.abs(vv)))
        a = dt * a_row
        cum = sum(_dot(tri[d], piece) for piece in _split_bf16(a, 3))
        cum_t = cum.T
        tot = cum[CHUNK - 1:CHUNK, :] if d == 0 else cum[0:1, :]
        expand = ef_ref[...] if d == 0 else eb_ref[...]
        per_head = jnp.concatenate([dt, jnp.exp(cum), jnp.exp(tot - cum)], axis=0)
        per_lane = sum(_dot(piece, expand) for piece in _split_bf16(per_head, 2))
        dt_x = per_lane[:CHUNK]
        dec_out_x = per_lane[CHUNK:2 * CHUNK]
        dec_st_x = per_lane[2 * CHUNK:]
        etot_x = dec_out_x[CHUNK - 1:CHUNK, :] if d == 0 else dec_out_x[0:1, :]
        xdt = xs * dt_x
        xdt_b = xdt.astype(BF16)
        xst_b = (xdt * dec_st_x).astype(BF16)
        cb_ = [cm[:, g * D_STATE:(g + 1) * D_STATE].astype(BF16) for g in range(2)]
        bb_ = [bm[:, g * D_STATE:(g + 1) * D_STATE].astype(BF16) for g in range(2)]
        bt_ = [bm[:, g * D_STATE:(g + 1) * D_STATE].T.astype(BF16) for g in range(2)]
        cbm = [_dot_nt(cb_[g], bb_[g]) for g in range(2)]
        mask = lower if d == 0 else upper
        for j in range(3):
            sl = slice(j * LANES, (j + 1) * LANES)
            parts = []
            for h in (2 * j, 2 * j + 1):
                hl = h + N_HS * d
                seg = jnp.where(mask, cum[:, hl:hl + 1] - cum_t[hl:hl + 1, :], -jnp.inf)
                sc = (cbm[h // 3] * jnp.exp(seg)).astype(BF16)
                parts.append(_dot(sc, xdt_b[:, sl]))
            y_diag = jnp.where(lane_lo, parts[0], parts[1])
            st = st_scr[d, j]
            st_b = st.astype(BF16)
            g0, g1 = (2 * j) // 3, (2 * j + 1) // 3
            if g0 == g1:
                y_off = _dot(cb_[g0], st_b)
                upd = _dot(bt_[g0], xst_b[:, sl])
            else:
                y_off = jnp.where(lane_lo, _dot(cb_[g0], st_b), _dot(cb_[g1], st_b))
                upd = jnp.where(lane_lo, _dot(bt_[g0], xst_b[:, sl]), _dot(bt_[g1], xst_b[:, sl]))
            yacc_scr[pl.ds(r0, CHUNK), sl] += y_diag + y_off * dec_out_x[:, sl]
            st_scr[d, j] = etot_x[:, sl] * st + upd

    def body(i, carry):
        chunk_dir(i, 0)
        chunk_dir(jnp.where(i < N_CTX_CHUNKS, N_CTX_CHUNKS - 1 - i, N_CHUNKS + N_CTX_CHUNKS - 1 - i), 1)
        return carry

    lax.fori_loop(0, N_CHUNKS, body, 0)

    nw = nw_ref[...]
    for t in range(N_TILES):
        rows = slice(t * TILE, (t + 1) * TILE)
        y = yacc_scr[rows, :] * _silu(z_ref[rows, :])
        ms = jnp.mean(y * y, axis=-1, keepdims=True)
        y_ref[rows, :] = (y * lax.rsqrt(ms + EPS) * nw).astype(BF16)


def _ssd(li, xbc, z, dt, cw, cb, dtb, alog, dsk, nw, ef, eb):
    def seq(width):
        return pl.BlockSpec((None, ROWS, width), lambda b: (b, 0, 0))

    return pl.pallas_call(
        _ssd_kernel,
        grid=(BATCH,),
        in_specs=[seq(CONV_DIM), seq(D_INNER), seq(LANES),
                  _full_spec((8, CONV_DIM)), _full_spec((1, CONV_DIM)),
                  _full_spec((1, LANES)), _full_spec((1, LANES)),
                  _full_spec((1, D_INNER)), _full_spec((1, D_INNER)),
                  _full_spec((LANES, D_INNER)), _full_spec((LANES, D_INNER))],
        out_specs=seq(D_INNER),
        out_shape=jax.ShapeDtypeStruct((BATCH, ROWS, D_INNER), BF16),
        scratch_shapes=[pltpu.VMEM((ROWS, CONV_DIM), F32), pltpu.VMEM((ROWS, D_INNER), F32),
                        pltpu.VMEM((2, 3, D_STATE, LANES), F32)],
        compiler_params=_cparams(("parallel",), VMEM_LIMIT),
        name=f"ssd_l{li}",
    )(xbc, z, dt, cw, cb, dtb, alog, dsk, nw, ef, eb)


def _top2_gates(logits):
    lane = lax.broadcasted_iota(jnp.int32, logits.shape, 1).astype(F32)
    neg = -jnp.inf
    lg = jnp.where(lane < N_EXP, logits, neg)
    m1 = jnp.max(lg, axis=-1, keepdims=True)
    i1 = jnp.min(jnp.where(lg == m1, lane, float(LANES)), axis=-1, keepdims=True)
    lg2 = jnp.where(lane == i1, neg, lg)
    m2 = jnp.max(lg2, axis=-1, keepdims=True)
    i2 = jnp.min(jnp.where(lg2 == m2, lane, float(LANES)), axis=-1, keepdims=True)
    e2 = jnp.exp(m2 - m1)
    return lane, i1, i2, 1.0 / (1.0 + e2), e2 / (1.0 + e2)


R_E1, R_E2, R_W1, R_W2, R_RANK1, R_RANK2 = range(6)


def _route_record(logits, cnt_scr):
    lane, i1, i2, w1, w2 = _top2_gates(logits)
    rows = logits.shape[0]
    hit1, hit2 = lane == i1, lane == i2
    onehot = jnp.where(hit1, 1.0, 0.0) + jnp.where(hit2, 1.0, 0.0)
    ri = lax.broadcasted_iota(jnp.int32, (rows, rows), 0)
    ci = lax.broadcasted_iota(jnp.int32, (rows, rows), 1)
    earlier = jnp.where(ri > ci, 1.0, 0.0).astype(BF16)
    before = _dot(earlier, onehot.astype(BF16)) + cnt_scr[...]
    rank1 = jnp.sum(jnp.where(hit1, before, 0.0), axis=-1, keepdims=True)
    rank2 = jnp.sum(jnp.where(hit2, before, 0.0), axis=-1, keepdims=True)
    cnt_scr[...] += jnp.sum(onehot, axis=0, keepdims=True)
    rec = jnp.zeros(logits.shape, F32)
    for idx, val in ((R_E1, i1), (R_E2, i2), (R_W1, w1), (R_W2, w2), (R_RANK1, rank1), (R_RANK2, rank2)):
        rec = jnp.where(lane == float(idx), val, rec)
    return rec


def _outproj_kernel(*refs, moe, ctx_split):
    if ctx_split:
        ya_l, ya_c, yb_ref, yd_l, yd_c = refs[:5]
        refs = refs[5:]
    else:
        ya_l, yb_ref, yd_l = refs[:3]
        refs = refs[3:]
    x_ref, g1_ref, lg_ref, lb_ref, sh_ref, sc_ref, wo_ref = refs[:7]
    refs = refs[7:]
    if moe:
        wr_ref, br_ref, x1_ref, h2_ref, route_ref, counts_ref, cnt_scr = refs

        @pl.when((pl.program_id(0) == 0) & (pl.program_id(1) == 0))
        def _():
            cnt_scr[...] = jnp.zeros(cnt_scr.shape, F32)
    else:
        x1_ref, h2_ref = refs
    if ctx_split:
        is_ctx = pl.program_id(1) == 0
        ya = jnp.where(is_ctx, ya_c[...], ya_l[...])
        yd = jnp.where(is_ctx, yd_c[...], yd_l[...])
    else:
        ya, yd = ya_l[...], yd_l[...]
    mix = (_dot(ya, wo_ref[0:384, :]) + _dot(yb_ref[...], wo_ref[384:768, :])
           + _dot(yd, wo_ref[768:1024, :]))
    x1 = _ln(ALPHA * x_ref[...] + g1_ref[...] * mix) * lg_ref[...] + lb_ref[...]
    x1_ref[...] = x1
    h2 = _ln(x1) * (1.0 + sc_ref[...]) + sh_ref[...]
    h2_ref[...] = h2.astype(h2_ref.dtype)
    if moe:
        h_hi, h_mid = _split_bf16(h2, 2)
        both = _dot(h_hi, wr_ref[...])
        logits = (both[:, :LANES] + both[:, LANES:] + _dot(h_mid, wr_ref[:, :LANES])) + br_ref[...]
        route_ref[...] = _route_record(logits, cnt_scr)
        counts_ref[...] = cnt_scr[...]


def _outproj(li, ya_lat, ya_ctx, yb, yd_lat, yd_ctx, x_all, mod5, lg, lb, wo, moe, wr, br, j0):
    ctx_split = j0 == 0
    nt = N_TILES - j0
    off = CTX_LEN // TILE
    lat = lambda w: pl.BlockSpec((None, TILE, w), lambda b, j: (b, jnp.maximum(j + j0 - off, 0), 0))
    ctx = lambda w: pl.BlockSpec((None, TILE, w), lambda b, j: (b, 0, 0))
    row = _full_spec((1, D_MODEL))
    if ctx_split:
        ins = [ya_lat, ya_ctx, yb, yd_lat, yd_ctx]
        specs = [lat(384), ctx(384), _tok_spec(D_INNER, j0), lat(D_C), ctx(D_C)]
    else:
        ins = [ya_lat, yb, yd_lat]
        specs = [lat(384), _tok_spec(D_INNER, j0), lat(D_C)]
    ins += [x_all, mod5, lg, lb, mod5, mod5, wo]
    specs += [_tok_spec(D_MODEL, j0), _mod_spec(li, 2, j0), row, row,
              _mod_spec(li, 3, j0), _mod_spec(li, 4, j0), _full_spec((D_MODEL, D_MODEL))]
    outs = [jax.ShapeDtypeStruct((BATCH, nt * TILE, D_MODEL), F32),
            jax.ShapeDtypeStruct((BATCH, nt * TILE, D_MODEL), F32 if moe else BF16)]
    out_specs = [_tok_spec(D_MODEL), _tok_spec(D_MODEL)]
    scratch = []
    if moe:
        ins += [wr, br]
        specs += [_full_spec((D_MODEL, 2 * LANES)), _full_spec((1, LANES))]
        outs += [jax.ShapeDtypeStruct((BATCH, nt * TILE, LANES), F32),
                 jax.ShapeDtypeStruct((1, LANES), F32)]
        out_specs += [_tok_spec(LANES), _full_spec((1, LANES))]
        scratch = [pltpu.VMEM((1, LANES), F32)]
    sem = ("arbitrary", "arbitrary") if moe else ("parallel", "parallel")
    return pl.pallas_call(
        functools.partial(_outproj_kernel, moe=moe, ctx_split=ctx_split),
        grid=(BATCH, nt),
        in_specs=specs,
        out_specs=out_specs,
        out_shape=outs,
        scratch_shapes=scratch,
        compiler_params=_cparams(sem, VMEM_LIMIT),
        name=f"outproj_l{li}",
    )(*ins)


def _swiglu_acc(h2, wgu_ref, wd_ref):
    acc = jnp.zeros((h2.shape[0], D_MODEL), F32)
    for c in range(N_FF_CHUNKS):
        cols = slice(c * FF_CHUNK, (c + 1) * FF_CHUNK)
        up_cols = slice(D_FF + c * FF_CHUNK, D_FF + (c + 1) * FF_CHUNK)
        act = (_silu(_dot(h2, wgu_ref[:, cols])) * _dot(h2, wgu_ref[:, up_cols])).astype(BF16)
        acc = acc + _dot(act, wd_ref[cols, :])
    return acc


def _ffn_dense_kernel(x1_ref, h2_ref, g2_ref, lg_ref, lb_ref, wgu_ref, wd_ref, o_ref):
    y = _swiglu_acc(h2_ref[...], wgu_ref, wd_ref)
    o_ref[...] = _ln(ALPHA * x1_ref[...] + g2_ref[...] * y) * lg_ref[...] + lb_ref[...]


def _ffn_dense(li, x1, h2, mod5, lg, lb, wgu, wd):
    row = _full_spec((1, D_MODEL))
    return pl.pallas_call(
        _ffn_dense_kernel,
        grid=(BATCH, N_TILES),
        in_specs=[_tok_spec(D_MODEL), _tok_spec(D_MODEL), _mod_spec(li, 5, 0), row, row,
                  _full_spec((D_MODEL, 2 * D_FF)), _full_spec((D_FF, D_MODEL))],
        out_specs=_tok_spec(D_MODEL),
        out_shape=jax.ShapeDtypeStruct((BATCH, ROWS, D_MODEL), F32),
        compiler_params=_cparams(("parallel", "parallel"), VMEM_LIMIT),
        name=f"ffn_dense_l{li}",
    )(x1, h2, mod5, lg, lb, wgu, wd)


EXP_TILE = 256


def _slot_row_copy(src_ref, src_row, dst_ref, dst_row, sem):
    return pltpu.make_async_copy(src_ref.at[pl.ds(src_row, 1), :], dst_ref.at[pl.ds(dst_row, 1), :], sem)


def _wait_rows(src_ref, dst_ref, sem):
    pltpu.make_async_copy(src_ref, dst_ref, sem).wait()


def _dispatch_kernel(pos_ref, fill_start_ref, fill_rows_ref, h2_ref, xs_ref, zero_scr, sem, zsem):
    first = (pl.program_id(0) == 0) & (pl.program_id(1) == 0)
    sub = 8
    pad_chunks = [1 << s for s in reversed(range(sub.bit_length() - 1, EXP_TILE.bit_length() - 1))]

    def pad_copies(e):
        start, n = fill_start_ref[e], fill_rows_ref[e]
        head = (-start) & (sub - 1)
        for i in range(sub - 1):
            yield i < jnp.minimum(head, n), _slot_row_copy(zero_scr, i, xs_ref, start + i, zsem)
        body = jnp.maximum(n - head, 0)
        for rows in pad_chunks:
            off = pl.multiple_of(start + head + (body & ~(2 * rows - 1)), sub)
            dst = xs_ref.at[pl.ds(off, rows), :]
            yield (body & rows) != 0, pltpu.make_async_copy(zero_scr.at[pl.ds(0, rows), :], dst, zsem)

    @pl.when(first)
    def _():
        zero_scr[...] = jnp.zeros(zero_scr.shape, F32)
        for e in range(fill_rows_ref.shape[0]):
            for cond, copy in pad_copies(e):
                pl.when(cond)(copy.start)

    base = (pl.program_id(0) * pl.num_programs(1) + pl.program_id(1)) * (2 * TILE)

    def issue(r, carry):
        for k in range(2):
            _slot_row_copy(h2_ref, r, xs_ref, pos_ref[base + 2 * r + k], sem).start(priority=k)
        return carry

    lax.fori_loop(0, TILE, issue, 0, unroll=8)
    for k in range(2):
        _wait_rows(h2_ref, xs_ref.at[pl.ds(0, TILE), :], sem)

    @pl.when(first)
    def _():
        for e in range(fill_rows_ref.shape[0]):
            for cond, copy in pad_copies(e):
                pl.when(cond)(copy.wait)


def _grouped_kernel(tile_e_ref, n_used_ref, xs_ref, wgu_ref, wd_ref, ys_ref):
    del tile_e_ref
    live = pl.program_id(0) < n_used_ref[0]

    @pl.when(live)
    def _():
        ys_ref[...] = _swiglu_acc(xs_ref[...].astype(BF16), wgu_ref, wd_ref)

    @pl.when(jnp.logical_not(live))
    def _():
        ys_ref[...] = jnp.zeros(ys_ref.shape, F32)


def _combine_kernel(pos_ref, ys_ref, route_ref, x1_ref, g2_ref, lg_ref, lb_ref, o_ref, ybuf, sem):
    base = (pl.program_id(0) * pl.num_programs(1) + pl.program_id(1)) * (2 * TILE)

    def issue(r, carry):
        for k in range(2):
            _slot_row_copy(ys_ref, pos_ref[base + 2 * r + k], ybuf.at[k], r, sem).start(priority=k)
        return carry

    lax.fori_loop(0, TILE, issue, 0, unroll=8)
    for k in range(2):
        _wait_rows(ys_ref.at[pl.ds(0, TILE), :], ybuf.at[k], sem)
    route = route_ref[...]
    lane = lax.broadcasted_iota(jnp.int32, route.shape, 1)
    w1 = jnp.sum(jnp.where(lane == R_W1, route, 0.0), axis=-1, keepdims=True)
    w2 = jnp.sum(jnp.where(lane == R_W2, route, 0.0), axis=-1, keepdims=True)
    y = w1 * ybuf[0] + w2 * ybuf[1]
    o_ref[...] = _ln(ALPHA * x1_ref[...] + g2_ref[...] * y) * lg_ref[...] + lb_ref[...]


def _ffn_routed(li, x1, h2, route, counts, mod5, lg, lb, wgu, wd, j0):
    nt = N_TILES - j0
    n_tok = BATCH * nt * TILE
    n_tiles = 2 * n_tok // EXP_TILE + N_EXP
    n_slots = n_tiles * EXP_TILE
    cnt = counts[0, :N_EXP].astype(jnp.int32)
    padded = (cnt + EXP_TILE - 1) // EXP_TILE * EXP_TILE
    ends = jnp.cumsum(padded)
    offs = ends - padded
    n_used = ends[-1:] // EXP_TILE
    tile_start = jnp.minimum(jnp.arange(n_tiles), n_used - 1) * EXP_TILE
    tile_e = jnp.sum((tile_start[:, None] >= ends[None, :]).astype(jnp.int32), axis=1)
    rec = route.reshape(n_tok, LANES)
    experts = rec[:, R_E1:R_E2 + 1].astype(jnp.int32)
    ranks = rec[:, R_RANK1:R_RANK2 + 1].astype(jnp.int32)
    off_of = jnp.sum(jnp.where(experts[..., None] == jnp.arange(N_EXP), offs, 0), axis=-1)
    pos = (off_of + ranks).reshape(2 * n_tok)

    h2_flat = h2.reshape(n_tok, D_MODEL)
    half = EXP_TILE // 2
    tail_start = n_used * EXP_TILE + jnp.arange(2 * N_EXP) * half
    tail_rows = jnp.where(tail_start < n_slots, half, 0)
    tail_start = jnp.minimum(tail_start, n_slots - half)
    any_spec = pl.BlockSpec(memory_space=pl.ANY)
    xs = pl.pallas_call(
        _dispatch_kernel,
        grid_spec=pltpu.PrefetchScalarGridSpec(
            num_scalar_prefetch=3,
            grid=(BATCH, nt),
            in_specs=[pl.BlockSpec((TILE, D_MODEL), lambda b, j, *_: (b * nt + j, 0))],
            out_specs=any_spec,
            scratch_shapes=[pltpu.VMEM((EXP_TILE // 2, D_MODEL), F32),
                            pltpu.SemaphoreType.DMA(()), pltpu.SemaphoreType.DMA(())]),
        out_shape=jax.ShapeDtypeStruct((n_slots, D_MODEL), F32),
        compiler_params=_cparams(("arbitrary", "arbitrary")),
        name=f"moe_dispatch_l{li}",
    )(pos, jnp.concatenate([offs + cnt, tail_start]), jnp.concatenate([padded - cnt, tail_rows]), h2_flat)

    used = lambda i, te, nu: jnp.minimum(i, nu[0] - 1)
    ys = pl.pallas_call(
        _grouped_kernel,
        grid_spec=pltpu.PrefetchScalarGridSpec(
            num_scalar_prefetch=2,
            grid=(n_tiles,),
            in_specs=[
                pl.BlockSpec((EXP_TILE, D_MODEL), lambda i, te, nu: (used(i, te, nu), 0)),
                pl.BlockSpec((None, D_MODEL, 2 * D_FF), lambda i, te, nu: (te[i], 0, 0)),
                pl.BlockSpec((None, D_FF, D_MODEL), lambda i, te, nu: (te[i], 0, 0))],
            out_specs=pl.BlockSpec((EXP_TILE, D_MODEL), lambda i, te, nu: (i, 0))),
        out_shape=jax.ShapeDtypeStruct((n_slots, D_MODEL), F32),
        compiler_params=_cparams(("arbitrary",), VMEM_LIMIT),
        name=f"moe_grouped_l{li}",
    )(tile_e, n_used, xs, wgu, wd)

    row = pl.BlockSpec((1, D_MODEL), lambda b, j, pos: (0, 0))
    tok = lambda w: pl.BlockSpec((None, TILE, w), lambda b, j, pos: (b, j, 0))
    mod = pl.BlockSpec((None, None, None, 1, D_MODEL),
                       lambda b, j, pos: (li, jnp.where(j + j0 == 0, CTX_MOD_ROW, b), 5, 0, 0))
    return pl.pallas_call(
        _combine_kernel,
        grid_spec=pltpu.PrefetchScalarGridSpec(
            num_scalar_prefetch=1,
            grid=(BATCH, nt),
            in_specs=[any_spec, tok(LANES), tok(D_MODEL), mod, row, row],
            out_specs=tok(D_MODEL),
            scratch_shapes=[pltpu.VMEM((2, TILE, D_MODEL), F32), pltpu.SemaphoreType.DMA(())]),
        out_shape=jax.ShapeDtypeStruct((BATCH, nt * TILE, D_MODEL), F32),
        compiler_params=_cparams(("arbitrary", "arbitrary"), VMEM_LIMIT),
        name=f"moe_combine_l{li}",
    )(pos, ys, route, x1, mod5, lg, lb)


def _rope_tables(dim):
    rows_n = SEQ // GRID_W
    t_row = jnp.repeat(jnp.arange(rows_n, dtype=F32), GRID_W)
    t_col = jnp.tile(jnp.arange(GRID_W, dtype=F32), rows_n)
    quarter = dim // 4
    inv = ROPE_BASE ** (-jnp.arange(quarter, dtype=F32) / quarter)
    ang = jnp.concatenate([t_row[:, None] * inv, t_col[:, None] * inv], axis=-1)
    cos, sin = jnp.cos(ang), jnp.sin(ang)
    reps = LANES // dim
    cos_f = jnp.tile(jnp.concatenate([cos, cos], axis=-1), (1, reps))
    sin_s = jnp.tile(jnp.concatenate([-sin, sin], axis=-1), (1, reps))
    cos_f = jnp.concatenate([jnp.ones((CTX_LEN, LANES), F32), cos_f], axis=0)
    sin_s = jnp.concatenate([jnp.zeros((CTX_LEN, LANES), F32), sin_s], axis=0)
    return cos_f, sin_s


def _head_expand(offset):
    rows = jnp.arange(LANES)[:, None]
    cols = jnp.arange(D_INNER)[None, :]
    return (rows == cols // HD_S + offset).astype(BF16)


def kernel(x, c, ctx, c_ctx, w_ada, b_ada, w_in, attn_sink, conv_w, conv_b, dt_bias, a_log, d_skip,
           ssm_norm_w, lam_q, lam_k, diff_norm_w, w_out, ln1_g, ln1_b, ln2_g, ln2_b, ffn_w_gu,
           ffn_w_down, router_w, router_b, exp_w_gu, exp_w_down):
    x_all = jnp.concatenate([ctx, x], axis=1)
    c_all = jnp.concatenate(
        [c, c_ctx[None, :], jnp.zeros((MOD_ROWS - BATCH - 1, D_MODEL), F32)], axis=0)
    mod5 = _ada_all(c_all, w_ada, b_ada).reshape(DEPTH, MOD_ROWS, 6, 1, D_MODEL)
    ropes = _rope_tables(HD_A) + _rope_tables(HD_C)
    ef, eb = _head_expand(0), _head_expand(N_HS)

    for li in range(DEPTH):
        ctx_out = li < DEPTH - 1
        j0 = 0 if ctx_out else CTX_LEN // TILE
        lam_init = 0.8 - 0.6 * math.exp(-0.3 * li)
        w = w_in[li]
        wq = w[:, :384].reshape(D_MODEL, N_KVA, 3, HD_A).transpose(0, 2, 1, 3).reshape(D_MODEL, 384)
        w_re = jnp.concatenate(
            [wq, w[:, 384:1920], w[:, 1932:2700], w[:, 1920:1932],
             jnp.zeros((D_MODEL, D_IN_PAD - 2700), F32)], axis=1).astype(BF16)
        qa, ka, va, z, xbc, dt, qc, kc, vc = _inproj(li, x_all, mod5, w_re, ropes)

        sink_p = attn_sink[li].reshape(N_KVA, 3).T.reshape(N_HA)
        ya_lat, ya_ctx = _attn_a(li, qa, ka, va, sink_p, ctx_out)
        nw_c = jnp.tile(diff_norm_w[li], N_HC)[None, :]
        yd_lat, yd_ctx = _attn_c(li, qc, kc, vc, lam_q[li], lam_k[li], nw_c, lam_init, ctx_out)
        cw = jnp.pad(conv_w[li], ((0, 8 - CONV_K), (0, 0)))
        pad12 = lambda a: jnp.pad(a.reshape(1, 2 * N_HS), ((0, 0), (0, LANES - 2 * N_HS)))
        yb = _ssd(li, xbc, z, dt, cw, conv_b[li][None, :], pad12(dt_bias[li]), pad12(a_log[li]),
                  jnp.repeat(d_skip[li], HD_S)[None, :], ssm_norm_w[li][None, :], ef, eb)

        wo = w_out[li]
        wo_a = wo[:384].reshape(N_KVA, 3, HD_A, D_MODEL).transpose(1, 0, 2, 3).reshape(384, D_MODEL)
        wo_re = jnp.concatenate([wo_a, wo[384:]], axis=0).astype(BF16)
        moe = li % 2 == 1
        fj = li // 2
        if moe:
            wr_f = jnp.pad(router_w[fj], ((0, 0), (0, LANES - N_EXP)))
            wr_hi = lax.bitcast_convert_type(
                lax.bitcast_convert_type(wr_f, jnp.uint32) & jnp.uint32(0xFFFF0000), F32)
            wr = jnp.concatenate([wr_hi.astype(BF16), (wr_f - wr_hi).astype(BF16)], axis=1)
            br = jnp.pad(router_b[fj], (0, LANES - N_EXP))[None, :]
        else:
            wr = br = None
        res = _outproj(li, ya_lat, ya_ctx, yb, yd_lat, yd_ctx, x_all, mod5, ln1_g[li][None, :],
                       ln1_b[li][None, :], wo_re, moe, wr, br, j0)
        lg2, lb2 = ln2_g[li][None, :], ln2_b[li][None, :]
        if moe:
            x1, h2, route, counts = res
            wgu, wd = exp_w_gu[fj].astype(BF16), exp_w_down[fj].astype(BF16)
            x_all = _ffn_routed(li, x1, h2, route, counts, mod5, lg2, lb2, wgu, wd, j0)
        else:
            x1, h2 = res
            wgu, wd = ffn_w_gu[fj].astype(BF16), ffn_w_down[fj].astype(BF16)
            x_all = _ffn_dense(li, x1, h2, mod5, lg2, lb2, wgu, wd)
    return x_all
```

```python
import functools
import math

import jax
import jax.numpy as jnp
from jax import lax
from jax.experimental import pallas as pl
from jax.experimental.pallas import tpu as pltpu

F32 = jnp.float32
BF16 = jnp.bfloat16

D_MODEL = 1024
BATCH = 8
SEQ = 2048
DEPTH = 4
GRID_W = 64
CTX_LEN = 256
BLK = 128
WINDOW = 128
ROPE_BASE = 10000.0
EPS = 1e-6
LOG2E = 1.4426950408889634
HD_A = 64
N_HA = 6
N_KVA = 2
D_INNER = 384
HD_S = 64
N_HS = 6
D_STATE = 128
CONV_K = 5
CONV_DIM = 896
CHUNK = 128
N_HC = 4
HD_C = 32
D_C = 256
D_FF = 2816
N_EXP = 8
ALPHA = (2 * DEPTH) ** 0.25

LANES = 128
TILE = 256
ROWS = CTX_LEN + SEQ
N_TILES = ROWS // TILE
N_CHUNKS = ROWS // CHUNK
N_CTX_CHUNKS = CTX_LEN // CHUNK
FF_CHUNK = 256
N_FF_CHUNKS = D_FF // FF_CHUNK
D_IN_PAD = 2816
MOD_ROWS = 16
CTX_MOD_ROW = BATCH
VMEM_LIMIT = 56 * 1024 * 1024


def _cparams(sem, vmem=None):
    return pltpu.CompilerParams(dimension_semantics=sem, vmem_limit_bytes=vmem)


def _sigmoid(x):
    return 1.0 / (1.0 + jnp.exp(-x))


def _silu(x):
    return x * _sigmoid(x)


def _ln(x):
    mu = jnp.mean(x, axis=-1, keepdims=True)
    xc = x - mu
    var = jnp.mean(xc * xc, axis=-1, keepdims=True)
    return xc * lax.rsqrt(var + EPS)


def _dot(a, b, precision=None):
    return jnp.dot(a, b, preferred_element_type=F32, precision=precision)


def _split_bf16(x, parts):
    out = []
    for _ in range(parts - 1):
        hi = x.astype(BF16)
        out.append(hi)
        x = x - hi.astype(F32)
    out.append(x.astype(BF16))
    return out


def _dot_nt(a, b):
    return lax.dot_general(a, b, (((1,), (1,)), ((), ())), preferred_element_type=F32)


def _ada_kernel(c_ref, w_ref, b_ref, o_ref):
    act = _silu(c_ref[...])
    o_ref[...] = _dot(act.astype(BF16), w_ref[...].astype(BF16)) + b_ref[...]


def _ada_all(c_all, w_ada, b_ada):
    tn = 1536
    return pl.pallas_call(
        _ada_kernel,
        grid=(DEPTH, 6 * D_MODEL // tn),
        in_specs=[
            pl.BlockSpec((MOD_ROWS, D_MODEL), lambda l, j: (0, 0)),
            pl.BlockSpec((None, D_MODEL, tn), lambda l, j: (l, 0, j)),
            pl.BlockSpec((None, 1, tn), lambda l, j: (l, 0, j)),
        ],
        out_specs=pl.BlockSpec((None, MOD_ROWS, tn), lambda l, j: (l, 0, j)),
        out_shape=jax.ShapeDtypeStruct((DEPTH, MOD_ROWS, 6 * D_MODEL), F32),
        compiler_params=_cparams(("parallel", "parallel")),
        name="ada_ln",
    )(c_all, w_ada, b_ada.reshape(DEPTH, 1, 6 * D_MODEL))


def _mod_spec(li, piece, j0):
    return pl.BlockSpec(
        (None, None, None, 1, D_MODEL),
        lambda b, j: (li, jnp.where(j + j0 == 0, CTX_MOD_ROW, b), piece, 0, 0))


def _tok_spec(width, j0=0):
    return pl.BlockSpec((None, TILE, width), lambda b, j: (b, j + j0, 0))


def _full_spec(shape):
    n = len(shape)
    return pl.BlockSpec(shape, lambda *_: (0,) * n)


def _rope(v, cos, sin_signed, half):
    lane = lax.broadcasted_iota(jnp.int32, (v.shape[0], LANES), 1)
    first = (lane % (2 * half)) < half
    outs = []
    for s in range(v.shape[1] // LANES):
        blk = v[:, s * LANES:(s + 1) * LANES]
        swapped = jnp.where(first, pltpu.roll(blk, LANES - half, 1), pltpu.roll(blk, half, 1))
        outs.append(blk * cos + swapped * sin_signed)
    return jnp.concatenate(outs, axis=1)


def _inproj_kernel(x_ref, sh_ref, sc_ref, w_ref, cosa_ref, sina_ref, cosc_ref, sinc_ref,
                   qa_ref, ka_ref, va_ref, z_ref, xbc_ref, dt_ref, qc_ref, kc_ref, vc_ref):
    h = _ln(x_ref[...]) * (1.0 + sc_ref[...]) + sh_ref[...]
    hb = h.astype(BF16)

    def proj(lo, hi):
        return _dot(hb, w_ref[:, lo:hi])

    cosa, sina = cosa_ref[...], sina_ref[...]
    cosc, sinc = cosc_ref[...], sinc_ref[...]
    qa_ref[...] = (_rope(proj(0, 384), cosa, sina, HD_A // 2) * (HD_A ** -0.5)).astype(BF16)
    ka_ref[...] = _rope(proj(384, 512), cosa, sina, HD_A // 2).astype(BF16)
    va_ref[...] = proj(512, 640).astype(BF16)
    z_ref[...] = proj(640, 1024)
    xbc_ref[...] = proj(1024, 1920)
    qc_ref[...] = (_rope(proj(1920, 2176), cosc, sinc, HD_C // 2) * (HD_C ** -0.5 * LOG2E)).astype(BF16)
    kc_ref[...] = _rope(proj(2176, 2432), cosc, sinc, HD_C // 2).astype(BF16)
    vc = proj(2432, 2688)
    lane = lax.broadcasted_iota(jnp.int32, vc.shape, 1)
    for h in range(N_HC):
        in_head = jnp.where(lane >= h * 2 * HD_C, lane - h * 2 * HD_C, 2 * HD_C) < 2 * HD_C
        vc_ref[h] = jnp.where(in_head, vc, 1.0).astype(BF16)
    dt_ref[...] = proj(2688, 2816)


def _inproj(li, x_all, mod5, w_re, ropes):
    widths = (384, 128, 128, 384, CONV_DIM, LANES, D_C, D_C, D_C)
    dtypes = (BF16, BF16, BF16, F32, F32, F32, BF16, BF16, BF16)
    rope_spec = pl.BlockSpec((TILE, LANES), lambda b, j: (j, 0))
    return pl.pallas_call(
        _inproj_kernel,
        grid=(BATCH, N_TILES),
        in_specs=[_tok_spec(D_MODEL), _mod_spec(li, 0, 0), _mod_spec(li, 1, 0),
                  _full_spec((D_MODEL, D_IN_PAD)), rope_spec, rope_spec, rope_spec, rope_spec],
        out_specs=[_tok_spec(w) for w in widths[:-1]]
        + [pl.BlockSpec((None, N_HC, TILE, D_C), lambda b, j: (b, 0, j, 0))],
        out_shape=[jax.ShapeDtypeStruct((BATCH, ROWS, w), d) for w, d in zip(widths[:-1], dtypes)]
        + [jax.ShapeDtypeStruct((BATCH, N_HC, ROWS, D_C), BF16)],
        compiler_params=_cparams(("parallel", "parallel"), VMEM_LIMIT),
        name=f"inproj_l{li}",
    )(x_all, mod5, mod5, w_re, *ropes)


def _attn_a_kernel(*refs, local):
    if local:
        (sink_ref, q_ref, kp_ref, kc_ref, kn_ref, kx_ref,
         vp_ref, vc_ref, vn_ref, vx_ref, o_ref) = refs
    else:
        sink_ref, q_ref, kx_ref, vx_ref, o_ref = refs
    tq = q_ref.shape[0]
    q = q_ref[...].astype(F32)
    lane_lo = lax.broadcasted_iota(jnp.int32, (tq, LANES), 1) < HD_A
    parts = []
    for s in range(3):
        blk = q[:, s * LANES:(s + 1) * LANES]
        parts.append(jnp.where(lane_lo, blk, 0.0))
        parts.append(jnp.where(lane_lo, 0.0, blk))
    qst = jnp.concatenate(parts, axis=0).astype(BF16)
    if local:
        k = jnp.concatenate([kp_ref[...], kc_ref[...], kn_ref[...], kx_ref[...]], axis=0)
        v = jnp.concatenate([vp_ref[...], vc_ref[...], vn_ref[...], vx_ref[...]], axis=0)
        n = pl.program_id(1)
        r = lax.broadcasted_iota(jnp.int32, (tq, 3 * BLK), 0)
        c = lax.broadcasted_iota(jnp.int32, (tq, 3 * BLK), 1)
        prev_v = jnp.where(n > 0, c - r, -1)
        next_v = jnp.where(n < SEQ // BLK - 1, r - (c - 2 * BLK), -1)
        valid = jnp.where(c < BLK, prev_v, jnp.where(c < 2 * BLK, 0, next_v)) >= 0
    else:
        k, v = kx_ref[...], vx_ref[...]
    s_all = _dot_nt(qst, k)
    probs = []
    for rb in range(6):
        sink = sink_ref[rb]
        sh = s_all[rb * tq:(rb + 1) * tq]
        if local:
            s_loc = jnp.where(valid, sh[:, :3 * BLK], -jnp.inf)
            s_ctx = sh[:, 3 * BLK:]
            m = jnp.maximum(jnp.max(s_loc, axis=-1, keepdims=True),
                            jnp.max(s_ctx, axis=-1, keepdims=True))
            m = jnp.maximum(m, sink)
            p_loc = jnp.exp(s_loc - m)
            p_ctx = jnp.exp(s_ctx - m)
            den = (jnp.sum(p_loc, axis=-1, keepdims=True) + jnp.sum(p_ctx, axis=-1, keepdims=True)
                   + jnp.exp(sink - m))
            p = jnp.concatenate([p_loc, p_ctx], axis=1) / den
        else:
            m = jnp.maximum(jnp.max(sh, axis=-1, keepdims=True), sink)
            p_ctx = jnp.exp(sh - m)
            den = jnp.sum(p_ctx, axis=-1, keepdims=True) + jnp.exp(sink - m)
            p = p_ctx / den
        probs.append(p.astype(BF16))
    o_all = _dot(jnp.concatenate(probs, axis=0), v)
    outs = []
    for s in range(3):
        o0 = o_all[(2 * s) * tq:(2 * s + 1) * tq]
        o1 = o_all[(2 * s + 1) * tq:(2 * s + 2) * tq]
        outs.append(jnp.where(lane_lo, o0, o1))
    o_ref[...] = jnp.concatenate(outs, axis=1).astype(BF16)


def _attn_a(li, qa, ka, va, sink_p, with_ctx):
    nb = SEQ // BLK
    off = CTX_LEN // BLK
    smem = pl.BlockSpec(memory_space=pltpu.SMEM)

    def kv(shift):
        return pl.BlockSpec(
            (None, BLK, LANES), lambda b, n: (b, off + jnp.clip(n + shift, 0, nb - 1), 0))

    ctx_kv = pl.BlockSpec((None, CTX_LEN, LANES), lambda b, n: (b, 0, 0))
    q_spec = pl.BlockSpec((None, BLK, 384), lambda b, n: (b, off + n, 0))
    y_lat = pl.pallas_call(
        functools.partial(_attn_a_kernel, local=True),
        grid=(BATCH, nb),
        in_specs=[smem, q_spec, kv(-1), kv(0), kv(1), ctx_kv, kv(-1), kv(0), kv(1), ctx_kv],
        out_specs=pl.BlockSpec((None, BLK, 384), lambda b, n: (b, n, 0)),
        out_shape=jax.ShapeDtypeStruct((BATCH, SEQ, 384), BF16),
        compiler_params=_cparams(("parallel", "parallel")),
        name=f"attn_a_l{li}",
    )(sink_p, qa, ka, ka, ka, ka, va, va, va, va)
    if not with_ctx:
        return y_lat, None
    ctx_spec384 = pl.BlockSpec((None, CTX_LEN, 384), lambda b: (b, 0, 0))
    ctx_spec128 = pl.BlockSpec((None, CTX_LEN, LANES), lambda b: (b, 0, 0))
    y_ctx = pl.pallas_call(
        functools.partial(_attn_a_kernel, local=False),
        grid=(BATCH,),
        in_specs=[smem, ctx_spec384, ctx_spec128, ctx_spec128],
        out_specs=ctx_spec384,
        out_shape=jax.ShapeDtypeStruct((BATCH, CTX_LEN, 384), BF16),
        compiler_params=_cparams(("parallel",)),
        name=f"attn_a_ctx_l{li}",
    )(sink_p, qa, ka, va)
    return y_lat, y_ctx


def _attn_c_kernel(lq_ref, lk_ref, nw_ref, *refs, lam_init):
    q_refs, (k_ref, v_ref, o_ref) = refs[:-3], refs[-3:]
    tq = o_ref.shape[0]
    e = jnp.exp(jnp.sum(lq_ref[...] * lk_ref[...], axis=-1, keepdims=True))
    lam = e[0:1] - e[1:2] + lam_init
    q = jnp.concatenate([r[...] for r in q_refs], axis=0).astype(F32)
    k = k_ref[...]
    lane = lax.broadcasted_iota(jnp.int32, (tq, D_C), 1)
    lane2 = lax.broadcasted_iota(jnp.int32, (2 * tq, D_C), 1)
    acc = jnp.zeros((tq, D_C), F32)
    for h in range(N_HC):
        base = h * 2 * HD_C
        in0 = jnp.where(lane >= base, lane - base, 2 * HD_C) < HD_C
        in1 = jnp.where(lane >= base + HD_C, lane - base - HD_C, HD_C) < HD_C
        qst = jnp.concatenate([jnp.where(in0, q, 0.0), jnp.where(in1, q, 0.0)], axis=0).astype(BF16)
        s = _dot_nt(qst, k)
        p = jnp.exp2(s - jnp.max(s, axis=-1, keepdims=True)).astype(BF16)
        pv = _dot(p, v_ref[h])
        den = jnp.sum(jnp.where(lane2 == (base + 2 * HD_C) % D_C, pv, 0.0), axis=-1, keepdims=True)
        pv = pv * (1.0 / den)
        o = pv[:tq] - lam * pv[tq:]
        inh = jnp.where(lane >= base, lane - base, 2 * HD_C) < 2 * HD_C
        oh = jnp.where(inh, o, 0.0)
        ms = jnp.sum(oh * oh, axis=-1, keepdims=True) * (1.0 / (2 * HD_C))
        acc = acc + oh * lax.rsqrt(ms + EPS)
    o_ref[...] = (acc * nw_ref[...] * (1.0 - lam_init)).astype(BF16)


def _attn_c(li, qc, kc, vc, lam_q, lam_k, nw_row, lam_init, with_ctx):
    parts = 2
    tq = parts * TILE
    off = CTX_LEN // TILE
    q_specs = [pl.BlockSpec((None, TILE, D_C), lambda b, n, i=i: (b, off + parts * n + i, 0))
               for i in range(parts)]
    small = [_full_spec((2, HD_C)), _full_spec((2, HD_C)), _full_spec((1, D_C))]
    kern = functools.partial(_attn_c_kernel, lam_init=lam_init)
    all_kv = pl.BlockSpec((None, ROWS, D_C), lambda b, n: (b, 0, 0))
    all_v = pl.BlockSpec((None, N_HC, ROWS, D_C), lambda b, n: (b, 0, 0, 0))
    y_lat = pl.pallas_call(
        kern,
        grid=(BATCH, SEQ // tq),
        in_specs=small + q_specs + [all_kv, all_v],
        out_specs=pl.BlockSpec((None, tq, D_C), lambda b, n: (b, n, 0)),
        out_shape=jax.ShapeDtypeStruct((BATCH, SEQ, D_C), BF16),
        compiler_params=_cparams(("parallel", "parallel"), VMEM_LIMIT),
        name=f"attn_c_l{li}",
    )(lam_q, lam_k, nw_row, *([qc] * parts), kc, vc)
    if not with_ctx:
        return y_lat, None
    ctx_spec = pl.BlockSpec((None, CTX_LEN, D_C), lambda b: (b, 0, 0))
    small1 = [_full_spec((2, HD_C)), _full_spec((2, HD_C)), _full_spec((1, D_C))]
    y_ctx = pl.pallas_call(
        kern,
        grid=(BATCH,),
        in_specs=small1 + [ctx_spec, ctx_spec,
                           pl.BlockSpec((None, N_HC, CTX_LEN, D_C), lambda b: (b, 0, 0, 0))],
        out_specs=ctx_spec,
        out_shape=jax.ShapeDtypeStruct((BATCH, CTX_LEN, D_C), BF16),
        compiler_params=_cparams(("parallel",)),
        name=f"attn_c_ctx_l{li}",
    )(lam_q, lam_k, nw_row, qc, kc, vc)
    return y_lat, y_ctx


def _ssd_kernel(xbc_ref, z_ref, dt_ref, cw_ref, cb_ref, dtb_ref, alog_ref, dsk_ref, nw_ref,
                ef_ref, eb_ref, y_ref, u_scr, yacc_scr, st_scr):
    cw = cw_ref[...]
    cb = cb_ref[...]
    dsk = dsk_ref[...]
    pad = 8
    for c in range(N_CHUNKS):
        r0 = CHUNK * c
        zeros = jnp.zeros((pad, CONV_DIM), F32)
        if c in (0, N_CTX_CHUNKS):
            blk = jnp.concatenate([zeros, xbc_ref[r0:r0 + CHUNK + pad, :]], axis=0)
        elif c in (N_CTX_CHUNKS - 1, N_CHUNKS - 1):
            blk = jnp.concatenate([xbc_ref[r0 - pad:r0 + CHUNK, :], zeros], axis=0)
        else:
            blk = xbc_ref[r0 - pad:r0 + CHUNK + pad, :]
        acc = None
        for j in range(CONV_K):
            sh = (CONV_K // 2 - j) % (CHUNK + 2 * pad)
            rolled = blk if sh == 0 else pltpu.roll(blk, sh, 0)
            term = rolled[pad:pad + CHUNK, :] * cw[j:j + 1, :]
            acc = term if acc is None else acc + term
        u = _silu(acc + cb)
        u_scr[r0:r0 + CHUNK, :] = u
        yacc_scr[r0:r0 + CHUNK, :] = u[:, :D_INNER] * dsk

    st_scr[...] = jnp.zeros(st_scr.shape, F32)
    ri = lax.broadcasted_iota(jnp.int32, (CHUNK, CHUNK), 0)
    ci = lax.broadcasted_iota(jnp.int32, (CHUNK, CHUNK), 1)
    lower = ri >= ci
    upper = ri <= ci
    tri = (jnp.where(lower, 1.0, 0.0).astype(BF16), jnp.where(upper, 1.0, 0.0).astype(BF16))
    a_row = -jnp.exp(alog_ref[...])
    dtb = dtb_ref[...]
    lane_lo = ci < HD_S

    def chunk_dir(c, d):
        r0 = pl.multiple_of(c * CHUNK, CHUNK)
        u = u_scr[pl.ds(r0, CHUNK), :]
        xs = u[:, :D_INNER]
        bm = u[:, D_INNER:D_INNER + 2 * D_STATE]
        cm = u[:, D_INNER + 2 * D_STATE:]
        vv = dt_ref[pl.ds(r0, CHUNK), :] + dtb
        dt = jnp.maximum(vv, 0.0) + jnp.log(1.0 + jnp.exp(-jnp.abs(vv)))
        a = dt * a_row
        cum = sum(_dot(tri[d], piece) for piece in _split_bf16(a, 3))
        cum_t = cum.T
        tot = cum[CHUNK - 1:CHUNK, :] if d == 0 else cum[0:1, :]
        expand = ef_ref[...] if d == 0 else eb_ref[...]
        per_head = jnp.concatenate([dt, jnp.exp(cum), jnp.exp(tot - cum)], axis=0)
        per_lane = sum(_dot(piece, expand) for piece in _split_bf16(per_head, 2))
        dt_x = per_lane[:CHUNK]
        dec_out_x = per_lane[CHUNK:2 * CHUNK]
        dec_st_x = per_lane[2 * CHUNK:]
        etot_x = dec_out_x[CHUNK - 1:CHUNK, :] if d == 0 else dec_out_x[0:1, :]
        xdt = xs * dt_x
        xdt_b = xdt.astype(BF16)
        xst_b = (xdt * dec_st_x).astype(BF16)
        cb_ = [cm[:, g * D_STATE:(g + 1) * D_STATE].astype(BF16) for g in range(2)]
        bb_ = [bm[:, g * D_STATE:(g + 1) * D_STATE].astype(BF16) for g in range(2)]
        bt_ = [bm[:, g * D_STATE:(g + 1) * D_STATE].T.astype(BF16) for g in range(2)]
        cbm = [_dot_nt(cb_[g], bb_[g]) for g in range(2)]
        mask = lower if d == 0 else upper
        for j in range(3):
            sl = slice(j * LANES, (j + 1) * LANES)
            parts = []
            for h in (2 * j, 2 * j + 1):
                hl = h + N_HS * d
                seg = jnp.where(mask, cum[:, hl:hl + 1] - cum_t[hl:hl + 1, :], -jnp.inf)
                sc = (cbm[h // 3] * jnp.exp(seg)).astype(BF16)
                parts.append(_dot(sc, xdt_b[:, sl]))
            y_diag = jnp.where(lane_lo, parts[0], parts[1])
            st = st_scr[d, j]
            st_b = st.astype(BF16)
            g0, g1 = (2 * j) // 3, (2 * j + 1) // 3
            if g0 == g1:
                y_off = _dot(cb_[g0], st_b)
                upd = _dot(bt_[g0], xst_b[:, sl])
            else:
                y_off = jnp.where(lane_lo, _dot(cb_[g0], st_b), _dot(cb_[g1], st_b))
                upd = jnp.where(lane_lo, _dot(bt_[g0], xst_b[:, sl]), _dot(bt_[g1], xst_b[:, sl]))
            yacc_scr[pl.ds(r0, CHUNK), sl] += y_diag + y_off * dec_out_x[:, sl]
            st_scr[d, j] = etot_x[:, sl] * st + upd

    def body(i, carry):
        chunk_dir(i, 0)
        chunk_dir(jnp.where(i < N_CTX_CHUNKS, N_CTX_CHUNKS - 1 - i, N_CHUNKS + N_CTX_CHUNKS - 1 - i), 1)
        return carry

    lax.fori_loop(0, N_CHUNKS, body, 0)

    nw = nw_ref[...]
    for t in range(N_TILES):
        rows = slice(t * TILE, (t + 1) * TILE)
        y = yacc_scr[rows, :] * _silu(z_ref[rows, :])
        ms = jnp.mean(y * y, axis=-1, keepdims=True)
        y_ref[rows, :] = (y * lax.rsqrt(ms + EPS) * nw).astype(BF16)


def _ssd(li, xbc, z, dt, cw, cb, dtb, alog, dsk, nw, ef, eb):
    def seq(width):
        return pl.BlockSpec((None, ROWS, width), lambda b: (b, 0, 0))

    return pl.pallas_call(
        _ssd_kernel,
        grid=(BATCH,),
        in_specs=[seq(CONV_DIM), seq(D_INNER), seq(LANES),
                  _full_spec((8, CONV_DIM)), _full_spec((1, CONV_DIM)),
                  _full_spec((1, LANES)), _full_spec((1, LANES)),
                  _full_spec((1, D_INNER)), _full_spec((1, D_INNER)),
                  _full_spec((LANES, D_INNER)), _full_spec((LANES, D_INNER))],
        out_specs=seq(D_INNER),
        out_shape=jax.ShapeDtypeStruct((BATCH, ROWS, D_INNER), BF16),
        scratch_shapes=[pltpu.VMEM((ROWS, CONV_DIM), F32), pltpu.VMEM((ROWS, D_INNER), F32),
                        pltpu.VMEM((2, 3, D_STATE, LANES), F32)],
        compiler_params=_cparams(("parallel",), VMEM_LIMIT),
        name=f"ssd_l{li}",
    )(xbc, z, dt, cw, cb, dtb, alog, dsk, nw, ef, eb)


def _top2_gates(logits):
    lane = lax.broadcasted_iota(jnp.int32, logits.shape, 1).astype(F32)
    neg = -jnp.inf
    lg = jnp.where(lane < N_EXP, logits, neg)
    m1 = jnp.max(lg, axis=-1, keepdims=True)
    i1 = jnp.min(jnp.where(lg == m1, lane, float(LANES)), axis=-1, keepdims=True)
    lg2 = jnp.where(lane == i1, neg, lg)
    m2 = jnp.max(lg2, axis=-1, keepdims=True)
    i2 = jnp.min(jnp.where(lg2 == m2, lane, float(LANES)), axis=-1, keepdims=True)
    e2 = jnp.exp(m2 - m1)
    return lane, i1, i2, 1.0 / (1.0 + e2), e2 / (1.0 + e2)


R_E1, R_E2, R_W1, R_W2, R_RANK1, R_RANK2 = range(6)


def _route_record(logits, cnt_scr):
    lane, i1, i2, w1, w2 = _top2_gates(logits)
    rows = logits.shape[0]
    hit1, hit2 = lane == i1, lane == i2
    onehot = jnp.where(hit1, 1.0, 0.0) + jnp.where(hit2, 1.0, 0.0)
    ri = lax.broadcasted_iota(jnp.int32, (rows, rows), 0)
    ci = lax.broadcasted_iota(jnp.int32, (rows, rows), 1)
    earlier = jnp.where(ri > ci, 1.0, 0.0).astype(BF16)
    before = _dot(earlier, onehot.astype(BF16)) + cnt_scr[...]
    rank1 = jnp.sum(jnp.where(hit1, before, 0.0), axis=-1, keepdims=True)
    rank2 = jnp.sum(jnp.where(hit2, before, 0.0), axis=-1, keepdims=True)
    cnt_scr[...] += jnp.sum(onehot, axis=0, keepdims=True)
    rec = jnp.zeros(logits.shape, F32)
    for idx, val in ((R_E1, i1), (R_E2, i2), (R_W1, w1), (R_W2, w2), (R_RANK1, rank1), (R_RANK2, rank2)):
        rec = jnp.where(lane == float(idx), val, rec)
    return rec


def _outproj_kernel(*refs, moe, ctx_split):
    if ctx_split:
        ya_l, ya_c, yb_ref, yd_l, yd_c = refs[:5]
        refs = refs[5:]
    else:
        ya_l, yb_ref, yd_l = refs[:3]
        refs = refs[3:]
    x_ref, g1_ref, lg_ref, lb_ref, sh_ref, sc_ref, wo_ref = refs[:7]
    refs = refs[7:]
    if moe:
        wr_ref, br_ref, x1_ref, h2_ref, route_ref, counts_ref, cnt_scr = refs

        @pl.when((pl.program_id(0) == 0) & (pl.program_id(1) == 0))
        def _():
            cnt_scr[...] = jnp.zeros(cnt_scr.shape, F32)
    else:
        x1_ref, h2_ref = refs
    if ctx_split:
        is_ctx = pl.program_id(1) == 0
        ya = jnp.where(is_ctx, ya_c[...], ya_l[...])
        yd = jnp.where(is_ctx, yd_c[...], yd_l[...])
    else:
        ya, yd = ya_l[...], yd_l[...]
    mix = (_dot(ya, wo_ref[0:384, :]) + _dot(yb_ref[...], wo_ref[384:768, :])
           + _dot(yd, wo_ref[768:1024, :]))
    x1 = _ln(ALPHA * x_ref[...] + g1_ref[...] * mix) * lg_ref[...] + lb_ref[...]
    x1_ref[...] = x1
    h2 = _ln(x1) * (1.0 + sc_ref[...]) + sh_ref[...]
    h2_ref[...] = h2.astype(h2_ref.dtype)
    if moe:
        h_hi, h_mid = _split_bf16(h2, 2)
        both = _dot(h_hi, wr_ref[...])
        logits = (both[:, :LANES] + both[:, LANES:] + _dot(h_mid, wr_ref[:, :LANES])) + br_ref[...]
        route_ref[...] = _route_record(logits, cnt_scr)
        counts_ref[...] = cnt_scr[...]


def _outproj(li, ya_lat, ya_ctx, yb, yd_lat, yd_ctx, x_all, mod5, lg, lb, wo, moe, wr, br, j0):
    ctx_split = j0 == 0
    nt = N_TILES - j0
    off = CTX_LEN // TILE
    lat = lambda w: pl.BlockSpec((None, TILE, w), lambda b, j: (b, jnp.maximum(j + j0 - off, 0), 0))
    ctx = lambda w: pl.BlockSpec((None, TILE, w), lambda b, j: (b, 0, 0))
    row = _full_spec((1, D_MODEL))
    if ctx_split:
        ins = [ya_lat, ya_ctx, yb, yd_lat, yd_ctx]
        specs = [lat(384), ctx(384), _tok_spec(D_INNER, j0), lat(D_C), ctx(D_C)]
    else:
        ins = [ya_lat, yb, yd_lat]
        specs = [lat(384), _tok_spec(D_INNER, j0), lat(D_C)]
    ins += [x_all, mod5, lg, lb, mod5, mod5, wo]
    specs += [_tok_spec(D_MODEL, j0), _mod_spec(li, 2, j0), row, row,
              _mod_spec(li, 3, j0), _mod_spec(li, 4, j0), _full_spec((D_MODEL, D_MODEL))]
    outs = [jax.ShapeDtypeStruct((BATCH, nt * TILE, D_MODEL), F32),
            jax.ShapeDtypeStruct((BATCH, nt * TILE, D_MODEL), F32 if moe else BF16)]
    out_specs = [_tok_spec(D_MODEL), _tok_spec(D_MODEL)]
    scratch = []
    if moe:
        ins += [wr, br]
        specs += [_full_spec((D_MODEL, 2 * LANES)), _full_spec((1, LANES))]
        outs += [jax.ShapeDtypeStruct((BATCH, nt * TILE, LANES), F32),
                 jax.ShapeDtypeStruct((1, LANES), F32)]
        out_specs += [_tok_spec(LANES), _full_spec((1, LANES))]
        scratch = [pltpu.VMEM((1, LANES), F32)]
    sem = ("arbitrary", "arbitrary") if moe else ("parallel", "parallel")
    return pl.pallas_call(
        functools.partial(_outproj_kernel, moe=moe, ctx_split=ctx_split),
        grid=(BATCH, nt),
        in_specs=specs,
        out_specs=out_specs,
        out_shape=outs,
        scratch_shapes=scratch,
        compiler_params=_cparams(sem, VMEM_LIMIT),
        name=f"outproj_l{li}",
    )(*ins)


def _swiglu_acc(h2, wgu_ref, wd_ref):
    acc = jnp.zeros((h2.shape[0], D_MODEL), F32)
    for c in range(N_FF_CHUNKS):
        cols = slice(c * FF_CHUNK, (c + 1) * FF_CHUNK)
        up_cols = slice(D_FF + c * FF_CHUNK, D_FF + (c + 1) * FF_CHUNK)
        act = (_silu(_dot(h2, wgu_ref[:, cols])) * _dot(h2, wgu_ref[:, up_cols])).astype(BF16)
        acc = acc + _dot(act, wd_ref[cols, :])
    return acc


W_STAGE = 256


def _stage_weights(lead, wgu_hbm, wd_hbm, wgu_bf, wd_bf, stage_gu, stage_d, sems):
    def gu_src(c):
        return wgu_hbm.at[lead + (slice(None), pl.ds(c * W_STAGE, W_STAGE))]

    def d_src(c):
        return wd_hbm.at[lead + (pl.ds(c * W_STAGE, W_STAGE), slice(None))]

    n_gu, n_d = 2 * D_FF // W_STAGE, D_FF // W_STAGE
    copies = [pltpu.make_async_copy(gu_src(c), stage_gu.at[c % 2], sems.at[c % 2]) for c in range(n_gu)]
    copies += [pltpu.make_async_copy(d_src(c), stage_d.at[c % 2], sems.at[2 + c % 2]) for c in range(n_d)]
    copies[0].start()
    for i, copy in enumerate(copies):
        if i + 1 < len(copies):
            copies[i + 1].start()
        copy.wait()
        if i < n_gu:
            wgu_bf[:, i * W_STAGE:(i + 1) * W_STAGE] = stage_gu[i % 2].astype(BF16)
        else:
            c = i - n_gu
            wd_bf[c * W_STAGE:(c + 1) * W_STAGE, :] = stage_d[c % 2].astype(BF16)


_WEIGHT_SCRATCH = [pltpu.VMEM((D_MODEL, 2 * D_FF), BF16), pltpu.VMEM((D_FF, D_MODEL), BF16),
                   pltpu.VMEM((2, D_MODEL, W_STAGE), F32), pltpu.VMEM((2, W_STAGE, D_MODEL), F32),
                   pltpu.SemaphoreType.DMA((4,))]


def _ffn_dense_kernel(x1_ref, h2_ref, g2_ref, lg_ref, lb_ref, wgu_hbm, wd_hbm, o_ref,
                      wgu_bf, wd_bf, stage_gu, stage_d, sems, *, fj):
    @pl.when((pl.program_id(0) == 0) & (pl.program_id(1) == 0))
    def _():
        _stage_weights((fj,), wgu_hbm, wd_hbm, wgu_bf, wd_bf, stage_gu, stage_d, sems)

    y = _swiglu_acc(h2_ref[...], wgu_bf, wd_bf)
    o_ref[...] = _ln(ALPHA * x1_ref[...] + g2_ref[...] * y) * lg_ref[...] + lb_ref[...]


def _ffn_dense(li, x1, h2, mod5, lg, lb, wgu, wd):
    row = _full_spec((1, D_MODEL))
    return pl.pallas_call(
        functools.partial(_ffn_dense_kernel, fj=li // 2),
        grid=(BATCH, N_TILES),
        in_specs=[_tok_spec(D_MODEL), _tok_spec(D_MODEL), _mod_spec(li, 5, 0), row, row,
                  pl.BlockSpec(memory_space=pl.ANY), pl.BlockSpec(memory_space=pl.ANY)],
        out_specs=_tok_spec(D_MODEL),
        out_shape=jax.ShapeDtypeStruct((BATCH, ROWS, D_MODEL), F32),
        scratch_shapes=_WEIGHT_SCRATCH,
        compiler_params=_cparams(("arbitrary", "arbitrary"), VMEM_LIMIT),
        name=f"ffn_dense_l{li}",
    )(x1, h2, mod5, lg, lb, wgu, wd)


EXP_TILE = 256


def _slot_row_copy(src_ref, src_row, dst_ref, dst_row, sem):
    return pltpu.make_async_copy(src_ref.at[pl.ds(src_row, 1), :], dst_ref.at[pl.ds(dst_row, 1), :], sem)


def _wait_rows(src_ref, dst_ref, sem):
    pltpu.make_async_copy(src_ref, dst_ref, sem).wait()


def _dispatch_kernel(pos_ref, fill_start_ref, fill_rows_ref, h2_ref, xs_ref, zero_scr, sem, zsem):
    first = (pl.program_id(0) == 0) & (pl.program_id(1) == 0)
    sub = 8
    pad_chunks = [1 << s for s in reversed(range(sub.bit_length() - 1, EXP_TILE.bit_length() - 1))]

    def pad_copies(e):
        start, n = fill_start_ref[e], fill_rows_ref[e]
        head = (-start) & (sub - 1)
        for i in range(sub - 1):
            yield i < jnp.minimum(head, n), _slot_row_copy(zero_scr, i, xs_ref, start + i, zsem)
        body = jnp.maximum(n - head, 0)
        for rows in pad_chunks:
            off = pl.multiple_of(start + head + (body & ~(2 * rows - 1)), sub)
            dst = xs_ref.at[pl.ds(off, rows), :]
            yield (body & rows) != 0, pltpu.make_async_copy(zero_scr.at[pl.ds(0, rows), :], dst, zsem)

    @pl.when(first)
    def _():
        zero_scr[...] = jnp.zeros(zero_scr.shape, F32)
        for e in range(fill_rows_ref.shape[0]):
            for cond, copy in pad_copies(e):
                pl.when(cond)(copy.start)

    base = (pl.program_id(0) * pl.num_programs(1) + pl.program_id(1)) * (2 * TILE)

    for r in range(TILE):
        for k in range(2):
            _slot_row_copy(h2_ref, r, xs_ref, pos_ref[base + 2 * r + k], sem).start(priority=k)
    for k in range(2):
        _wait_rows(h2_ref, xs_ref.at[pl.ds(0, TILE), :], sem)

    @pl.when(first)
    def _():
        for e in range(fill_rows_ref.shape[0]):
            for cond, copy in pad_copies(e):
                pl.when(cond)(copy.wait)


def _grouped_kernel(tile_e_ref, n_used_ref, xs_ref, wgu_hbm, wd_hbm, ys_ref,
                    wgu_bf, wd_bf, stage_gu, stage_d, sems, *, fj):
    i = pl.program_id(0)
    live = i < n_used_ref[0]
    e = tile_e_ref[i]

    @pl.when(live & ((i == 0) | (e != tile_e_ref[jnp.maximum(i - 1, 0)])))
    def _():
        _stage_weights((fj, e), wgu_hbm, wd_hbm, wgu_bf, wd_bf, stage_gu, stage_d, sems)

    @pl.when(live)
    def _():
        ys_ref[...] = _swiglu_acc(xs_ref[...].astype(BF16), wgu_bf, wd_bf)

    @pl.when(jnp.logical_not(live))
    def _():
        ys_ref[...] = jnp.zeros(ys_ref.shape, F32)


def _combine_kernel(pos_ref, ys_ref, route_ref, x1_ref, g2_ref, lg_ref, lb_ref, o_ref, ybuf, sem):
    base = (pl.program_id(0) * pl.num_programs(1) + pl.program_id(1)) * (2 * TILE)

    for r in range(TILE):
        for k in range(2):
            _slot_row_copy(ys_ref, pos_ref[base + 2 * r + k], ybuf.at[k], r, sem).start(priority=k)
    for k in range(2):
        _wait_rows(ys_ref.at[pl.ds(0, TILE), :], ybuf.at[k], sem)
    route = route_ref[...]
    lane = lax.broadcasted_iota(jnp.int32, route.shape, 1)
    w1 = jnp.sum(jnp.where(lane == R_W1, route, 0.0), axis=-1, keepdims=True)
    w2 = jnp.sum(jnp.where(lane == R_W2, route, 0.0), axis=-1, keepdims=True)
    y = w1 * ybuf[0] + w2 * ybuf[1]
    o_ref[...] = _ln(ALPHA * x1_ref[...] + g2_ref[...] * y) * lg_ref[...] + lb_ref[...]


def _ffn_routed(li, x1, h2, route, counts, mod5, lg, lb, wgu, wd, j0):
    nt = N_TILES - j0
    n_tok = BATCH * nt * TILE
    n_tiles = 2 * n_tok // EXP_TILE + N_EXP
    n_slots = n_tiles * EXP_TILE
    cnt = counts[0, :N_EXP].astype(jnp.int32)
    padded = (cnt + EXP_TILE - 1) // EXP_TILE * EXP_TILE
    ends = jnp.cumsum(padded)
    offs = ends - padded
    n_used = ends[-1:] // EXP_TILE
    tile_start = jnp.minimum(jnp.arange(n_tiles), n_used - 1) * EXP_TILE
    tile_e = jnp.sum((tile_start[:, None] >= ends[None, :]).astype(jnp.int32), axis=1)
    rec = route.reshape(n_tok, LANES)
    experts = rec[:, R_E1:R_E2 + 1].astype(jnp.int32)
    ranks = rec[:, R_RANK1:R_RANK2 + 1].astype(jnp.int32)
    off_of = jnp.sum(jnp.where(experts[..., None] == jnp.arange(N_EXP), offs, 0), axis=-1)
    pos = (off_of + ranks).reshape(2 * n_tok)

    h2_flat = h2.reshape(n_tok, D_MODEL)
    half = EXP_TILE // 2
    tail_start = n_used * EXP_TILE + jnp.arange(2 * N_EXP) * half
    tail_rows = jnp.where(tail_start < n_slots, half, 0)
    tail_start = jnp.minimum(tail_start, n_slots - half)
    any_spec = pl.BlockSpec(memory_space=pl.ANY)
    xs = pl.pallas_call(
        _dispatch_kernel,
        grid_spec=pltpu.PrefetchScalarGridSpec(
            num_scalar_prefetch=3,
            grid=(BATCH, nt),
            in_specs=[pl.BlockSpec((TILE, D_MODEL), lambda b, j, *_: (b * nt + j, 0))],
            out_specs=any_spec,
            scratch_shapes=[pltpu.VMEM((EXP_TILE // 2, D_MODEL), F32),
                            pltpu.SemaphoreType.DMA(()), pltpu.SemaphoreType.DMA(())]),
        out_shape=jax.ShapeDtypeStruct((n_slots, D_MODEL), F32),
        compiler_params=_cparams(("arbitrary", "arbitrary")),
        name=f"moe_dispatch_l{li}",
    )(pos, jnp.concatenate([offs + cnt, tail_start]), jnp.concatenate([padded - cnt, tail_rows]), h2_flat)

    used = lambda i, te, nu: jnp.minimum(i, nu[0] - 1)
    ys = pl.pallas_call(
        functools.partial(_grouped_kernel, fj=li // 2),
        grid_spec=pltpu.PrefetchScalarGridSpec(
            num_scalar_prefetch=2,
            grid=(n_tiles,),
            in_specs=[
                pl.BlockSpec((EXP_TILE, D_MODEL), lambda i, te, nu: (used(i, te, nu), 0)),
                pl.BlockSpec(memory_space=pl.ANY), pl.BlockSpec(memory_space=pl.ANY)],
            out_specs=pl.BlockSpec((EXP_TILE, D_MODEL), lambda i, te, nu: (i, 0)),
            scratch_shapes=_WEIGHT_SCRATCH),
        out_shape=jax.ShapeDtypeStruct((n_slots, D_MODEL), F32),
        compiler_params=_cparams(("arbitrary",), VMEM_LIMIT),
        name=f"moe_grouped_l{li}",
    )(tile_e, n_used, xs, wgu, wd)

    row = pl.BlockSpec((1, D_MODEL), lambda b, j, pos: (0, 0))
    tok = lambda w: pl.BlockSpec((None, TILE, w), lambda b, j, pos: (b, j, 0))
    mod = pl.BlockSpec((None, None, None, 1, D_MODEL),
                       lambda b, j, pos: (li, jnp.where(j + j0 == 0, CTX_MOD_ROW, b), 5, 0, 0))
    return pl.pallas_call(
        _combine_kernel,
        grid_spec=pltpu.PrefetchScalarGridSpec(
            num_scalar_prefetch=1,
            grid=(BATCH, nt),
            in_specs=[any_spec, tok(LANES), tok(D_MODEL), mod, row, row],
            out_specs=tok(D_MODEL),
            scratch_shapes=[pltpu.VMEM((2, TILE, D_MODEL), F32), pltpu.SemaphoreType.DMA(())]),
        out_shape=jax.ShapeDtypeStruct((BATCH, nt * TILE, D_MODEL), F32),
        compiler_params=_cparams(("arbitrary", "arbitrary"), VMEM_LIMIT),
        name=f"moe_combine_l{li}",
    )(pos, ys, route, x1, mod5, lg, lb)


def _rope_tables(dim):
    rows_n = SEQ // GRID_W
    t_row = jnp.repeat(jnp.arange(rows_n, dtype=F32), GRID_W)
    t_col = jnp.tile(jnp.arange(GRID_W, dtype=F32), rows_n)
    quarter = dim // 4
    inv = ROPE_BASE ** (-jnp.arange(quarter, dtype=F32) / quarter)
    ang = jnp.concatenate([t_row[:, None] * inv, t_col[:, None] * inv], axis=-1)
    cos, sin = jnp.cos(ang), jnp.sin(ang)
    reps = LANES // dim
    cos_f = jnp.tile(jnp.concatenate([cos, cos], axis=-1), (1, reps))
    sin_s = jnp.tile(jnp.concatenate([-sin, sin], axis=-1), (1, reps))
    cos_f = jnp.concatenate([jnp.ones((CTX_LEN, LANES), F32), cos_f], axis=0)
    sin_s = jnp.concatenate([jnp.zeros((CTX_LEN, LANES), F32), sin_s], axis=0)
    return cos_f, sin_s


def _head_expand(offset):
    rows = jnp.arange(LANES)[:, None]
    cols = jnp.arange(D_INNER)[None, :]
    return (rows == cols // HD_S + offset).astype(BF16)


def kernel(x, c, ctx, c_ctx, w_ada, b_ada, w_in, attn_sink, conv_w, conv_b, dt_bias, a_log, d_skip,
           ssm_norm_w, lam_q, lam_k, diff_norm_w, w_out, ln1_g, ln1_b, ln2_g, ln2_b, ffn_w_gu,
           ffn_w_down, router_w, router_b, exp_w_gu, exp_w_down):
    x_all = jnp.concatenate([ctx, x], axis=1)
    c_all = jnp.concatenate(
        [c, c_ctx[None, :], jnp.zeros((MOD_ROWS - BATCH - 1, D_MODEL), F32)], axis=0)
    mod5 = _ada_all(c_all, w_ada, b_ada).reshape(DEPTH, MOD_ROWS, 6, 1, D_MODEL)
    ropes = _rope_tables(HD_A) + _rope_tables(HD_C)
    ef, eb = _head_expand(0), _head_expand(N_HS)

    for li in range(DEPTH):
        ctx_out = li < DEPTH - 1
        j0 = 0 if ctx_out else CTX_LEN // TILE
        lam_init = 0.8 - 0.6 * math.exp(-0.3 * li)
        w = w_in[li]
        wq = w[:, :384].reshape(D_MODEL, N_KVA, 3, HD_A).transpose(0, 2, 1, 3).reshape(D_MODEL, 384)
        w_re = jnp.concatenate(
            [wq, w[:, 384:1920], w[:, 1932:2700], w[:, 1920:1932],
             jnp.zeros((D_MODEL, D_IN_PAD - 2700), F32)], axis=1).astype(BF16)
        qa, ka, va, z, xbc, dt, qc, kc, vc = _inproj(li, x_all, mod5, w_re, ropes)

        sink_p = attn_sink[li].reshape(N_KVA, 3).T.reshape(N_HA)
        ya_lat, ya_ctx = _attn_a(li, qa, ka, va, sink_p, ctx_out)
        nw_c = jnp.tile(diff_norm_w[li], N_HC)[None, :]
        yd_lat, yd_ctx = _attn_c(li, qc, kc, vc, lam_q[li], lam_k[li], nw_c, lam_init, ctx_out)
        cw = jnp.pad(conv_w[li], ((0, 8 - CONV_K), (0, 0)))
        pad12 = lambda a: jnp.pad(a.reshape(1, 2 * N_HS), ((0, 0), (0, LANES - 2 * N_HS)))
        yb = _ssd(li, xbc, z, dt, cw, conv_b[li][None, :], pad12(dt_bias[li]), pad12(a_log[li]),
                  jnp.repeat(d_skip[li], HD_S)[None, :], ssm_norm_w[li][None, :], ef, eb)

        wo = w_out[li]
        wo_a = wo[:384].reshape(N_KVA, 3, HD_A, D_MODEL).transpose(1, 0, 2, 3).reshape(384, D_MODEL)
        wo_re = jnp.concatenate([wo_a, wo[384:]], axis=0).astype(BF16)
        moe = li % 2 == 1
        fj = li // 2
        if moe:
            wr_f = jnp.pad(router_w[fj], ((0, 0), (0, LANES - N_EXP)))
            wr_hi = lax.bitcast_convert_type(
                lax.bitcast_convert_type(wr_f, jnp.uint32) & jnp.uint32(0xFFFF0000), F32)
            wr = jnp.concatenate([wr_hi.astype(BF16), (wr_f - wr_hi).astype(BF16)], axis=1)
            br = jnp.pad(router_b[fj], (0, LANES - N_EXP))[None, :]
        else:
            wr = br = None
        res = _outproj(li, ya_lat, ya_ctx, yb, yd_lat, yd_ctx, x_all, mod5, ln1_g[li][None, :],
                       ln1_b[li][None, :], wo_re, moe, wr, br, j0)
        lg2, lb2 = ln2_g[li][None, :], ln2_b[li][None, :]
        if moe:
            x1, h2, route, counts = res
            wgu, wd = exp_w_gu, exp_w_down
            x_all = _ffn_routed(li, x1, h2, route, counts, mod5, lg2, lb2, wgu, wd, j0)
        else:
            x1, h2 = res
            wgu, wd = ffn_w_gu, ffn_w_down
            x_all = _ffn_dense(li, x1, h2, mod5, lg2, lb2, wgu, wd)
    return x_all
```

```python
import functools
import math

import jax
import jax.numpy as jnp
from jax import lax
from jax.experimental import pallas as pl
from jax.experimental.pallas import tpu as pltpu

F32 = jnp.float32
BF16 = jnp.bfloat16

D_MODEL = 1024
BATCH = 8
SEQ = 2048
DEPTH = 4
GRID_W = 64
CTX_LEN = 256
BLK = 128
WINDOW = 128
ROPE_BASE = 10000.0
EPS = 1e-6
LOG2E = 1.4426950408889634
HD_A = 64
N_HA = 6
N_KVA = 2
D_INNER = 384
HD_S = 64
N_HS = 6
D_STATE = 128
CONV_K = 5
CONV_DIM = 896
CHUNK = 128
N_HC = 4
HD_C = 32
D_C = 256
D_FF = 2816
N_EXP = 8
ALPHA = (2 * DEPTH) ** 0.25

LANES = 128
TILE = 256
ROWS = CTX_LEN + SEQ
N_TILES = ROWS // TILE
N_CHUNKS = ROWS // CHUNK
N_CTX_CHUNKS = CTX_LEN // CHUNK
FF_CHUNK = 256
N_FF_CHUNKS = D_FF // FF_CHUNK
D_IN_PAD = 2816
MOD_ROWS = 16
CTX_MOD_ROW = BATCH
VMEM_LIMIT = 56 * 1024 * 1024


def _cparams(sem, vmem=None):
    return pltpu.CompilerParams(dimension_semantics=sem, vmem_limit_bytes=vmem)


def _sigmoid(x):
    return 1.0 / (1.0 + jnp.exp(-x))


def _silu(x):
    return x * _sigmoid(x)


def _ln(x):
    mu = jnp.mean(x, axis=-1, keepdims=True)
    xc = x - mu
    var = jnp.mean(xc * xc, axis=-1, keepdims=True)
    return xc * lax.rsqrt(var + EPS)


def _dot(a, b, precision=None):
    return jnp.dot(a, b, preferred_element_type=F32, precision=precision)


def _split_bf16(x, parts):
    out = []
    for _ in range(parts - 1):
        hi = x.astype(BF16)
        out.append(hi)
        x = x - hi.astype(F32)
    out.append(x.astype(BF16))
    return out


def _dot_nt(a, b):
    return lax.dot_general(a, b, (((1,), (1,)), ((), ())), preferred_element_type=F32)


def _ada_kernel(c_ref, w_ref, b_ref, o_ref):
    act = _silu(c_ref[...])
    o_ref[...] = _dot(act.astype(BF16), w_ref[...].astype(BF16)) + b_ref[...]


def _ada_all(c_all, w_ada, b_ada):
    tn = 1536
    return pl.pallas_call(
        _ada_kernel,
        grid=(DEPTH, 6 * D_MODEL // tn),
        in_specs=[
            pl.BlockSpec((MOD_ROWS, D_MODEL), lambda l, j: (0, 0)),
            pl.BlockSpec((None, D_MODEL, tn), lambda l, j: (l, 0, j)),
            pl.BlockSpec((None, 1, tn), lambda l, j: (l, 0, j)),
        ],
        out_specs=pl.BlockSpec((None, MOD_ROWS, tn), lambda l, j: (l, 0, j)),
        out_shape=jax.ShapeDtypeStruct((DEPTH, MOD_ROWS, 6 * D_MODEL), F32),
        compiler_params=_cparams(("parallel", "parallel")),
        name="ada_ln",
    )(c_all, w_ada, b_ada.reshape(DEPTH, 1, 6 * D_MODEL))


def _mod_spec(li, piece, j0):
    return pl.BlockSpec(
        (None, None, None, 1, D_MODEL),
        lambda b, j: (li, jnp.where(j + j0 == 0, CTX_MOD_ROW, b), piece, 0, 0))


def _tok_spec(width, j0=0):
    return pl.BlockSpec((None, TILE, width), lambda b, j: (b, j + j0, 0))


def _full_spec(shape):
    n = len(shape)
    return pl.BlockSpec(shape, lambda *_: (0,) * n)


def _rope(v, cos, sin_signed, half):
    lane = lax.broadcasted_iota(jnp.int32, (v.shape[0], LANES), 1)
    first = (lane % (2 * half)) < half
    outs = []
    for s in range(v.shape[1] // LANES):
        blk = v[:, s * LANES:(s + 1) * LANES]
        swapped = jnp.where(first, pltpu.roll(blk, LANES - half, 1), pltpu.roll(blk, half, 1))
        outs.append(blk * cos + swapped * sin_signed)
    return jnp.concatenate(outs, axis=1)


def _inproj_kernel(x_ref, sh_ref, sc_ref, w_ref, cosa_ref, sina_ref, cosc_ref, sinc_ref,
                   qa_ref, ka_ref, va_ref, z_ref, xbc_ref, dt_ref, qc_ref, kc_ref, vc_ref):
    h = _ln(x_ref[...]) * (1.0 + sc_ref[...]) + sh_ref[...]
    hb = h.astype(BF16)

    def proj(lo, hi):
        return _dot(hb, w_ref[:, lo:hi])

    cosa, sina = cosa_ref[...], sina_ref[...]
    cosc, sinc = cosc_ref[...], sinc_ref[...]
    qk_a = proj(0, 512)
    qa_ref[...] = (_rope(qk_a[:, :384], cosa, sina, HD_A // 2) * (HD_A ** -0.5 * LOG2E)).astype(BF16)
    ka_ref[...] = _rope(qk_a[:, 384:], cosa, sina, HD_A // 2).astype(BF16)
    vz = proj(512, 1024)
    va_ref[...] = vz[:, :LANES].astype(BF16)
    z_ref[...] = vz[:, LANES:]
    xbc_dt = proj(1024, 2048)
    xbc_ref[...] = xbc_dt[:, :CONV_DIM]
    dt_ref[...] = xbc_dt[:, CONV_DIM:]
    qc_ref[...] = (_rope(proj(2048, 2304), cosc, sinc, HD_C // 2) * (HD_C ** -0.5 * LOG2E)).astype(BF16)
    kc_ref[...] = _rope(proj(2304, 2560), cosc, sinc, HD_C // 2).astype(BF16)
    vc = proj(2560, 2816)
    lane = lax.broadcasted_iota(jnp.int32, vc.shape, 1)
    for h in range(N_HC):
        in_head = jnp.where(lane >= h * 2 * HD_C, lane - h * 2 * HD_C, 2 * HD_C) < 2 * HD_C
        vc_ref[h] = jnp.where(in_head, vc, 1.0).astype(BF16)


def _inproj(li, x_all, mod5, w_re, ropes):
    widths = (384, 128, 128, 384, CONV_DIM, LANES, D_C, D_C, D_C)
    dtypes = (BF16, BF16, BF16, F32, F32, F32, BF16, BF16, BF16)
    rope_spec = pl.BlockSpec((TILE, LANES), lambda b, j: (j, 0))
    return pl.pallas_call(
        _inproj_kernel,
        grid=(BATCH, N_TILES),
        in_specs=[_tok_spec(D_MODEL), _mod_spec(li, 0, 0), _mod_spec(li, 1, 0),
                  _full_spec((D_MODEL, D_IN_PAD)), rope_spec, rope_spec, rope_spec, rope_spec],
        out_specs=[_tok_spec(w) for w in widths[:-1]]
        + [pl.BlockSpec((None, N_HC, TILE, D_C), lambda b, j: (b, 0, j, 0))],
        out_shape=[jax.ShapeDtypeStruct((BATCH, ROWS, w), d) for w, d in zip(widths[:-1], dtypes)]
        + [jax.ShapeDtypeStruct((BATCH, N_HC, ROWS, D_C), BF16)],
        compiler_params=_cparams(("parallel", "parallel"), VMEM_LIMIT),
        name=f"inproj_l{li}",
    )(x_all, mod5, mod5, w_re, *ropes)


def _attn_a_block(sink_ref, bias, q, k, v):
    tq = q.shape[0]
    q = q.astype(F32)
    lane_lo = lax.broadcasted_iota(jnp.int32, (tq, LANES), 1) < HD_A
    slabs = [q[:, s * LANES:(s + 1) * LANES] for s in range(3)]
    qst = jnp.concatenate([jnp.where(lane_lo, blk, 0.0) for blk in slabs]
                          + [jnp.where(lane_lo, 0.0, blk) for blk in slabs], axis=0).astype(BF16)
    s_all = _dot_nt(qst, k)
    probs, sink_terms = [], []
    for rb in range(N_HA):
        sink = sink_ref[rb] * LOG2E
        sh = s_all[rb * tq:(rb + 1) * tq]
        if bias is not None:
            sh = jnp.concatenate([sh[:, :3 * BLK] + bias, sh[:, 3 * BLK:]], axis=1)
        m = jnp.maximum(jnp.max(sh, axis=-1, keepdims=True), sink)
        probs.append(jnp.exp2(sh - m).astype(BF16))
        sink_terms.append(jnp.exp2(sink - m))
    vf = v.astype(F32)
    key_lo = lax.broadcasted_iota(jnp.int32, vf.shape, 1) < HD_A
    v_heads = (jnp.where(key_lo, vf, 1.0).astype(BF16), jnp.where(key_lo, 1.0, vf).astype(BF16))
    outs = []
    for g in range(N_KVA):
        o = _dot(jnp.concatenate(probs[3 * g:3 * g + 3], axis=0), v_heads[g])
        lane3 = lax.broadcasted_iota(jnp.int32, o.shape, 1)
        den = jnp.sum(jnp.where(lane3 == (1 - g) * HD_A, o, 0.0), axis=-1, keepdims=True)
        den = den + jnp.concatenate(sink_terms[3 * g:3 * g + 3], axis=0)
        outs.append(o * (1.0 / den))
    return jnp.concatenate(
        [jnp.where(lane_lo, outs[0][s * tq:(s + 1) * tq], outs[1][s * tq:(s + 1) * tq]) for s in range(3)],
        axis=1).astype(BF16)


def _attn_a_kernel(*refs, local):
    if not local:
        sink_ref, q_ref, kx_ref, vx_ref, o_ref = refs
        o_ref[...] = _attn_a_block(sink_ref, None, q_ref[...], kx_ref[...], vx_ref[...])
        return
    sink_ref, bias_a, bias_b, q_ref, k_refs, v_refs, o_ref = (*refs[:4], refs[4:9], refs[9:14], refs[14])
    k_blocks = [r[...] for r in k_refs]
    v_blocks = [r[...] for r in v_refs]
    for half, bias_ref in enumerate((bias_a, bias_b)):
        k = jnp.concatenate(k_blocks[half:half + 3] + k_blocks[4:], axis=0)
        v = jnp.concatenate(v_blocks[half:half + 3] + v_blocks[4:], axis=0)
        rows = slice(half * BLK, (half + 1) * BLK)
        o_ref[rows, :] = _attn_a_block(sink_ref, bias_ref[...], q_ref[rows, :], k, v)


def _band_bias():
    i = jnp.arange(BLK)[:, None]
    j = jnp.arange(3 * BLK)[None, :] - BLK
    in_band = jnp.abs(i - j) <= WINDOW
    first = in_band & (j >= 0)
    last = in_band & (j < BLK)
    return jnp.where(jnp.stack([first, in_band, last]), 0.0, -jnp.inf).astype(F32)


def _attn_a(li, qa, ka, va, sink, with_ctx):
    nb = SEQ // BLK
    off = CTX_LEN // BLK
    smem = pl.BlockSpec(memory_space=pltpu.SMEM)
    bias_a = pl.BlockSpec((None, BLK, 3 * BLK), lambda b, n: (jnp.where(n == 0, 0, 1), 0, 0))
    bias_b = pl.BlockSpec((None, BLK, 3 * BLK), lambda b, n: (jnp.where(n == nb // 2 - 1, 2, 1), 0, 0))

    def kv(shift):
        return pl.BlockSpec(
            (None, BLK, LANES), lambda b, n: (b, off + jnp.clip(2 * n + shift, 0, nb - 1), 0))

    ctx_kv = pl.BlockSpec((None, CTX_LEN, LANES), lambda b, n: (b, 0, 0))
    q_spec = pl.BlockSpec((None, 2 * BLK, 384), lambda b, n: (b, CTX_LEN // (2 * BLK) + n, 0))
    kvs = [kv(-1), kv(0), kv(1), kv(2), ctx_kv]
    y_lat = pl.pallas_call(
        functools.partial(_attn_a_kernel, local=True),
        grid=(BATCH, nb // 2),
        in_specs=[smem, bias_a, bias_b, q_spec] + kvs + kvs,
        out_specs=pl.BlockSpec((None, 2 * BLK, 384), lambda b, n: (b, n, 0)),
        out_shape=jax.ShapeDtypeStruct((BATCH, SEQ, 384), BF16),
        compiler_params=_cparams(("parallel", "parallel")),
        name=f"attn_a_l{li}",
    )(sink, _band_bias(), _band_bias(), qa, *([ka] * 5), *([va] * 5))
    if not with_ctx:
        return y_lat, None
    ctx_spec384 = pl.BlockSpec((None, CTX_LEN, 384), lambda b: (b, 0, 0))
    ctx_spec128 = pl.BlockSpec((None, CTX_LEN, LANES), lambda b: (b, 0, 0))
    y_ctx = pl.pallas_call(
        functools.partial(_attn_a_kernel, local=False),
        grid=(BATCH,),
        in_specs=[smem, ctx_spec384, ctx_spec128, ctx_spec128],
        out_specs=ctx_spec384,
        out_shape=jax.ShapeDtypeStruct((BATCH, CTX_LEN, 384), BF16),
        compiler_params=_cparams(("parallel",)),
        name=f"attn_a_ctx_l{li}",
    )(sink, qa, ka, va)
    return y_lat, y_ctx


def _attn_c_kernel(lq_ref, lk_ref, nw_ref, *refs, lam_init):
    q_refs, (k_ref, v_ref, o_ref) = refs[:-3], refs[-3:]
    tq = o_ref.shape[0]
    e = jnp.exp(jnp.sum(lq_ref[...] * lk_ref[...], axis=-1, keepdims=True))
    lam = e[0:1] - e[1:2] + lam_init
    q = jnp.concatenate([r[...] for r in q_refs], axis=0).astype(F32)
    k = k_ref[...]
    lane = lax.broadcasted_iota(jnp.int32, (tq, D_C), 1)
    lane2 = lax.broadcasted_iota(jnp.int32, (2 * tq, D_C), 1)
    acc = jnp.zeros((tq, D_C), F32)
    for h in range(N_HC):
        base = h * 2 * HD_C
        in0 = jnp.where(lane >= base, lane - base, 2 * HD_C) < HD_C
        in1 = jnp.where(lane >= base + HD_C, lane - base - HD_C, HD_C) < HD_C
        qst = jnp.concatenate([jnp.where(in0, q, 0.0), jnp.where(in1, q, 0.0)], axis=0).astype(BF16)
        s = _dot_nt(qst, k)
        p = jnp.exp2(s - jnp.max(s, axis=-1, keepdims=True)).astype(BF16)
        pv = _dot(p, v_ref[h])
        den = jnp.sum(jnp.where(lane2 == (base + 2 * HD_C) % D_C, pv, 0.0), axis=-1, keepdims=True)
        pv = pv * (1.0 / den)
        o = pv[:tq] - lam * pv[tq:]
        inh = jnp.where(lane >= base, lane - base, 2 * HD_C) < 2 * HD_C
        oh = jnp.where(inh, o, 0.0)
        ms = jnp.sum(oh * oh, axis=-1, keepdims=True) * (1.0 / (2 * HD_C))
        acc = acc + oh * lax.rsqrt(ms + EPS)
    o_ref[...] = (acc * nw_ref[...] * (1.0 - lam_init)).astype(BF16)


def _attn_c(li, qc, kc, vc, lam_q, lam_k, nw_row, lam_init, with_ctx):
    parts = 2
    tq = parts * TILE
    off = CTX_LEN // TILE
    q_specs = [pl.BlockSpec((None, TILE, D_C), lambda b, n, i=i: (b, off + parts * n + i, 0))
               for i in range(parts)]
    small = [_full_spec((2, HD_C)), _full_spec((2, HD_C)), _full_spec((1, D_C))]
    kern = functools.partial(_attn_c_kernel, lam_init=lam_init)
    all_kv = pl.BlockSpec((None, ROWS, D_C), lambda b, n: (b, 0, 0))
    all_v = pl.BlockSpec((None, N_HC, ROWS, D_C), lambda b, n: (b, 0, 0, 0))
    y_lat = pl.pallas_call(
        kern,
        grid=(BATCH, SEQ // tq),
        in_specs=small + q_specs + [all_kv, all_v],
        out_specs=pl.BlockSpec((None, tq, D_C), lambda b, n: (b, n, 0)),
        out_shape=jax.ShapeDtypeStruct((BATCH, SEQ, D_C), BF16),
        compiler_params=_cparams(("parallel", "parallel"), VMEM_LIMIT),
        name=f"attn_c_l{li}",
    )(lam_q, lam_k, nw_row, *([qc] * parts), kc, vc)
    if not with_ctx:
        return y_lat, None
    ctx_spec = pl.BlockSpec((None, CTX_LEN, D_C), lambda b: (b, 0, 0))
    small1 = [_full_spec((2, HD_C)), _full_spec((2, HD_C)), _full_spec((1, D_C))]
    y_ctx = pl.pallas_call(
        kern,
        grid=(BATCH,),
        in_specs=small1 + [ctx_spec, ctx_spec,
                           pl.BlockSpec((None, N_HC, CTX_LEN, D_C), lambda b: (b, 0, 0, 0))],
        out_specs=ctx_spec,
        out_shape=jax.ShapeDtypeStruct((BATCH, CTX_LEN, D_C), BF16),
        compiler_params=_cparams(("parallel",)),
        name=f"attn_c_ctx_l{li}",
    )(lam_q, lam_k, nw_row, qc, kc, vc)
    return y_lat, y_ctx


def _ssd_kernel(xbc_ref, z_ref, dt_ref, cw_ref, cb_ref, dtb_ref, alog_ref, dsk_ref, nw_ref,
                ef_ref, eb_ref, y_ref, u_scr, yacc_scr, st_scr):
    cw = cw_ref[...]
    cb = cb_ref[...]
    dsk = dsk_ref[...]
    pad = 8
    for c in range(N_CHUNKS):
        r0 = CHUNK * c
        zeros = jnp.zeros((pad, CONV_DIM), F32)
        if c in (0, N_CTX_CHUNKS):
            blk = jnp.concatenate([zeros, xbc_ref[r0:r0 + CHUNK + pad, :]], axis=0)
        elif c in (N_CTX_CHUNKS - 1, N_CHUNKS - 1):
            blk = jnp.concatenate([xbc_ref[r0 - pad:r0 + CHUNK, :], zeros], axis=0)
        else:
            blk = xbc_ref[r0 - pad:r0 + CHUNK + pad, :]
        acc = None
        for j in range(CONV_K):
            sh = (CONV_K // 2 - j) % (CHUNK + 2 * pad)
            rolled = blk if sh == 0 else pltpu.roll(blk, sh, 0)
            term = rolled[pad:pad + CHUNK, :] * cw[j:j + 1, :]
            acc = term if acc is None else acc + term
        u = _silu(acc + cb)
        u_scr[r0:r0 + CHUNK, :] = u
        yacc_scr[r0:r0 + CHUNK, :] = u[:, :D_INNER] * dsk

    st_scr[...] = jnp.zeros(st_scr.shape, F32)
    ri = lax.broadcasted_iota(jnp.int32, (CHUNK, CHUNK), 0)
    ci = lax.broadcasted_iota(jnp.int32, (CHUNK, CHUNK), 1)
    lower = ri >= ci
    upper = ri <= ci
    tri = (jnp.where(lower, 1.0, 0.0).astype(BF16), jnp.where(upper, 1.0, 0.0).astype(BF16))
    a_row = -jnp.exp(alog_ref[...])
    dtb = dtb_ref[...]
    lane_lo = ci < HD_S

    def chunk_dir(c, d):
        r0 = pl.multiple_of(c * CHUNK, CHUNK)
        u = u_scr[pl.ds(r0, CHUNK), :]
        xs = u[:, :D_INNER]
        bm = u[:, D_INNER:D_INNER + 2 * D_STATE]
        cm = u[:, D_INNER + 2 * D_STATE:]
        vv = dt_ref[pl.ds(r0, CHUNK), :] + dtb
        dt = jnp.maximum(vv, 0.0) + jnp.log(1.0 + jnp.exp(-jnp.abs(vv)))
        a = dt * a_row
        cum = sum(_dot(tri[d], piece) for piece in _split_bf16(a, 3))
        cum_t = cum.T
        tot = cum[CHUNK - 1:CHUNK, :] if d == 0 else cum[0:1, :]
        expand = ef_ref[...] if d == 0 else eb_ref[...]
        per_head = jnp.concatenate([dt, jnp.exp(cum), jnp.exp(tot - cum)], axis=0)
        per_lane = sum(_dot(piece, expand) for piece in _split_bf16(per_head, 2))
        dt_x = per_lane[:CHUNK]
        dec_out_x = per_lane[CHUNK:2 * CHUNK]
        dec_st_x = per_lane[2 * CHUNK:]
        etot_x = dec_out_x[CHUNK - 1:CHUNK, :] if d == 0 else dec_out_x[0:1, :]
        xdt = xs * dt_x
        xdt_b = xdt.astype(BF16)
        xst_b = (xdt * dec_st_x).astype(BF16)
        cb_ = [cm[:, g * D_STATE:(g + 1) * D_STATE].astype(BF16) for g in range(2)]
        bb_ = [bm[:, g * D_STATE:(g + 1) * D_STATE].astype(BF16) for g in range(2)]
        bt_ = [bm[:, g * D_STATE:(g + 1) * D_STATE].T.astype(BF16) for g in range(2)]
        cbm = [_dot_nt(cb_[g], bb_[g]) for g in range(2)]
        mask = lower if d == 0 else upper
        for j in range(3):
            sl = slice(j * LANES, (j + 1) * LANES)
            parts = []
            for h in (2 * j, 2 * j + 1):
                hl = h + N_HS * d
                seg = jnp.where(mask, cum[:, hl:hl + 1] - cum_t[hl:hl + 1, :], -jnp.inf)
                sc = (cbm[h // 3] * jnp.exp(seg)).astype(BF16)
                parts.append(_dot(sc, xdt_b[:, sl]))
            y_diag = jnp.where(lane_lo, parts[0], parts[1])
            st = st_scr[d, j]
            st_b = st.astype(BF16)
            g0, g1 = (2 * j) // 3, (2 * j + 1) // 3
            if g0 == g1:
                y_off = _dot(cb_[g0], st_b)
                upd = _dot(bt_[g0], xst_b[:, sl])
            else:
                y_off = jnp.where(lane_lo, _dot(cb_[g0], st_b), _dot(cb_[g1], st_b))
                upd = jnp.where(lane_lo, _dot(bt_[g0], xst_b[:, sl]), _dot(bt_[g1], xst_b[:, sl]))
            yacc_scr[pl.ds(r0, CHUNK), sl] += y_diag + y_off * dec_out_x[:, sl]
            st_scr[d, j] = etot_x[:, sl] * st + upd

    def body(i, carry):
        chunk_dir(i, 0)
        chunk_dir(jnp.where(i < N_CTX_CHUNKS, N_CTX_CHUNKS - 1 - i, N_CHUNKS + N_CTX_CHUNKS - 1 - i), 1)
        return carry

    lax.fori_loop(0, N_CHUNKS, body, 0)

    nw = nw_ref[...]
    for t in range(N_TILES):
        rows = slice(t * TILE, (t + 1) * TILE)
        y = yacc_scr[rows, :] * _silu(z_ref[rows, :])
        ms = jnp.mean(y * y, axis=-1, keepdims=True)
        y_ref[rows, :] = (y * lax.rsqrt(ms + EPS) * nw).astype(BF16)


def _ssd(li, xbc, z, dt, cw, cb, dtb, alog, dsk, nw, ef, eb):
    def seq(width):
        return pl.BlockSpec((None, ROWS, width), lambda b: (b, 0, 0))

    return pl.pallas_call(
        _ssd_kernel,
        grid=(BATCH,),
        in_specs=[seq(CONV_DIM), seq(D_INNER), seq(LANES),
                  _full_spec((8, CONV_DIM)), _full_spec((1, CONV_DIM)),
                  _full_spec((1, LANES)), _full_spec((1, LANES)),
                  _full_spec((1, D_INNER)), _full_spec((1, D_INNER)),
                  _full_spec((LANES, D_INNER)), _full_spec((LANES, D_INNER))],
        out_specs=seq(D_INNER),
        out_shape=jax.ShapeDtypeStruct((BATCH, ROWS, D_INNER), BF16),
        scratch_shapes=[pltpu.VMEM((ROWS, CONV_DIM), F32), pltpu.VMEM((ROWS, D_INNER), F32),
                        pltpu.VMEM((2, 3, D_STATE, LANES), F32)],
        compiler_params=_cparams(("parallel",), VMEM_LIMIT),
        name=f"ssd_l{li}",
    )(xbc, z, dt, cw, cb, dtb, alog, dsk, nw, ef, eb)


def _top2_gates(logits):
    lane = lax.broadcasted_iota(jnp.int32, logits.shape, 1).astype(F32)
    neg = -jnp.inf
    lg = jnp.where(lane < N_EXP, logits, neg)
    m1 = jnp.max(lg, axis=-1, keepdims=True)
    i1 = jnp.min(jnp.where(lg == m1, lane, float(LANES)), axis=-1, keepdims=True)
    lg2 = jnp.where(lane == i1, neg, lg)
    m2 = jnp.max(lg2, axis=-1, keepdims=True)
    i2 = jnp.min(jnp.where(lg2 == m2, lane, float(LANES)), axis=-1, keepdims=True)
    e2 = jnp.exp(m2 - m1)
    return lane, i1, i2, 1.0 / (1.0 + e2), e2 / (1.0 + e2)


R_E1, R_E2, R_W1, R_W2, R_RANK1, R_RANK2 = range(6)


def _route_record(logits, cnt_scr):
    lane, i1, i2, w1, w2 = _top2_gates(logits)
    rows = logits.shape[0]
    hit1, hit2 = lane == i1, lane == i2
    onehot = jnp.where(hit1, 1.0, 0.0) + jnp.where(hit2, 1.0, 0.0)
    ri = lax.broadcasted_iota(jnp.int32, (rows, rows), 0)
    ci = lax.broadcasted_iota(jnp.int32, (rows, rows), 1)
    earlier = jnp.where(ri > ci, 1.0, 0.0).astype(BF16)
    before = _dot(earlier, onehot.astype(BF16)) + cnt_scr[...]
    rank1 = jnp.sum(jnp.where(hit1, before, 0.0), axis=-1, keepdims=True)
    rank2 = jnp.sum(jnp.where(hit2, before, 0.0), axis=-1, keepdims=True)
    cnt_scr[...] += jnp.sum(onehot, axis=0, keepdims=True)
    rec = jnp.zeros(logits.shape, F32)
    for idx, val in ((R_E1, i1), (R_E2, i2), (R_W1, w1), (R_W2, w2), (R_RANK1, rank1), (R_RANK2, rank2)):
        rec = jnp.where(lane == float(idx), val, rec)
    return rec


def _outproj_kernel(*refs, moe, ctx_split):
    if ctx_split:
        ya_l, ya_c, yb_ref, yd_l, yd_c = refs[:5]
        refs = refs[5:]
    else:
        ya_l, yb_ref, yd_l = refs[:3]
        refs = refs[3:]
    x_ref, g1_ref, lg_ref, lb_ref, sh_ref, sc_ref, wo_ref = refs[:7]
    refs = refs[7:]
    if moe:
        wr_ref, br_ref, x1_ref, h2_ref, route_ref, counts_ref, cnt_scr = refs

        @pl.when((pl.program_id(0) == 0) & (pl.program_id(1) == 0))
        def _():
            cnt_scr[...] = jnp.zeros(cnt_scr.shape, F32)
    else:
        x1_ref, h2_ref = refs
    if ctx_split:
        is_ctx = pl.program_id(1) == 0
        ya = jnp.where(is_ctx, ya_c[...], ya_l[...])
        yd = jnp.where(is_ctx, yd_c[...], yd_l[...])
    else:
        ya, yd = ya_l[...], yd_l[...]
    mix = (_dot(ya, wo_ref[0:384, :]) + _dot(yb_ref[...], wo_ref[384:768, :])
           + _dot(yd, wo_ref[768:1024, :]))
    x1 = _ln(ALPHA * x_ref[...] + g1_ref[...] * mix) * lg_ref[...] + lb_ref[...]
    x1_ref[...] = x1
    h2 = _ln(x1) * (1.0 + sc_ref[...]) + sh_ref[...]
    h2_ref[...] = h2.astype(h2_ref.dtype)
    if moe:
        h_hi, h_mid = _split_bf16(h2, 2)
        both = _dot(h_hi, wr_ref[...])
        logits = (both[:, :LANES] + both[:, LANES:] + _dot(h_mid, wr_ref[:, :LANES])) + br_ref[...]
        route_ref[...] = _route_record(logits, cnt_scr)
        counts_ref[...] = cnt_scr[...]


def _outproj(li, ya_lat, ya_ctx, yb, yd_lat, yd_ctx, x_all, mod5, lg, lb, wo, moe, wr, br, j0):
    ctx_split = j0 == 0
    nt = N_TILES - j0
    off = CTX_LEN // TILE
    lat = lambda w: pl.BlockSpec((None, TILE, w), lambda b, j: (b, jnp.maximum(j + j0 - off, 0), 0))
    ctx = lambda w: pl.BlockSpec((None, TILE, w), lambda b, j: (b, 0, 0))
    row = _full_spec((1, D_MODEL))
    if ctx_split:
        ins = [ya_lat, ya_ctx, yb, yd_lat, yd_ctx]
        specs = [lat(384), ctx(384), _tok_spec(D_INNER, j0), lat(D_C), ctx(D_C)]
    else:
        ins = [ya_lat, yb, yd_lat]
        specs = [lat(384), _tok_spec(D_INNER, j0), lat(D_C)]
    ins += [x_all, mod5, lg, lb, mod5, mod5, wo]
    specs += [_tok_spec(D_MODEL, j0), _mod_spec(li, 2, j0), row, row,
              _mod_spec(li, 3, j0), _mod_spec(li, 4, j0), _full_spec((D_MODEL, D_MODEL))]
    outs = [jax.ShapeDtypeStruct((BATCH, nt * TILE, D_MODEL), F32),
            jax.ShapeDtypeStruct((BATCH, nt * TILE, D_MODEL), F32 if moe else BF16)]
    out_specs = [_tok_spec(D_MODEL), _tok_spec(D_MODEL)]
    scratch = []
    if moe:
        ins += [wr, br]
        specs += [_full_spec((D_MODEL, 2 * LANES)), _full_spec((1, LANES))]
        outs += [jax.ShapeDtypeStruct((BATCH, nt * TILE, LANES), F32),
                 jax.ShapeDtypeStruct((1, LANES), F32)]
        out_specs += [_tok_spec(LANES), _full_spec((1, LANES))]
        scratch = [pltpu.VMEM((1, LANES), F32)]
    sem = ("arbitrary", "arbitrary") if moe else ("parallel", "parallel")
    return pl.pallas_call(
        functools.partial(_outproj_kernel, moe=moe, ctx_split=ctx_split),
        grid=(BATCH, nt),
        in_specs=specs,
        out_specs=out_specs,
        out_shape=outs,
        scratch_shapes=scratch,
        compiler_params=_cparams(sem, VMEM_LIMIT),
        name=f"outproj_l{li}",
    )(*ins)


def _swiglu_acc(h2, wgu_ref, wd_ref):
    acc = jnp.zeros((h2.shape[0], D_MODEL), F32)
    for c in range(N_FF_CHUNKS):
        cols = slice(c * FF_CHUNK, (c + 1) * FF_CHUNK)
        up_cols = slice(D_FF + c * FF_CHUNK, D_FF + (c + 1) * FF_CHUNK)
        act = (_silu(_dot(h2, wgu_ref[:, cols])) * _dot(h2, wgu_ref[:, up_cols])).astype(BF16)
        acc = acc + _dot(act, wd_ref[cols, :])
    return acc


W_STAGE = 256
GU_STAGE = 128


def _stage_weights(lead, wgu_hbm, wd_hbm, wgu_bf, wd_bf, stage_gu, stage_d, sems):
    def gu_src(c):
        return wgu_hbm.at[lead + (pl.ds(c * GU_STAGE, GU_STAGE), slice(None))]

    def d_src(c):
        return wd_hbm.at[lead + (pl.ds(c * W_STAGE, W_STAGE), slice(None))]

    n_gu, n_d = D_MODEL // GU_STAGE, D_FF // W_STAGE
    copies = [pltpu.make_async_copy(gu_src(c), stage_gu.at[c % 2], sems.at[c % 2]) for c in range(n_gu)]
    copies += [pltpu.make_async_copy(d_src(c), stage_d.at[c % 2], sems.at[2 + c % 2]) for c in range(n_d)]
    copies[0].start()
    for i, copy in enumerate(copies):
        if i + 1 < len(copies):
            copies[i + 1].start()
        copy.wait()
        if i < n_gu:
            wgu_bf[i * GU_STAGE:(i + 1) * GU_STAGE, :] = stage_gu[i % 2].astype(BF16)
        else:
            c = i - n_gu
            wd_bf[c * W_STAGE:(c + 1) * W_STAGE, :] = stage_d[c % 2].astype(BF16)


_WEIGHT_SCRATCH = [pltpu.VMEM((D_MODEL, 2 * D_FF), BF16), pltpu.VMEM((D_FF, D_MODEL), BF16),
                   pltpu.VMEM((2, GU_STAGE, 2 * D_FF), F32), pltpu.VMEM((2, W_STAGE, D_MODEL), F32),
                   pltpu.SemaphoreType.DMA((4,))]


def _ffn_dense_kernel(x1_ref, h2_ref, g2_ref, lg_ref, lb_ref, wgu_hbm, wd_hbm, o_ref,
                      wgu_bf, wd_bf, stage_gu, stage_d, sems, *, fj):
    @pl.when((pl.program_id(0) == 0) & (pl.program_id(1) == 0))
    def _():
        _stage_weights((fj,), wgu_hbm, wd_hbm, wgu_bf, wd_bf, stage_gu, stage_d, sems)

    y = _swiglu_acc(h2_ref[...], wgu_bf, wd_bf)
    o_ref[...] = _ln(ALPHA * x1_ref[...] + g2_ref[...] * y) * lg_ref[...] + lb_ref[...]


def _ffn_dense(li, x1, h2, mod5, lg, lb, wgu, wd):
    row = _full_spec((1, D_MODEL))
    return pl.pallas_call(
        functools.partial(_ffn_dense_kernel, fj=li // 2),
        grid=(BATCH, N_TILES),
        in_specs=[_tok_spec(D_MODEL), _tok_spec(D_MODEL), _mod_spec(li, 5, 0), row, row,
                  pl.BlockSpec(memory_space=pl.ANY), pl.BlockSpec(memory_space=pl.ANY)],
        out_specs=_tok_spec(D_MODEL),
        out_shape=jax.ShapeDtypeStruct((BATCH, ROWS, D_MODEL), F32),
        scratch_shapes=_WEIGHT_SCRATCH,
        compiler_params=_cparams(("arbitrary", "arbitrary"), VMEM_LIMIT),
        name=f"ffn_dense_l{li}",
    )(x1, h2, mod5, lg, lb, wgu, wd)


EXP_TILE = 256


def _slot_row_copy(src_ref, src_row, dst_ref, dst_row, sem):
    return pltpu.make_async_copy(src_ref.at[pl.ds(src_row, 1), :], dst_ref.at[pl.ds(dst_row, 1), :], sem)


def _wait_rows(src_ref, dst_ref, sem):
    pltpu.make_async_copy(src_ref, dst_ref, sem).wait()


def _dispatch_kernel(pos_ref, fill_start_ref, fill_rows_ref, h2_ref, xs_ref, zero_scr, sem, zsem):
    first = (pl.program_id(0) == 0) & (pl.program_id(1) == 0)
    sub = 8
    pad_chunks = [1 << s for s in reversed(range(sub.bit_length() - 1, EXP_TILE.bit_length() - 1))]

    def pad_copies(e):
        start, n = fill_start_ref[e], fill_rows_ref[e]
        head = (-start) & (sub - 1)
        for i in range(sub - 1):
            yield i < jnp.minimum(head, n), _slot_row_copy(zero_scr, i, xs_ref, start + i, zsem)
        body = jnp.maximum(n - head, 0)
        for rows in pad_chunks:
            off = pl.multiple_of(start + head + (body & ~(2 * rows - 1)), sub)
            dst = xs_ref.at[pl.ds(off, rows), :]
            yield (body & rows) != 0, pltpu.make_async_copy(zero_scr.at[pl.ds(0, rows), :], dst, zsem)

    @pl.when(first)
    def _():
        zero_scr[...] = jnp.zeros(zero_scr.shape, F32)
        for e in range(fill_rows_ref.shape[0]):
            for cond, copy in pad_copies(e):
                pl.when(cond)(copy.start)

    base = (pl.program_id(0) * pl.num_programs(1) + pl.program_id(1)) * (2 * TILE)

    for r in range(TILE):
        for k in range(2):
            _slot_row_copy(h2_ref, r, xs_ref, pos_ref[base + 2 * r + k], sem).start(priority=k)
    for k in range(2):
        _wait_rows(h2_ref, xs_ref.at[pl.ds(0, TILE), :], sem)

    @pl.when(first)
    def _():
        for e in range(fill_rows_ref.shape[0]):
            for cond, copy in pad_copies(e):
                pl.when(cond)(copy.wait)


def _grouped_kernel(tile_e_ref, n_used_ref, xs_ref, wgu_hbm, wd_hbm, ys_ref,
                    wgu_bf, wd_bf, stage_gu, stage_d, sems, *, fj):
    i = pl.program_id(0)
    live = i < n_used_ref[0]
    e = tile_e_ref[i]

    @pl.when(live & ((i == 0) | (e != tile_e_ref[jnp.maximum(i - 1, 0)])))
    def _():
        _stage_weights((fj, e), wgu_hbm, wd_hbm, wgu_bf, wd_bf, stage_gu, stage_d, sems)

    @pl.when(live)
    def _():
        ys_ref[...] = _swiglu_acc(xs_ref[...].astype(BF16), wgu_bf, wd_bf)

    @pl.when(jnp.logical_not(live))
    def _():
        ys_ref[...] = jnp.zeros(ys_ref.shape, F32)


def _combine_kernel(pos_ref, ys_ref, route_ref, x1_ref, g2_ref, lg_ref, lb_ref, o_ref, ybuf, sem):
    base = (pl.program_id(0) * pl.num_programs(1) + pl.program_id(1)) * (2 * TILE)

    for r in range(TILE):
        for k in range(2):
            _slot_row_copy(ys_ref, pos_ref[base + 2 * r + k], ybuf.at[k], r, sem).start(priority=k)
    for k in range(2):
        _wait_rows(ys_ref.at[pl.ds(0, TILE), :], ybuf.at[k], sem)
    route = route_ref[...]
    lane = lax.broadcasted_iota(jnp.int32, route.shape, 1)
    w1 = jnp.sum(jnp.where(lane == R_W1, route, 0.0), axis=-1, keepdims=True)
    w2 = jnp.sum(jnp.where(lane == R_W2, route, 0.0), axis=-1, keepdims=True)
    y = w1 * ybuf[0] + w2 * ybuf[1]
    o_ref[...] = _ln(ALPHA * x1_ref[...] + g2_ref[...] * y) * lg_ref[...] + lb_ref[...]


def _ffn_routed(li, x1, h2, route, counts, mod5, lg, lb, wgu, wd, j0):
    nt = N_TILES - j0
    n_tok = BATCH * nt * TILE
    n_tiles = 2 * n_tok // EXP_TILE + N_EXP
    n_slots = n_tiles * EXP_TILE
    cnt = counts[0, :N_EXP].astype(jnp.int32)
    padded = (cnt + EXP_TILE - 1) // EXP_TILE * EXP_TILE
    ends = jnp.cumsum(padded)
    offs = ends - padded
    n_used = ends[-1:] // EXP_TILE
    tile_start = jnp.minimum(jnp.arange(n_tiles), n_used - 1) * EXP_TILE
    tile_e = jnp.sum((tile_start[:, None] >= ends[None, :]).astype(jnp.int32), axis=1)
    rec = route.reshape(n_tok, LANES)
    experts = rec[:, R_E1:R_E2 + 1].astype(jnp.int32)
    ranks = rec[:, R_RANK1:R_RANK2 + 1].astype(jnp.int32)
    off_of = jnp.sum(jnp.where(experts[..., None] == jnp.arange(N_EXP), offs, 0), axis=-1)
    pos = (off_of + ranks).reshape(2 * n_tok)

    h2_flat = h2.reshape(n_tok, D_MODEL)
    half = EXP_TILE // 2
    tail_start = n_used * EXP_TILE + jnp.arange(2 * N_EXP) * half
    tail_rows = jnp.where(tail_start < n_slots, half, 0)
    tail_start = jnp.minimum(tail_start, n_slots - half)
    any_spec = pl.BlockSpec(memory_space=pl.ANY)
    xs = pl.pallas_call(
        _dispatch_kernel,
        grid_spec=pltpu.PrefetchScalarGridSpec(
            num_scalar_prefetch=3,
            grid=(BATCH, nt),
            in_specs=[pl.BlockSpec((TILE, D_MODEL), lambda b, j, *_: (b * nt + j, 0))],
            out_specs=any_spec,
            scratch_shapes=[pltpu.VMEM((EXP_TILE // 2, D_MODEL), F32),
                            pltpu.SemaphoreType.DMA(()), pltpu.SemaphoreType.DMA(())]),
        out_shape=jax.ShapeDtypeStruct((n_slots, D_MODEL), F32),
        compiler_params=_cparams(("arbitrary", "arbitrary")),
        name=f"moe_dispatch_l{li}",
    )(pos, jnp.concatenate([offs + cnt, tail_start]), jnp.concatenate([padded - cnt, tail_rows]), h2_flat)

    used = lambda i, te, nu: jnp.minimum(i, nu[0] - 1)
    ys = pl.pallas_call(
        functools.partial(_grouped_kernel, fj=li // 2),
        grid_spec=pltpu.PrefetchScalarGridSpec(
            num_scalar_prefetch=2,
            grid=(n_tiles,),
            in_specs=[
                pl.BlockSpec((EXP_TILE, D_MODEL), lambda i, te, nu: (used(i, te, nu), 0)),
                pl.BlockSpec(memory_space=pl.ANY), pl.BlockSpec(memory_space=pl.ANY)],
            out_specs=pl.BlockSpec((EXP_TILE, D_MODEL), lambda i, te, nu: (i, 0)),
            scratch_shapes=_WEIGHT_SCRATCH),
        out_shape=jax.ShapeDtypeStruct((n_slots, D_MODEL), F32),
        compiler_params=_cparams(("arbitrary",), VMEM_LIMIT),
        name=f"moe_grouped_l{li}",
    )(tile_e, n_used, xs, wgu, wd)

    row = pl.BlockSpec((1, D_MODEL), lambda b, j, pos: (0, 0))
    tok = lambda w: pl.BlockSpec((None, TILE, w), lambda b, j, pos: (b, j, 0))
    mod = pl.BlockSpec((None, None, None, 1, D_MODEL),
                       lambda b, j, pos: (li, jnp.where(j + j0 == 0, CTX_MOD_ROW, b), 5, 0, 0))
    return pl.pallas_call(
        _combine_kernel,
        grid_spec=pltpu.PrefetchScalarGridSpec(
            num_scalar_prefetch=1,
            grid=(BATCH, nt),
            in_specs=[any_spec, tok(LANES), tok(D_MODEL), mod, row, row],
            out_specs=tok(D_MODEL),
            scratch_shapes=[pltpu.VMEM((2, TILE, D_MODEL), F32), pltpu.SemaphoreType.DMA(())]),
        out_shape=jax.ShapeDtypeStruct((BATCH, nt * TILE, D_MODEL), F32),
        compiler_params=_cparams(("arbitrary", "arbitrary"), VMEM_LIMIT),
        name=f"moe_combine_l{li}",
    )(pos, ys, route, x1, mod5, lg, lb)


def _rope_tables(dim):
    rows_n = SEQ // GRID_W
    t_row = jnp.repeat(jnp.arange(rows_n, dtype=F32), GRID_W)
    t_col = jnp.tile(jnp.arange(GRID_W, dtype=F32), rows_n)
    quarter = dim // 4
    inv = ROPE_BASE ** (-jnp.arange(quarter, dtype=F32) / quarter)
    ang = jnp.concatenate([t_row[:, None] * inv, t_col[:, None] * inv], axis=-1)
    cos, sin = jnp.cos(ang), jnp.sin(ang)
    reps = LANES // dim
    cos_f = jnp.tile(jnp.concatenate([cos, cos], axis=-1), (1, reps))
    sin_s = jnp.tile(jnp.concatenate([-sin, sin], axis=-1), (1, reps))
    cos_f = jnp.concatenate([jnp.ones((CTX_LEN, LANES), F32), cos_f], axis=0)
    sin_s = jnp.concatenate([jnp.zeros((CTX_LEN, LANES), F32), sin_s], axis=0)
    return cos_f, sin_s


def _head_expand(offset):
    rows = jnp.arange(LANES)[:, None]
    cols = jnp.arange(D_INNER)[None, :]
    return (rows == cols // HD_S + offset).astype(BF16)


def kernel(x, c, ctx, c_ctx, w_ada, b_ada, w_in, attn_sink, conv_w, conv_b, dt_bias, a_log, d_skip,
           ssm_norm_w, lam_q, lam_k, diff_norm_w, w_out, ln1_g, ln1_b, ln2_g, ln2_b, ffn_w_gu,
           ffn_w_down, router_w, router_b, exp_w_gu, exp_w_down):
    x_all = jnp.concatenate([ctx, x], axis=1)
    c_all = jnp.concatenate(
        [c, c_ctx[None, :], jnp.zeros((MOD_ROWS - BATCH - 1, D_MODEL), F32)], axis=0)
    mod5 = _ada_all(c_all, w_ada, b_ada).reshape(DEPTH, MOD_ROWS, 6, 1, D_MODEL)
    ropes = _rope_tables(HD_A) + _rope_tables(HD_C)
    ef, eb = _head_expand(0), _head_expand(N_HS)

    for li in range(DEPTH):
        ctx_out = li < DEPTH - 1
        j0 = 0 if ctx_out else CTX_LEN // TILE
        lam_init = 0.8 - 0.6 * math.exp(-0.3 * li)
        w = w_in[li]
        wq = w[:, :384].reshape(D_MODEL, N_KVA, 3, HD_A).transpose(0, 2, 1, 3).reshape(D_MODEL, 384)
        w_re = jnp.concatenate(
            [wq, w[:, 384:1932], jnp.zeros((D_MODEL, D_IN_PAD - 2700), F32), w[:, 1932:2700]],
            axis=1).astype(BF16)
        qa, ka, va, z, xbc, dt, qc, kc, vc = _inproj(li, x_all, mod5, w_re, ropes)

        ya_lat, ya_ctx = _attn_a(li, qa, ka, va, attn_sink[li], ctx_out)
        nw_c = jnp.tile(diff_norm_w[li], N_HC)[None, :]
        yd_lat, yd_ctx = _attn_c(li, qc, kc, vc, lam_q[li], lam_k[li], nw_c, lam_init, ctx_out)
        cw = jnp.pad(conv_w[li], ((0, 8 - CONV_K), (0, 0)))
        pad12 = lambda a: jnp.pad(a.reshape(1, 2 * N_HS), ((0, 0), (0, LANES - 2 * N_HS)))
        yb = _ssd(li, xbc, z, dt, cw, conv_b[li][None, :], pad12(dt_bias[li]), pad12(a_log[li]),
                  jnp.repeat(d_skip[li], HD_S)[None, :], ssm_norm_w[li][None, :], ef, eb)

        wo = w_out[li]
        wo_a = wo[:384].reshape(N_KVA, 3, HD_A, D_MODEL).transpose(1, 0, 2, 3).reshape(384, D_MODEL)
        wo_re = jnp.concatenate([wo_a, wo[384:]], axis=0).astype(BF16)
        moe = li % 2 == 1
        fj = li // 2
        if moe:
            wr_f = jnp.pad(router_w[fj], ((0, 0), (0, LANES - N_EXP)))
            wr_hi = lax.bitcast_convert_type(
                lax.bitcast_convert_type(wr_f, jnp.uint32) & jnp.uint32(0xFFFF0000), F32)
            wr = jnp.concatenate([wr_hi.astype(BF16), (wr_f - wr_hi).astype(BF16)], axis=1)
            br = jnp.pad(router_b[fj], (0, LANES - N_EXP))[None, :]
        else:
            wr = br = None
        res = _outproj(li, ya_lat, ya_ctx, yb, yd_lat, yd_ctx, x_all, mod5, ln1_g[li][None, :],
                       ln1_b[li][None, :], wo_re, moe, wr, br, j0)
        lg2, lb2 = ln2_g[li][None, :], ln2_b[li][None, :]
        if moe:
            x1, h2, route, counts = res
            wgu, wd = exp_w_gu, exp_w_down
            x_all = _ffn_routed(li, x1, h2, route, counts, mod5, lg2, lb2, wgu, wd, j0)
        else:
            x1, h2 = res
            wgu, wd = ffn_w_gu, ffn_w_down
            x_all = _ffn_dense(li, x1, h2, mod5, lg2, lb2, wgu, wd)
    return x_all
```

```python
import functools
import math

import jax
import jax.numpy as jnp
from jax import lax
from jax.experimental import pallas as pl
from jax.experimental.pallas import tpu as pltpu

F32 = jnp.float32
BF16 = jnp.bfloat16

D_MODEL = 1024
BATCH = 8
SEQ = 2048
DEPTH = 4
GRID_W = 64
CTX_LEN = 256
BLK = 128
WINDOW = 128
ROPE_BASE = 10000.0
EPS = 1e-6
LOG2E = 1.4426950408889634
HD_A = 64
N_HA = 6
N_KVA = 2
D_INNER = 384
HD_S = 64
N_HS = 6
D_STATE = 128
CONV_K = 5
CONV_DIM = 896
CHUNK = 128
N_HC = 4
HD_C = 32
D_C = 256
D_FF = 2816
N_EXP = 8
ALPHA = (2 * DEPTH) ** 0.25

LANES = 128
TILE = 256
ROWS = CTX_LEN + SEQ
N_TILES = ROWS // TILE
N_CHUNKS = ROWS // CHUNK
N_CTX_CHUNKS = CTX_LEN // CHUNK
FF_CHUNK = 256
N_FF_CHUNKS = D_FF // FF_CHUNK
D_IN_PAD = 2816
MOD_ROWS = 16
CTX_MOD_ROW = BATCH
VMEM_LIMIT = 56 * 1024 * 1024


def _cparams(sem, vmem=None):
    return pltpu.CompilerParams(dimension_semantics=sem, vmem_limit_bytes=vmem)


def _sigmoid(x):
    return 1.0 / (1.0 + jnp.exp(-x))


def _silu(x):
    return x * _sigmoid(x)


def _ln(x):
    mu = jnp.mean(x, axis=-1, keepdims=True)
    xc = x - mu
    var = jnp.mean(xc * xc, axis=-1, keepdims=True)
    return xc * lax.rsqrt(var + EPS)


def _dot(a, b, precision=None):
    return jnp.dot(a, b, preferred_element_type=F32, precision=precision)


def _split_bf16(x, parts):
    out = []
    for _ in range(parts - 1):
        hi = x.astype(BF16)
        out.append(hi)
        x = x - hi.astype(F32)
    out.append(x.astype(BF16))
    return out


def _dot_nt(a, b):
    return lax.dot_general(a, b, (((1,), (1,)), ((), ())), preferred_element_type=F32)


def _ada_kernel(c_ref, w_ref, b_ref, o_ref):
    act = _silu(c_ref[...])
    o_ref[...] = _dot(act.astype(BF16), w_ref[...].astype(BF16)) + b_ref[...]


def _ada_all(c_all, w_ada, b_ada):
    tn = 1536
    return pl.pallas_call(
        _ada_kernel,
        grid=(DEPTH, 6 * D_MODEL // tn),
        in_specs=[
            pl.BlockSpec((MOD_ROWS, D_MODEL), lambda l, j: (0, 0)),
            pl.BlockSpec((None, D_MODEL, tn), lambda l, j: (l, 0, j)),
            pl.BlockSpec((None, 1, tn), lambda l, j: (l, 0, j)),
        ],
        out_specs=pl.BlockSpec((None, MOD_ROWS, tn), lambda l, j: (l, 0, j)),
        out_shape=jax.ShapeDtypeStruct((DEPTH, MOD_ROWS, 6 * D_MODEL), F32),
        compiler_params=_cparams(("parallel", "parallel")),
        name="ada_ln",
    )(c_all, w_ada, b_ada.reshape(DEPTH, 1, 6 * D_MODEL))


def _mod_spec(li, piece, j0):
    return pl.BlockSpec(
        (None, None, None, 1, D_MODEL),
        lambda b, j: (li, jnp.where(j + j0 == 0, CTX_MOD_ROW, b), piece, 0, 0))


def _tok_spec(width, j0=0):
    return pl.BlockSpec((None, TILE, width), lambda b, j: (b, j + j0, 0))


def _full_spec(shape):
    n = len(shape)
    return pl.BlockSpec(shape, lambda *_: (0,) * n)


def _layer_spec(tail, li):
    n = len(tail)
    return pl.BlockSpec((None,) + tuple(tail), lambda *_: (li,) + (0,) * n)


def _rope(v, cos, sin_signed, half):
    lane = lax.broadcasted_iota(jnp.int32, (v.shape[0], LANES), 1)
    first = (lane % (2 * half)) < half
    outs = []
    for s in range(v.shape[1] // LANES):
        blk = v[:, s * LANES:(s + 1) * LANES]
        swapped = jnp.where(first, pltpu.roll(blk, LANES - half, 1), pltpu.roll(blk, half, 1))
        outs.append(blk * cos + swapped * sin_signed)
    return jnp.concatenate(outs, axis=1)


def _inproj_kernel(x_ref, sh_ref, sc_ref, w_ref, cosa_ref, sina_ref, cosc_ref, sinc_ref,
                   qa_ref, ka_ref, va_ref, z_ref, xbc_ref, dt_ref, qc_ref, kc_ref, vc_ref):
    h = _ln(x_ref[...]) * (1.0 + sc_ref[...]) + sh_ref[...]
    hb = h.astype(BF16)

    def proj(lo, hi):
        return _dot(hb, w_ref[:, lo:hi])

    cosa, sina = cosa_ref[...], sina_ref[...]
    cosc, sinc = cosc_ref[...], sinc_ref[...]
    qk_a = proj(0, 512)
    qa_ref[...] = (_rope(qk_a[:, :384], cosa, sina, HD_A // 2) * (HD_A ** -0.5 * LOG2E)).astype(BF16)
    ka_ref[...] = _rope(qk_a[:, 384:], cosa, sina, HD_A // 2).astype(BF16)
    vz = proj(512, 1024)
    va_ref[...] = vz[:, :LANES].astype(BF16)
    z_ref[...] = vz[:, LANES:]
    xbc_dt = proj(1024, 2048)
    xbc_ref[...] = xbc_dt[:, :CONV_DIM]
    dt_ref[...] = xbc_dt[:, CONV_DIM:]
    qc_ref[...] = (_rope(proj(2048, 2304), cosc, sinc, HD_C // 2) * (HD_C ** -0.5 * LOG2E)).astype(BF16)
    kc_ref[...] = _rope(proj(2304, 2560), cosc, sinc, HD_C // 2).astype(BF16)
    vc = proj(2560, 2816)
    lane = lax.broadcasted_iota(jnp.int32, vc.shape, 1)
    for h in range(N_HC):
        in_head = jnp.where(lane >= h * 2 * HD_C, lane - h * 2 * HD_C, 2 * HD_C) < 2 * HD_C
        vc_ref[h] = jnp.where(in_head, vc, 1.0).astype(BF16)


def _inproj(li, x_all, mod5, w_re, ropes):
    widths = (384, 128, 128, 384, CONV_DIM, LANES, D_C, D_C, D_C)
    dtypes = (BF16, BF16, BF16, F32, F32, F32, BF16, BF16, BF16)
    rope_spec = pl.BlockSpec((TILE, LANES), lambda b, j: (j, 0))
    return pl.pallas_call(
        _inproj_kernel,
        grid=(BATCH, N_TILES),
        in_specs=[_tok_spec(D_MODEL), _mod_spec(li, 0, 0), _mod_spec(li, 1, 0),
                  _layer_spec((D_MODEL, D_IN_PAD), li), rope_spec, rope_spec, rope_spec, rope_spec],
        out_specs=[_tok_spec(w) for w in widths[:-1]]
        + [pl.BlockSpec((None, N_HC, TILE, D_C), lambda b, j: (b, 0, j, 0))],
        out_shape=[jax.ShapeDtypeStruct((BATCH, ROWS, w), d) for w, d in zip(widths[:-1], dtypes)]
        + [jax.ShapeDtypeStruct((BATCH, N_HC, ROWS, D_C), BF16)],
        compiler_params=_cparams(("parallel", "parallel"), VMEM_LIMIT),
        name=f"inproj_l{li}",
    )(x_all, mod5, mod5, w_re, *ropes)


def _attn_a_block(sink_ref, bias, q, k, v):
    tq = q.shape[0]
    q = q.astype(F32)
    lane_lo = lax.broadcasted_iota(jnp.int32, (tq, LANES), 1) < HD_A
    slabs = [q[:, s * LANES:(s + 1) * LANES] for s in range(3)]
    qst = jnp.concatenate([jnp.where(lane_lo, blk, 0.0) for blk in slabs]
                          + [jnp.where(lane_lo, 0.0, blk) for blk in slabs], axis=0).astype(BF16)
    s_all = _dot_nt(qst, k)
    probs, sink_terms = [], []
    for rb in range(N_HA):
        sink = sink_ref[rb] * LOG2E
        sh = s_all[rb * tq:(rb + 1) * tq]
        if bias is not None:
            sh = jnp.concatenate([sh[:, :3 * BLK] + bias, sh[:, 3 * BLK:]], axis=1)
        m = jnp.maximum(jnp.max(sh, axis=-1, keepdims=True), sink)
        probs.append(jnp.exp2(sh - m).astype(BF16))
        sink_terms.append(jnp.exp2(sink - m))
    vf = v.astype(F32)
    key_lo = lax.broadcasted_iota(jnp.int32, vf.shape, 1) < HD_A
    v_heads = (jnp.where(key_lo, vf, 1.0).astype(BF16), jnp.where(key_lo, 1.0, vf).astype(BF16))
    outs = []
    for g in range(N_KVA):
        o = _dot(jnp.concatenate(probs[3 * g:3 * g + 3], axis=0), v_heads[g])
        lane3 = lax.broadcasted_iota(jnp.int32, o.shape, 1)
        den = jnp.sum(jnp.where(lane3 == (1 - g) * HD_A, o, 0.0), axis=-1, keepdims=True)
        den = den + jnp.concatenate(sink_terms[3 * g:3 * g + 3], axis=0)
        outs.append(o * (1.0 / den))
    return jnp.concatenate(
        [jnp.where(lane_lo, outs[0][s * tq:(s + 1) * tq], outs[1][s * tq:(s + 1) * tq]) for s in range(3)],
        axis=1).astype(BF16)


def _attn_a_kernel(*refs, local):
    if not local:
        sink_ref, q_ref, kx_ref, vx_ref, o_ref = refs
        o_ref[...] = _attn_a_block(sink_ref, None, q_ref[...], kx_ref[...], vx_ref[...])
        return
    sink_ref, bias_a, bias_b, q_ref, k_refs, v_refs, o_ref = (*refs[:4], refs[4:9], refs[9:14], refs[14])
    k_blocks = [r[...] for r in k_refs]
    v_blocks = [r[...] for r in v_refs]
    for half, bias_ref in enumerate((bias_a, bias_b)):
        k = jnp.concatenate(k_blocks[half:half + 3] + k_blocks[4:], axis=0)
        v = jnp.concatenate(v_blocks[half:half + 3] + v_blocks[4:], axis=0)
        rows = slice(half * BLK, (half + 1) * BLK)
        o_ref[rows, :] = _attn_a_block(sink_ref, bias_ref[...], q_ref[rows, :], k, v)


def _band_bias():
    i = jnp.arange(BLK)[:, None]
    j = jnp.arange(3 * BLK)[None, :] - BLK
    in_band = jnp.abs(i - j) <= WINDOW
    first = in_band & (j >= 0)
    last = in_band & (j < BLK)
    return jnp.where(jnp.stack([first, in_band, last]), 0.0, -jnp.inf).astype(F32)


def _attn_a(li, qa, ka, va, sink, with_ctx):
    nb = SEQ // BLK
    off = CTX_LEN // BLK
    smem = pl.BlockSpec(memory_space=pltpu.SMEM)
    bias_a = pl.BlockSpec((None, BLK, 3 * BLK), lambda b, n: (jnp.where(n == 0, 0, 1), 0, 0))
    bias_b = pl.BlockSpec((None, BLK, 3 * BLK), lambda b, n: (jnp.where(n == nb // 2 - 1, 2, 1), 0, 0))

    def kv(shift):
        return pl.BlockSpec(
            (None, BLK, LANES), lambda b, n: (b, off + jnp.clip(2 * n + shift, 0, nb - 1), 0))

    ctx_kv = pl.BlockSpec((None, CTX_LEN, LANES), lambda b, n: (b, 0, 0))
    q_spec = pl.BlockSpec((None, 2 * BLK, 384), lambda b, n: (b, CTX_LEN // (2 * BLK) + n, 0))
    kvs = [kv(-1), kv(0), kv(1), kv(2), ctx_kv]
    y_lat = pl.pallas_call(
        functools.partial(_attn_a_kernel, local=True),
        grid=(BATCH, nb // 2),
        in_specs=[smem, bias_a, bias_b, q_spec] + kvs + kvs,
        out_specs=pl.BlockSpec((None, 2 * BLK, 384), lambda b, n: (b, n, 0)),
        out_shape=jax.ShapeDtypeStruct((BATCH, SEQ, 384), BF16),
        compiler_params=_cparams(("parallel", "parallel")),
        name=f"attn_a_l{li}",
    )(sink, _band_bias(), _band_bias(), qa, *([ka] * 5), *([va] * 5))
    if not with_ctx:
        return y_lat, None
    ctx_spec384 = pl.BlockSpec((None, CTX_LEN, 384), lambda b: (b, 0, 0))
    ctx_spec128 = pl.BlockSpec((None, CTX_LEN, LANES), lambda b: (b, 0, 0))
    y_ctx = pl.pallas_call(
        functools.partial(_attn_a_kernel, local=False),
        grid=(BATCH,),
        in_specs=[smem, ctx_spec384, ctx_spec128, ctx_spec128],
        out_specs=ctx_spec384,
        out_shape=jax.ShapeDtypeStruct((BATCH, CTX_LEN, 384), BF16),
        compiler_params=_cparams(("parallel",)),
        name=f"attn_a_ctx_l{li}",
    )(sink, qa, ka, va)
    return y_lat, y_ctx


def _attn_c_kernel(lq_ref, lk_ref, nw_ref, *refs, lam_init):
    q_refs, (k_ref, v_ref, o_ref) = refs[:-3], refs[-3:]
    tq = o_ref.shape[0]
    e = jnp.exp(jnp.sum(lq_ref[...] * lk_ref[...], axis=-1, keepdims=True))
    lam = e[0:1] - e[1:2] + lam_init
    q = jnp.concatenate([r[...] for r in q_refs], axis=0).astype(F32)
    k = k_ref[...]
    lane = lax.broadcasted_iota(jnp.int32, (tq, D_C), 1)
    lane2 = lax.broadcasted_iota(jnp.int32, (2 * tq, D_C), 1)
    acc = jnp.zeros((tq, D_C), F32)
    for h in range(N_HC):
        base = h * 2 * HD_C
        in0 = jnp.where(lane >= base, lane - base, 2 * HD_C) < HD_C
        in1 = jnp.where(lane >= base + HD_C, lane - base - HD_C, HD_C) < HD_C
        qst = jnp.concatenate([jnp.where(in0, q, 0.0), jnp.where(in1, q, 0.0)], axis=0).astype(BF16)
        s = _dot_nt(qst, k)
        p = jnp.exp2(s - jnp.max(s, axis=-1, keepdims=True)).astype(BF16)
        pv = _dot(p, v_ref[h])
        den = jnp.sum(jnp.where(lane2 == (base + 2 * HD_C) % D_C, pv, 0.0), axis=-1, keepdims=True)
        pv = pv * (1.0 / den)
        o = pv[:tq] - lam * pv[tq:]
        inh = jnp.where(lane >= base, lane - base, 2 * HD_C) < 2 * HD_C
        oh = jnp.where(inh, o, 0.0)
        ms = jnp.sum(oh * oh, axis=-1, keepdims=True) * (1.0 / (2 * HD_C))
        acc = acc + oh * lax.rsqrt(ms + EPS)
    o_ref[...] = (acc * nw_ref[...] * (1.0 - lam_init)).astype(BF16)


def _attn_c(li, qc, kc, vc, lam_q, lam_k, nw_row, lam_init, with_ctx):
    parts = 2
    tq = parts * TILE
    off = CTX_LEN // TILE
    q_specs = [pl.BlockSpec((None, TILE, D_C), lambda b, n, i=i: (b, off + parts * n + i, 0))
               for i in range(parts)]
    small = [_layer_spec((2, HD_C), li), _layer_spec((2, HD_C), li), _layer_spec((1, D_C), li)]
    kern = functools.partial(_attn_c_kernel, lam_init=lam_init)
    all_kv = pl.BlockSpec((None, ROWS, D_C), lambda b, n: (b, 0, 0))
    all_v = pl.BlockSpec((None, N_HC, ROWS, D_C), lambda b, n: (b, 0, 0, 0))
    y_lat = pl.pallas_call(
        kern,
        grid=(BATCH, SEQ // tq),
        in_specs=small + q_specs + [all_kv, all_v],
        out_specs=pl.BlockSpec((None, tq, D_C), lambda b, n: (b, n, 0)),
        out_shape=jax.ShapeDtypeStruct((BATCH, SEQ, D_C), BF16),
        compiler_params=_cparams(("parallel", "parallel"), VMEM_LIMIT),
        name=f"attn_c_l{li}",
    )(lam_q, lam_k, nw_row, *([qc] * parts), kc, vc)
    if not with_ctx:
        return y_lat, None
    ctx_spec = pl.BlockSpec((None, CTX_LEN, D_C), lambda b: (b, 0, 0))
    y_ctx = pl.pallas_call(
        kern,
        grid=(BATCH,),
        in_specs=small + [ctx_spec, ctx_spec,
                           pl.BlockSpec((None, N_HC, CTX_LEN, D_C), lambda b: (b, 0, 0, 0))],
        out_specs=ctx_spec,
        out_shape=jax.ShapeDtypeStruct((BATCH, CTX_LEN, D_C), BF16),
        compiler_params=_cparams(("parallel",)),
        name=f"attn_c_ctx_l{li}",
    )(lam_q, lam_k, nw_row, qc, kc, vc)
    return y_lat, y_ctx


def _ssd_kernel(xbc_ref, z_ref, dt_ref, cw_ref, cb_ref, dtb_ref, alog_ref, dsk_ref, nw_ref,
                ef_ref, eb_ref, y_ref, u_scr, yacc_scr, st_scr):
    cw = cw_ref[...]
    cb = cb_ref[...]
    dsk = dsk_ref[...]
    pad = 8
    for c in range(N_CHUNKS):
        r0 = CHUNK * c
        zeros = jnp.zeros((pad, CONV_DIM), F32)
        if c in (0, N_CTX_CHUNKS):
            blk = jnp.concatenate([zeros, xbc_ref[r0:r0 + CHUNK + pad, :]], axis=0)
        elif c in (N_CTX_CHUNKS - 1, N_CHUNKS - 1):
            blk = jnp.concatenate([xbc_ref[r0 - pad:r0 + CHUNK, :], zeros], axis=0)
        else:
            blk = xbc_ref[r0 - pad:r0 + CHUNK + pad, :]
        acc = None
        for j in range(CONV_K):
            sh = (CONV_K // 2 - j) % (CHUNK + 2 * pad)
            rolled = blk if sh == 0 else pltpu.roll(blk, sh, 0)
            term = rolled[pad:pad + CHUNK, :] * cw[j:j + 1, :]
            acc = term if acc is None else acc + term
        u = _silu(acc + cb)
        u_scr[r0:r0 + CHUNK, :] = u
        yacc_scr[r0:r0 + CHUNK, :] = u[:, :D_INNER] * dsk

    st_scr[...] = jnp.zeros(st_scr.shape, F32)
    ri = lax.broadcasted_iota(jnp.int32, (CHUNK, CHUNK), 0)
    ci = lax.broadcasted_iota(jnp.int32, (CHUNK, CHUNK), 1)
    lower = ri >= ci
    upper = ri <= ci
    tri = (jnp.where(lower, 1.0, 0.0).astype(BF16), jnp.where(upper, 1.0, 0.0).astype(BF16))
    a_row = -jnp.exp(alog_ref[...])
    dtb = dtb_ref[...]
    lane_lo = ci < HD_S

    def chunk_dir(c, d):
        r0 = pl.multiple_of(c * CHUNK, CHUNK)
        u = u_scr[pl.ds(r0, CHUNK), :]
        xs = u[:, :D_INNER]
        bm = u[:, D_INNER:D_INNER + 2 * D_STATE]
        cm = u[:, D_INNER + 2 * D_STATE:]
        vv = dt_ref[pl.ds(r0, CHUNK), :] + dtb
        dt = jnp.maximum(vv, 0.0) + jnp.log(1.0 + jnp.exp(-jnp.abs(vv)))
        a = dt * a_row
        cum = sum(_dot(tri[d], piece) for piece in _split_bf16(a, 3))
        cum_t = cum.T
        tot = cum[CHUNK - 1:CHUNK, :] if d == 0 else cum[0:1, :]
        expand = ef_ref[...] if d == 0 else eb_ref[...]
        per_head = jnp.concatenate([dt, jnp.exp(cum), jnp.exp(tot - cum)], axis=0)
        per_lane = sum(_dot(piece, expand) for piece in _split_bf16(per_head, 2))
        dt_x = per_lane[:CHUNK]
        dec_out_x = per_lane[CHUNK:2 * CHUNK]
        dec_st_x = per_lane[2 * CHUNK:]
        etot_x = dec_out_x[CHUNK - 1:CHUNK, :] if d == 0 else dec_out_x[0:1, :]
        xdt = xs * dt_x
        xdt_b = xdt.astype(BF16)
        xst_b = (xdt * dec_st_x).astype(BF16)
        cb_ = [cm[:, g * D_STATE:(g + 1) * D_STATE].astype(BF16) for g in range(2)]
        bb_ = [bm[:, g * D_STATE:(g + 1) * D_STATE].astype(BF16) for g in range(2)]
        bt_ = [bm[:, g * D_STATE:(g + 1) * D_STATE].T.astype(BF16) for g in range(2)]
        cbm = [_dot_nt(cb_[g], bb_[g]) for g in range(2)]
        mask = lower if d == 0 else upper
        for j in range(3):
            sl = slice(j * LANES, (j + 1) * LANES)
            parts = []
            for h in (2 * j, 2 * j + 1):
                hl = h + N_HS * d
                seg = jnp.where(mask, cum[:, hl:hl + 1] - cum_t[hl:hl + 1, :], -jnp.inf)
                sc = (cbm[h // 3] * jnp.exp(seg)).astype(BF16)
                parts.append(_dot(sc, xdt_b[:, sl]))
            y_diag = jnp.where(lane_lo, parts[0], parts[1])
            st = st_scr[d, j]
            st_b = st.astype(BF16)
            g0, g1 = (2 * j) // 3, (2 * j + 1) // 3
            if g0 == g1:
                y_off = _dot(cb_[g0], st_b)
                upd = _dot(bt_[g0], xst_b[:, sl])
            else:
                y_off = jnp.where(lane_lo, _dot(cb_[g0], st_b), _dot(cb_[g1], st_b))
                upd = jnp.where(lane_lo, _dot(bt_[g0], xst_b[:, sl]), _dot(bt_[g1], xst_b[:, sl]))
            yacc_scr[pl.ds(r0, CHUNK), sl] += y_diag + y_off * dec_out_x[:, sl]
            st_scr[d, j] = etot_x[:, sl] * st + upd

    def body(i, carry):
        chunk_dir(i, 0)
        chunk_dir(jnp.where(i < N_CTX_CHUNKS, N_CTX_CHUNKS - 1 - i, N_CHUNKS + N_CTX_CHUNKS - 1 - i), 1)
        return carry

    lax.fori_loop(0, N_CHUNKS, body, 0)

    nw = nw_ref[...]
    for t in range(N_TILES):
        rows = slice(t * TILE, (t + 1) * TILE)
        y = yacc_scr[rows, :] * _silu(z_ref[rows, :])
        ms = jnp.mean(y * y, axis=-1, keepdims=True)
        y_ref[rows, :] = (y * lax.rsqrt(ms + EPS) * nw).astype(BF16)


def _ssd(li, xbc, z, dt, cw, cb, dtb, alog, dsk, nw, ef, eb):
    def seq(width):
        return pl.BlockSpec((None, ROWS, width), lambda b: (b, 0, 0))

    return pl.pallas_call(
        _ssd_kernel,
        grid=(BATCH,),
        in_specs=[seq(CONV_DIM), seq(D_INNER), seq(LANES),
                  _layer_spec((8, CONV_DIM), li), _layer_spec((1, CONV_DIM), li),
                  _layer_spec((1, LANES), li), _layer_spec((1, LANES), li),
                  _layer_spec((1, D_INNER), li), _layer_spec((1, D_INNER), li),
                  _full_spec((LANES, D_INNER)), _full_spec((LANES, D_INNER))],
        out_specs=seq(D_INNER),
        out_shape=jax.ShapeDtypeStruct((BATCH, ROWS, D_INNER), BF16),
        scratch_shapes=[pltpu.VMEM((ROWS, CONV_DIM), F32), pltpu.VMEM((ROWS, D_INNER), F32),
                        pltpu.VMEM((2, 3, D_STATE, LANES), F32)],
        compiler_params=_cparams(("parallel",), VMEM_LIMIT),
        name=f"ssd_l{li}",
    )(xbc, z, dt, cw, cb, dtb, alog, dsk, nw, ef, eb)


def _top2_gates(logits):
    lane = lax.broadcasted_iota(jnp.int32, logits.shape, 1).astype(F32)
    neg = -jnp.inf
    lg = jnp.where(lane < N_EXP, logits, neg)
    m1 = jnp.max(lg, axis=-1, keepdims=True)
    i1 = jnp.min(jnp.where(lg == m1, lane, float(LANES)), axis=-1, keepdims=True)
    lg2 = jnp.where(lane == i1, neg, lg)
    m2 = jnp.max(lg2, axis=-1, keepdims=True)
    i2 = jnp.min(jnp.where(lg2 == m2, lane, float(LANES)), axis=-1, keepdims=True)
    e2 = jnp.exp(m2 - m1)
    return lane, i1, i2, 1.0 / (1.0 + e2), e2 / (1.0 + e2)


R_E1, R_E2, R_W1, R_W2, R_RANK1, R_RANK2 = range(6)


def _route_record(logits, cnt_scr):
    lane, i1, i2, w1, w2 = _top2_gates(logits)
    rows = logits.shape[0]
    hit1, hit2 = lane == i1, lane == i2
    onehot = jnp.where(hit1, 1.0, 0.0) + jnp.where(hit2, 1.0, 0.0)
    ri = lax.broadcasted_iota(jnp.int32, (rows, rows), 0)
    ci = lax.broadcasted_iota(jnp.int32, (rows, rows), 1)
    earlier = jnp.where(ri > ci, 1.0, 0.0).astype(BF16)
    before = _dot(earlier, onehot.astype(BF16)) + cnt_scr[...]
    rank1 = jnp.sum(jnp.where(hit1, before, 0.0), axis=-1, keepdims=True)
    rank2 = jnp.sum(jnp.where(hit2, before, 0.0), axis=-1, keepdims=True)
    cnt_scr[...] += jnp.sum(onehot, axis=0, keepdims=True)
    rec = jnp.zeros(logits.shape, F32)
    for idx, val in ((R_E1, i1), (R_E2, i2), (R_W1, w1), (R_W2, w2), (R_RANK1, rank1), (R_RANK2, rank2)):
        rec = jnp.where(lane == float(idx), val, rec)
    return rec


def _outproj_kernel(*refs, moe, ctx_split):
    if ctx_split:
        ya_l, ya_c, yb_ref, yd_l, yd_c = refs[:5]
        refs = refs[5:]
    else:
        ya_l, yb_ref, yd_l = refs[:3]
        refs = refs[3:]
    x_ref, g1_ref, lg_ref, lb_ref, sh_ref, sc_ref, wo_ref = refs[:7]
    refs = refs[7:]
    if moe:
        wr_ref, br_ref, x1_ref, h2_ref, route_ref, counts_ref, cnt_scr = refs

        @pl.when((pl.program_id(0) == 0) & (pl.program_id(1) == 0))
        def _():
            cnt_scr[...] = jnp.zeros(cnt_scr.shape, F32)
    else:
        x1_ref, h2_ref = refs
    if ctx_split:
        is_ctx = pl.program_id(1) == 0
        ya = jnp.where(is_ctx, ya_c[...], ya_l[...])
        yd = jnp.where(is_ctx, yd_c[...], yd_l[...])
    else:
        ya, yd = ya_l[...], yd_l[...]
    mix = (_dot(ya, wo_ref[0:384, :]) + _dot(yb_ref[...], wo_ref[384:768, :])
           + _dot(yd, wo_ref[768:1024, :]))
    x1 = _ln(ALPHA * x_ref[...] + g1_ref[...] * mix) * lg_ref[...] + lb_ref[...]
    x1_ref[...] = x1
    h2 = _ln(x1) * (1.0 + sc_ref[...]) + sh_ref[...]
    h2_ref[...] = h2.astype(h2_ref.dtype)
    if moe:
        h_hi, h_mid = _split_bf16(h2, 2)
        both = _dot(h_hi, wr_ref[...])
        logits = (both[:, :LANES] + both[:, LANES:] + _dot(h_mid, wr_ref[:, :LANES])) + br_ref[...]
        route_ref[...] = _route_record(logits, cnt_scr)
        counts_ref[...] = cnt_scr[...]


def _outproj(li, ya_lat, ya_ctx, yb, yd_lat, yd_ctx, x_all, mod5, lg, lb, wo, moe, wr, br, j0):
    ctx_split = j0 == 0
    nt = N_TILES - j0
    off = CTX_LEN // TILE
    lat = lambda w: pl.BlockSpec((None, TILE, w), lambda b, j: (b, jnp.maximum(j + j0 - off, 0), 0))
    ctx = lambda w: pl.BlockSpec((None, TILE, w), lambda b, j: (b, 0, 0))
    row = _layer_spec((1, D_MODEL), li)
    if ctx_split:
        ins = [ya_lat, ya_ctx, yb, yd_lat, yd_ctx]
        specs = [lat(384), ctx(384), _tok_spec(D_INNER, j0), lat(D_C), ctx(D_C)]
    else:
        ins = [ya_lat, yb, yd_lat]
        specs = [lat(384), _tok_spec(D_INNER, j0), lat(D_C)]
    ins += [x_all, mod5, lg, lb, mod5, mod5, wo]
    specs += [_tok_spec(D_MODEL, j0), _mod_spec(li, 2, j0), row, row,
              _mod_spec(li, 3, j0), _mod_spec(li, 4, j0), _layer_spec((D_MODEL, D_MODEL), li)]
    outs = [jax.ShapeDtypeStruct((BATCH, nt * TILE, D_MODEL), F32),
            jax.ShapeDtypeStruct((BATCH, nt * TILE, D_MODEL), F32 if moe else BF16)]
    out_specs = [_tok_spec(D_MODEL), _tok_spec(D_MODEL)]
    scratch = []
    if moe:
        ins += [wr, br]
        specs += [_layer_spec((D_MODEL, 2 * LANES), li // 2), _layer_spec((1, LANES), li // 2)]
        outs += [jax.ShapeDtypeStruct((BATCH, nt * TILE, LANES), F32),
                 jax.ShapeDtypeStruct((1, LANES), F32)]
        out_specs += [_tok_spec(LANES), _full_spec((1, LANES))]
        scratch = [pltpu.VMEM((1, LANES), F32)]
    sem = ("arbitrary", "arbitrary") if moe else ("parallel", "parallel")
    return pl.pallas_call(
        functools.partial(_outproj_kernel, moe=moe, ctx_split=ctx_split),
        grid=(BATCH, nt),
        in_specs=specs,
        out_specs=out_specs,
        out_shape=outs,
        scratch_shapes=scratch,
        compiler_params=_cparams(sem, VMEM_LIMIT),
        name=f"outproj_l{li}",
    )(*ins)


def _swiglu_acc(h2, wgu_ref, wd_ref):
    acc = jnp.zeros((h2.shape[0], D_MODEL), F32)
    for c in range(N_FF_CHUNKS):
        cols = slice(c * FF_CHUNK, (c + 1) * FF_CHUNK)
        up_cols = slice(D_FF + c * FF_CHUNK, D_FF + (c + 1) * FF_CHUNK)
        act = (_silu(_dot(h2, wgu_ref[:, cols])) * _dot(h2, wgu_ref[:, up_cols])).astype(BF16)
        acc = acc + _dot(act, wd_ref[cols, :])
    return acc


W_STAGE = 256
GU_STAGE = 128


def _stage_weights(lead, wgu_hbm, wd_hbm, wgu_bf, wd_bf, stage_gu, stage_d, sems):
    def gu_src(c):
        return wgu_hbm.at[lead + (pl.ds(c * GU_STAGE, GU_STAGE), slice(None))]

    def d_src(c):
        return wd_hbm.at[lead + (pl.ds(c * W_STAGE, W_STAGE), slice(None))]

    n_gu, n_d = D_MODEL // GU_STAGE, D_FF // W_STAGE
    copies = [pltpu.make_async_copy(gu_src(c), stage_gu.at[c % 2], sems.at[c % 2]) for c in range(n_gu)]
    copies += [pltpu.make_async_copy(d_src(c), stage_d.at[c % 2], sems.at[2 + c % 2]) for c in range(n_d)]
    copies[0].start()
    for i, copy in enumerate(copies):
        if i + 1 < len(copies):
            copies[i + 1].start()
        copy.wait()
        if i < n_gu:
            wgu_bf[i * GU_STAGE:(i + 1) * GU_STAGE, :] = stage_gu[i % 2].astype(BF16)
        else:
            c = i - n_gu
            wd_bf[c * W_STAGE:(c + 1) * W_STAGE, :] = stage_d[c % 2].astype(BF16)


_WEIGHT_SCRATCH = [pltpu.VMEM((D_MODEL, 2 * D_FF), BF16), pltpu.VMEM((D_FF, D_MODEL), BF16),
                   pltpu.VMEM((2, GU_STAGE, 2 * D_FF), F32), pltpu.VMEM((2, W_STAGE, D_MODEL), F32),
                   pltpu.SemaphoreType.DMA((4,))]


FFN_TILE = 3 * TILE


def _ffn_dense_kernel(x1_ref, h2_ref, g2_head_ref, g2_ref, lg_ref, lb_ref, wgu_hbm, wd_hbm, o_ref,
                      wgu_bf, wd_bf, stage_gu, stage_d, sems, *, fj):
    @pl.when((pl.program_id(0) == 0) & (pl.program_id(1) == 0))
    def _():
        _stage_weights((fj,), wgu_hbm, wd_hbm, wgu_bf, wd_bf, stage_gu, stage_d, sems)

    y = _swiglu_acc(h2_ref[...], wgu_bf, wd_bf)
    for rows, gate in ((slice(0, TILE), g2_head_ref), (slice(TILE, FFN_TILE), g2_ref)):
        o_ref[rows, :] = (_ln(ALPHA * x1_ref[rows, :] + gate[...] * y[rows]) * lg_ref[...] + lb_ref[...])


def _ffn_dense(li, x1, h2, mod5, lg, lb, wgu, wd):
    row = _layer_spec((1, D_MODEL), li)
    tok = pl.BlockSpec((None, FFN_TILE, D_MODEL), lambda b, j: (b, j, 0))
    batch_mod = pl.BlockSpec((None, None, None, 1, D_MODEL), lambda b, j: (li, b, 5, 0, 0))
    return pl.pallas_call(
        functools.partial(_ffn_dense_kernel, fj=li // 2),
        grid=(BATCH, ROWS // FFN_TILE),
        in_specs=[tok, tok, _mod_spec(li, 5, 0), batch_mod, row, row,
                  pl.BlockSpec(memory_space=pl.ANY), pl.BlockSpec(memory_space=pl.ANY)],
        out_specs=tok,
        out_shape=jax.ShapeDtypeStruct((BATCH, ROWS, D_MODEL), F32),
        scratch_shapes=_WEIGHT_SCRATCH,
        compiler_params=_cparams(("arbitrary", "arbitrary"), VMEM_LIMIT),
        name=f"ffn_dense_l{li}",
    )(x1, h2, mod5, mod5, lg, lb, wgu, wd)


EXP_TILE = 512


def _slot_row_copy(src_ref, src_row, dst_ref, dst_row, sem):
    return pltpu.make_async_copy(src_ref.at[pl.ds(src_row, 1), :], dst_ref.at[pl.ds(dst_row, 1), :], sem)


def _wait_rows(src_ref, dst_ref, sem):
    pltpu.make_async_copy(src_ref, dst_ref, sem).wait()


def _dispatch_kernel(pos_ref, fill_start_ref, fill_rows_ref, h2_ref, xs_ref, zero_scr, sem, zsem):
    first = (pl.program_id(0) == 0) & (pl.program_id(1) == 0)
    sub = 8
    pad_chunks = [1 << s for s in reversed(range(sub.bit_length() - 1, EXP_TILE.bit_length() - 1))]

    def pad_copies(e):
        start, n = fill_start_ref[e], fill_rows_ref[e]
        head = (-start) & (sub - 1)
        for i in range(sub - 1):
            yield i < jnp.minimum(head, n), _slot_row_copy(zero_scr, i, xs_ref, start + i, zsem)
        body = jnp.maximum(n - head, 0)
        for rows in pad_chunks:
            off = pl.multiple_of(start + head + (body & ~(2 * rows - 1)), sub)
            dst = xs_ref.at[pl.ds(off, rows), :]
            yield (body & rows) != 0, pltpu.make_async_copy(zero_scr.at[pl.ds(0, rows), :], dst, zsem)

    @pl.when(first)
    def _():
        zero_scr[...] = jnp.zeros(zero_scr.shape, F32)
        for e in range(fill_rows_ref.shape[0]):
            for cond, copy in pad_copies(e):
                pl.when(cond)(copy.start)

    base = (pl.program_id(0) * pl.num_programs(1) + pl.program_id(1)) * (2 * TILE)

    for r in range(TILE):
        for k in range(2):
            _slot_row_copy(h2_ref, r, xs_ref, pos_ref[base + 2 * r + k], sem).start(priority=k)
    for k in range(2):
        _wait_rows(h2_ref, xs_ref.at[pl.ds(0, TILE), :], sem)

    @pl.when(first)
    def _():
        for e in range(fill_rows_ref.shape[0]):
            for cond, copy in pad_copies(e):
                pl.when(cond)(copy.wait)


def _grouped_kernel(tile_e_ref, n_used_ref, xs_ref, wgu_hbm, wd_hbm, ys_ref,
                    wgu_bf, wd_bf, stage_gu, stage_d, sems, *, fj):
    i = pl.program_id(0)
    live = i < n_used_ref[0]
    e = tile_e_ref[i]

    @pl.when(live & ((i == 0) | (e != tile_e_ref[jnp.maximum(i - 1, 0)])))
    def _():
        _stage_weights((fj, e), wgu_hbm, wd_hbm, wgu_bf, wd_bf, stage_gu, stage_d, sems)

    @pl.when(live)
    def _():
        ys_ref[...] = _swiglu_acc(xs_ref[...].astype(BF16), wgu_bf, wd_bf)

    @pl.when(jnp.logical_not(live))
    def _():
        ys_ref[...] = jnp.zeros(ys_ref.shape, F32)


def _combine_kernel(pos_ref, ys_ref, route_ref, x1_ref, g2_ref, lg_ref, lb_ref, o_ref, ybuf, sem):
    base = (pl.program_id(0) * pl.num_programs(1) + pl.program_id(1)) * (2 * TILE)

    for r in range(TILE):
        for k in range(2):
            _slot_row_copy(ys_ref, pos_ref[base + 2 * r + k], ybuf.at[k], r, sem).start(priority=k)
    for k in range(2):
        _wait_rows(ys_ref.at[pl.ds(0, TILE), :], ybuf.at[k], sem)
    route = route_ref[...]
    lane = lax.broadcasted_iota(jnp.int32, route.shape, 1)
    w1 = jnp.sum(jnp.where(lane == R_W1, route, 0.0), axis=-1, keepdims=True)
    w2 = jnp.sum(jnp.where(lane == R_W2, route, 0.0), axis=-1, keepdims=True)
    y = w1 * ybuf[0] + w2 * ybuf[1]
    o_ref[...] = _ln(ALPHA * x1_ref[...] + g2_ref[...] * y) * lg_ref[...] + lb_ref[...]


def _ffn_routed(li, x1, h2, route, counts, mod5, lg, lb, wgu, wd, j0):
    nt = N_TILES - j0
    n_tok = BATCH * nt * TILE
    n_tiles = 2 * n_tok // EXP_TILE + N_EXP
    n_slots = n_tiles * EXP_TILE
    cnt = counts[0, :N_EXP].astype(jnp.int32)
    padded = (cnt + EXP_TILE - 1) // EXP_TILE * EXP_TILE
    ends = jnp.cumsum(padded)
    offs = ends - padded
    n_used = ends[-1:] // EXP_TILE
    tile_start = jnp.minimum(jnp.arange(n_tiles), n_used - 1) * EXP_TILE
    tile_e = jnp.sum((tile_start[:, None] >= ends[None, :]).astype(jnp.int32), axis=1)
    rec = route.reshape(n_tok, LANES)
    experts = rec[:, R_E1:R_E2 + 1].astype(jnp.int32)
    ranks = rec[:, R_RANK1:R_RANK2 + 1].astype(jnp.int32)
    off_of = jnp.sum(jnp.where(experts[..., None] == jnp.arange(N_EXP), offs, 0), axis=-1)
    pos = (off_of + ranks).reshape(2 * n_tok)

    h2_flat = h2.reshape(n_tok, D_MODEL)
    half = EXP_TILE // 2
    tail_start = n_used * EXP_TILE + jnp.arange(2 * N_EXP) * half
    tail_rows = jnp.where(tail_start < n_slots, half, 0)
    tail_start = jnp.minimum(tail_start, n_slots - half)
    any_spec = pl.BlockSpec(memory_space=pl.ANY)
    xs = pl.pallas_call(
        _dispatch_kernel,
        grid_spec=pltpu.PrefetchScalarGridSpec(
            num_scalar_prefetch=3,
            grid=(BATCH, nt),
            in_specs=[pl.BlockSpec((TILE, D_MODEL), lambda b, j, *_: (b * nt + j, 0))],
            out_specs=any_spec,
            scratch_shapes=[pltpu.VMEM((EXP_TILE // 2, D_MODEL), F32),
                            pltpu.SemaphoreType.DMA(()), pltpu.SemaphoreType.DMA(())]),
        out_shape=jax.ShapeDtypeStruct((n_slots, D_MODEL), F32),
        compiler_params=_cparams(("arbitrary", "arbitrary")),
        name=f"moe_dispatch_l{li}",
    )(pos, jnp.concatenate([offs + cnt, tail_start]), jnp.concatenate([padded - cnt, tail_rows]), h2_flat)

    used = lambda i, te, nu: jnp.minimum(i, nu[0] - 1)
    ys = pl.pallas_call(
        functools.partial(_grouped_kernel, fj=li // 2),
        grid_spec=pltpu.PrefetchScalarGridSpec(
            num_scalar_prefetch=2,
            grid=(n_tiles,),
            in_specs=[
                pl.BlockSpec((EXP_TILE, D_MODEL), lambda i, te, nu: (used(i, te, nu), 0)),
                pl.BlockSpec(memory_space=pl.ANY), pl.BlockSpec(memory_space=pl.ANY)],
            out_specs=pl.BlockSpec((EXP_TILE, D_MODEL), lambda i, te, nu: (i, 0)),
            scratch_shapes=_WEIGHT_SCRATCH),
        out_shape=jax.ShapeDtypeStruct((n_slots, D_MODEL), F32),
        compiler_params=_cparams(("arbitrary",), VMEM_LIMIT),
        name=f"moe_grouped_l{li}",
    )(tile_e, n_used, xs, wgu, wd)

    row = pl.BlockSpec((None, 1, D_MODEL), lambda b, j, pos: (li, 0, 0))
    tok = lambda w: pl.BlockSpec((None, TILE, w), lambda b, j, pos: (b, j, 0))
    mod = pl.BlockSpec((None, None, None, 1, D_MODEL),
                       lambda b, j, pos: (li, jnp.where(j + j0 == 0, CTX_MOD_ROW, b), 5, 0, 0))
    return pl.pallas_call(
        _combine_kernel,
        grid_spec=pltpu.PrefetchScalarGridSpec(
            num_scalar_prefetch=1,
            grid=(BATCH, nt),
            in_specs=[any_spec, tok(LANES), tok(D_MODEL), mod, row, row],
            out_specs=tok(D_MODEL),
            scratch_shapes=[pltpu.VMEM((2, TILE, D_MODEL), F32), pltpu.SemaphoreType.DMA(())]),
        out_shape=jax.ShapeDtypeStruct((BATCH, nt * TILE, D_MODEL), F32),
        compiler_params=_cparams(("arbitrary", "arbitrary"), VMEM_LIMIT),
        name=f"moe_combine_l{li}",
    )(pos, ys, route, x1, mod5, lg, lb)


def _rope_tables(dim):
    rows_n = SEQ // GRID_W
    t_row = jnp.repeat(jnp.arange(rows_n, dtype=F32), GRID_W)
    t_col = jnp.tile(jnp.arange(GRID_W, dtype=F32), rows_n)
    quarter = dim // 4
    inv = ROPE_BASE ** (-jnp.arange(quarter, dtype=F32) / quarter)
    ang = jnp.concatenate([t_row[:, None] * inv, t_col[:, None] * inv], axis=-1)
    cos, sin = jnp.cos(ang), jnp.sin(ang)
    reps = LANES // dim
    cos_f = jnp.tile(jnp.concatenate([cos, cos], axis=-1), (1, reps))
    sin_s = jnp.tile(jnp.concatenate([-sin, sin], axis=-1), (1, reps))
    cos_f = jnp.concatenate([jnp.ones((CTX_LEN, LANES), F32), cos_f], axis=0)
    sin_s = jnp.concatenate([jnp.zeros((CTX_LEN, LANES), F32), sin_s], axis=0)
    return cos_f, sin_s


def _head_expand(offset):
    rows = jnp.arange(LANES)[:, None]
    cols = jnp.arange(D_INNER)[None, :]
    return (rows == cols // HD_S + offset).astype(BF16)


def kernel(x, c, ctx, c_ctx, w_ada, b_ada, w_in, attn_sink, conv_w, conv_b, dt_bias, a_log, d_skip,
           ssm_norm_w, lam_q, lam_k, diff_norm_w, w_out, ln1_g, ln1_b, ln2_g, ln2_b, ffn_w_gu,
           ffn_w_down, router_w, router_b, exp_w_gu, exp_w_down):
    x_all = jnp.concatenate([ctx, x], axis=1)
    c_all = jnp.concatenate(
        [c, c_ctx[None, :], jnp.zeros((MOD_ROWS - BATCH - 1, D_MODEL), F32)], axis=0)
    mod5 = _ada_all(c_all, w_ada, b_ada).reshape(DEPTH, MOD_ROWS, 6, 1, D_MODEL)
    ropes = _rope_tables(HD_A) + _rope_tables(HD_C)
    ef, eb = _head_expand(0), _head_expand(N_HS)

    w_q = (w_in[:, :, :384].reshape(DEPTH, D_MODEL, N_KVA, 3, HD_A).transpose(0, 1, 3, 2, 4)
           .reshape(DEPTH, D_MODEL, 384))
    w_re = jnp.concatenate(
        [w_q, w_in[:, :, 384:1932], jnp.zeros((DEPTH, D_MODEL, D_IN_PAD - 2700), F32), w_in[:, :, 1932:]],
        axis=2).astype(BF16)
    wo_a = (w_out[:, :384].reshape(DEPTH, N_KVA, 3, HD_A, D_MODEL).transpose(0, 2, 1, 3, 4)
            .reshape(DEPTH, 384, D_MODEL))
    wo_re = jnp.concatenate([wo_a, w_out[:, 384:]], axis=1).astype(BF16)
    nw_c = jnp.tile(diff_norm_w, (1, N_HC))[:, None, :]
    cw = jnp.pad(conv_w, ((0, 0), (0, 8 - CONV_K), (0, 0)))
    pad12 = lambda a: jnp.pad(a.reshape(DEPTH, 1, 2 * N_HS), ((0, 0), (0, 0), (0, LANES - 2 * N_HS)))
    dtb, alog = pad12(dt_bias), pad12(a_log)
    dsk = jnp.repeat(d_skip, HD_S, axis=1)[:, None, :]
    lg1, lb1, lg2, lb2 = (a[:, None, :] for a in (ln1_g, ln1_b, ln2_g, ln2_b))
    wr_f = jnp.pad(router_w, ((0, 0), (0, 0), (0, LANES - N_EXP)))
    wr_hi = lax.bitcast_convert_type(lax.bitcast_convert_type(wr_f, jnp.uint32) & jnp.uint32(0xFFFF0000), F32)
    wr = jnp.concatenate([wr_hi.astype(BF16), (wr_f - wr_hi).astype(BF16)], axis=2)
    br = jnp.pad(router_b, ((0, 0), (0, LANES - N_EXP)))[:, None, :]

    for li in range(DEPTH):
        ctx_out = li < DEPTH - 1
        j0 = 0 if ctx_out else CTX_LEN // TILE
        lam_init = 0.8 - 0.6 * math.exp(-0.3 * li)
        qa, ka, va, z, xbc, dt, qc, kc, vc = _inproj(li, x_all, mod5, w_re, ropes)
        ya_lat, ya_ctx = _attn_a(li, qa, ka, va, attn_sink[li], ctx_out)
        yd_lat, yd_ctx = _attn_c(li, qc, kc, vc, lam_q, lam_k, nw_c, lam_init, ctx_out)
        yb = _ssd(li, xbc, z, dt, cw, conv_b[:, None, :], dtb, alog, dsk, ssm_norm_w[:, None, :], ef, eb)
        moe = li % 2 == 1
        res = _outproj(li, ya_lat, ya_ctx, yb, yd_lat, yd_ctx, x_all, mod5, lg1, lb1, wo_re, moe, wr, br, j0)
        if moe:
            x1, h2, route, counts = res
            x_all = _ffn_routed(li, x1, h2, route, counts, mod5, lg2, lb2, exp_w_gu, exp_w_down, j0)
        else:
            x1, h2 = res
            x_all = _ffn_dense(li, x1, h2, mod5, lg2, lb2, ffn_w_gu, ffn_w_down)
    return x_all
```

```python
import functools
import math

import jax
import jax.numpy as jnp
from jax import lax
from jax.experimental import pallas as pl
from jax.experimental.pallas import tpu as pltpu

F32 = jnp.float32
BF16 = jnp.bfloat16

D_MODEL = 1024
BATCH = 8
SEQ = 2048
DEPTH = 4
GRID_W = 64
CTX_LEN = 256
BLK = 128
WINDOW = 128
ROPE_BASE = 10000.0
EPS = 1e-6
LOG2E = 1.4426950408889634
HD_A = 64
N_HA = 6
N_KVA = 2
D_INNER = 384
HD_S = 64
N_HS = 6
D_STATE = 128
CONV_K = 5
CONV_DIM = 896
CHUNK = 128
N_HC = 4
HD_C = 32
D_C = 256
D_FF = 2816
N_EXP = 8
ALPHA = (2 * DEPTH) ** 0.25

LANES = 128
TILE = 256
WIDE_TILE = 3 * TILE
ROWS = CTX_LEN + SEQ
N_TILES = ROWS // TILE
N_CHUNKS = ROWS // CHUNK
N_CTX_CHUNKS = CTX_LEN // CHUNK
FF_CHUNK = 256
N_FF_CHUNKS = D_FF // FF_CHUNK
D_IN_PAD = 2816
MOD_ROWS = 16
CTX_MOD_ROW = BATCH
VMEM_LIMIT = 56 * 1024 * 1024


def _cparams(sem, vmem=None):
    return pltpu.CompilerParams(dimension_semantics=sem, vmem_limit_bytes=vmem)


def _sigmoid(x):
    return 1.0 / (1.0 + jnp.exp(-x))


def _silu(x):
    return x * _sigmoid(x)


def _ln(x):
    mu = jnp.mean(x, axis=-1, keepdims=True)
    xc = x - mu
    var = jnp.mean(xc * xc, axis=-1, keepdims=True)
    return xc * lax.rsqrt(var + EPS)


def _dot(a, b, precision=None):
    return jnp.dot(a, b, preferred_element_type=F32, precision=precision)


def _split_bf16(x, parts):
    out = []
    for _ in range(parts - 1):
        hi = x.astype(BF16)
        out.append(hi)
        x = x - hi.astype(F32)
    out.append(x.astype(BF16))
    return out


def _dot_nt(a, b):
    return lax.dot_general(a, b, (((1,), (1,)), ((), ())), preferred_element_type=F32)


def _ada_kernel(c_ref, w_ref, b_ref, o_ref):
    act = _silu(c_ref[...])
    o_ref[...] = _dot(act.astype(BF16), w_ref[...].astype(BF16)) + b_ref[...]


def _ada_all(c_all, w_ada, b_ada):
    tn = 1536
    return pl.pallas_call(
        _ada_kernel,
        grid=(DEPTH, 6 * D_MODEL // tn),
        in_specs=[
            pl.BlockSpec((MOD_ROWS, D_MODEL), lambda l, j: (0, 0)),
            pl.BlockSpec((None, D_MODEL, tn), lambda l, j: (l, 0, j)),
            pl.BlockSpec((None, 1, tn), lambda l, j: (l, 0, j)),
        ],
        out_specs=pl.BlockSpec((None, MOD_ROWS, tn), lambda l, j: (l, 0, j)),
        out_shape=jax.ShapeDtypeStruct((DEPTH, MOD_ROWS, 6 * D_MODEL), F32),
        compiler_params=_cparams(("parallel", "parallel")),
        name="ada_ln",
    )(c_all, w_ada, b_ada.reshape(DEPTH, 1, 6 * D_MODEL))


def _mod_spec(li, piece, j0):
    return pl.BlockSpec(
        (None, None, None, 1, D_MODEL),
        lambda b, j: (li, jnp.where(j + j0 == 0, CTX_MOD_ROW, b), piece, 0, 0))


def _tok_spec(width, j0=0):
    return pl.BlockSpec((None, TILE, width), lambda b, j: (b, j + j0, 0))


def _full_spec(shape):
    n = len(shape)
    return pl.BlockSpec(shape, lambda *_: (0,) * n)


def _layer_spec(tail, li):
    n = len(tail)
    return pl.BlockSpec((None,) + tuple(tail), lambda *_: (li,) + (0,) * n)


def _rope(v, cos, sin_signed, half):
    lane = lax.broadcasted_iota(jnp.int32, (v.shape[0], LANES), 1)
    first = (lane % (2 * half)) < half
    outs = []
    for s in range(v.shape[1] // LANES):
        blk = v[:, s * LANES:(s + 1) * LANES]
        swapped = jnp.where(first, pltpu.roll(blk, LANES - half, 1), pltpu.roll(blk, half, 1))
        outs.append(blk * cos + swapped * sin_signed)
    return jnp.concatenate(outs, axis=1)


def _inproj_kernel(x_ref, sh_head_ref, sc_head_ref, sh_ref, sc_ref, w_ref,
                   cosa_ref, sina_ref, cosc_ref, sinc_ref,
                   qa_ref, ka_ref, va_ref, z_ref, xbc_ref, dt_ref, qc_ref, kc_ref, vc_ref):
    xn = _ln(x_ref[...])
    hb = jnp.concatenate([xn[:TILE] * (1.0 + sc_head_ref[...]) + sh_head_ref[...],
                          xn[TILE:] * (1.0 + sc_ref[...]) + sh_ref[...]], axis=0).astype(BF16)

    def proj(lo, hi):
        return _dot(hb, w_ref[:, lo:hi])

    cosa, sina = cosa_ref[...], sina_ref[...]
    cosc, sinc = cosc_ref[...], sinc_ref[...]
    qk_a = proj(0, 512)
    qa_ref[...] = (_rope(qk_a[:, :384], cosa, sina, HD_A // 2) * (HD_A ** -0.5 * LOG2E)).astype(BF16)
    ka_ref[...] = _rope(qk_a[:, 384:], cosa, sina, HD_A // 2).astype(BF16)
    vz = proj(512, 1024)
    va_ref[...] = vz[:, :LANES].astype(BF16)
    z_ref[...] = vz[:, LANES:]
    xbc_dt = proj(1024, 2048)
    xbc_ref[...] = xbc_dt[:, :CONV_DIM]
    dt_ref[...] = xbc_dt[:, CONV_DIM:]
    qc_ref[...] = (_rope(proj(2048, 2304), cosc, sinc, HD_C // 2) * (HD_C ** -0.5 * LOG2E)).astype(BF16)
    kc_ref[...] = _rope(proj(2304, 2560), cosc, sinc, HD_C // 2).astype(BF16)
    vc = proj(2560, 2816)
    lane = lax.broadcasted_iota(jnp.int32, vc.shape, 1)
    for h in range(N_HC):
        in_head = jnp.where(lane >= h * 2 * HD_C, lane - h * 2 * HD_C, 2 * HD_C) < 2 * HD_C
        vc_ref[h] = jnp.where(in_head, vc, 1.0).astype(BF16)


def _inproj(li, x_all, mod5, w_re, ropes):
    widths = (384, 128, 128, 384, CONV_DIM, LANES, D_C, D_C, D_C)
    dtypes = (BF16, BF16, BF16, F32, F32, F32, BF16, BF16, BF16)
    rope_spec = pl.BlockSpec((WIDE_TILE, LANES), lambda b, j: (j, 0))
    tok = lambda w: pl.BlockSpec((None, WIDE_TILE, w), lambda b, j: (b, j, 0))
    batch_mod = lambda piece: pl.BlockSpec(
        (None, None, None, 1, D_MODEL), lambda b, j: (li, b, piece, 0, 0))
    return pl.pallas_call(
        _inproj_kernel,
        grid=(BATCH, ROWS // WIDE_TILE),
        in_specs=[tok(D_MODEL), _mod_spec(li, 0, 0), _mod_spec(li, 1, 0), batch_mod(0), batch_mod(1),
                  _layer_spec((D_MODEL, D_IN_PAD), li), rope_spec, rope_spec, rope_spec, rope_spec],
        out_specs=[tok(w) for w in widths[:-1]]
        + [pl.BlockSpec((None, N_HC, WIDE_TILE, D_C), lambda b, j: (b, 0, j, 0))],
        out_shape=[jax.ShapeDtypeStruct((BATCH, ROWS, w), d) for w, d in zip(widths[:-1], dtypes)]
        + [jax.ShapeDtypeStruct((BATCH, N_HC, ROWS, D_C), BF16)],
        compiler_params=_cparams(("parallel", "parallel"), VMEM_LIMIT),
        name=f"inproj_l{li}",
    )(x_all, mod5, mod5, mod5, mod5, w_re, *ropes)


def _attn_a_blocks(sink_ref, blocks):
    tq = blocks[0][1].shape[0]
    lane_lo = lax.broadcasted_iota(jnp.int32, (tq, LANES), 1) < HD_A
    scores = []
    for _, q, k, _ in blocks:
        q = q.astype(F32)
        slabs = [q[:, s * LANES:(s + 1) * LANES] for s in range(3)]
        qst = jnp.concatenate([jnp.where(lane_lo, blk, 0.0) for blk in slabs]
                              + [jnp.where(lane_lo, 0.0, blk) for blk in slabs], axis=0).astype(BF16)
        scores.append(_dot_nt(qst, k))
    probs = [[] for _ in blocks]
    sink_terms = [[] for _ in blocks]
    for rb in range(N_HA):
        sink = sink_ref[rb] * LOG2E
        for i, (bias, _, _, _) in enumerate(blocks):
            sh = scores[i][rb * tq:(rb + 1) * tq]
            if bias is not None:
                sh = jnp.concatenate([sh[:, :3 * BLK] + bias, sh[:, 3 * BLK:]], axis=1)
            m = jnp.maximum(jnp.max(sh, axis=-1, keepdims=True), sink)
            probs[i].append(jnp.exp2(sh - m).astype(BF16))
            sink_terms[i].append(jnp.exp2(sink - m))
    results = []
    for i, (_, _, _, v) in enumerate(blocks):
        vf = v.astype(F32)
        key_lo = lax.broadcasted_iota(jnp.int32, vf.shape, 1) < HD_A
        v_heads = (jnp.where(key_lo, vf, 1.0).astype(BF16), jnp.where(key_lo, 1.0, vf).astype(BF16))
        outs = []
        for g in range(N_KVA):
            o = _dot(jnp.concatenate(probs[i][3 * g:3 * g + 3], axis=0), v_heads[g])
            lane3 = lax.broadcasted_iota(jnp.int32, o.shape, 1)
            den = jnp.sum(jnp.where(lane3 == (1 - g) * HD_A, o, 0.0), axis=-1, keepdims=True)
            den = den + jnp.concatenate(sink_terms[i][3 * g:3 * g + 3], axis=0)
            outs.append(o * (1.0 / den))
        results.append(jnp.concatenate(
            [jnp.where(lane_lo, outs[0][s * tq:(s + 1) * tq], outs[1][s * tq:(s + 1) * tq])
             for s in range(3)], axis=1).astype(BF16))
    return results


def _attn_a_kernel(*refs, local):
    if not local:
        sink_ref, q_ref, kx_ref, vx_ref, o_ref = refs
        o_ref[...] = _attn_a_blocks(sink_ref, [(None, q_ref[...], kx_ref[...], vx_ref[...])])[0]
        return
    sink_ref, bias_a, bias_b, q_ref, k_refs, v_refs, o_ref = (*refs[:4], refs[4:9], refs[9:14], refs[14])
    k_blocks = [r[...] for r in k_refs]
    v_blocks = [r[...] for r in v_refs]
    blocks = []
    for half, bias_ref in enumerate((bias_a, bias_b)):
        k = jnp.concatenate(k_blocks[half:half + 3] + k_blocks[4:], axis=0)
        v = jnp.concatenate(v_blocks[half:half + 3] + v_blocks[4:], axis=0)
        blocks.append((bias_ref[...], q_ref[half * BLK:(half + 1) * BLK, :], k, v))
    for half, out in enumerate(_attn_a_blocks(sink_ref, blocks)):
        o_ref[half * BLK:(half + 1) * BLK, :] = out


def _band_bias():
    i = jnp.arange(BLK)[:, None]
    j = jnp.arange(3 * BLK)[None, :] - BLK
    in_band = jnp.abs(i - j) <= WINDOW
    first = in_band & (j >= 0)
    last = in_band & (j < BLK)
    return jnp.where(jnp.stack([first, in_band, last]), 0.0, -jnp.inf).astype(F32)


def _attn_a(li, qa, ka, va, sink, with_ctx):
    nb = SEQ // BLK
    off = CTX_LEN // BLK
    smem = pl.BlockSpec(memory_space=pltpu.SMEM)
    bias_a = pl.BlockSpec((None, BLK, 3 * BLK), lambda b, n: (jnp.where(n == 0, 0, 1), 0, 0))
    bias_b = pl.BlockSpec((None, BLK, 3 * BLK), lambda b, n: (jnp.where(n == nb // 2 - 1, 2, 1), 0, 0))

    def kv(shift):
        return pl.BlockSpec(
            (None, BLK, LANES), lambda b, n: (b, off + jnp.clip(2 * n + shift, 0, nb - 1), 0))

    ctx_kv = pl.BlockSpec((None, CTX_LEN, LANES), lambda b, n: (b, 0, 0))
    q_spec = pl.BlockSpec((None, 2 * BLK, 384), lambda b, n: (b, CTX_LEN // (2 * BLK) + n, 0))
    kvs = [kv(-1), kv(0), kv(1), kv(2), ctx_kv]
    y_lat = pl.pallas_call(
        functools.partial(_attn_a_kernel, local=True),
        grid=(BATCH, nb // 2),
        in_specs=[smem, bias_a, bias_b, q_spec] + kvs + kvs,
        out_specs=pl.BlockSpec((None, 2 * BLK, 384), lambda b, n: (b, n, 0)),
        out_shape=jax.ShapeDtypeStruct((BATCH, SEQ, 384), BF16),
        compiler_params=_cparams(("parallel", "parallel")),
        name=f"attn_a_l{li}",
    )(sink, _band_bias(), _band_bias(), qa, *([ka] * 5), *([va] * 5))
    if not with_ctx:
        return y_lat, None
    ctx_spec384 = pl.BlockSpec((None, CTX_LEN, 384), lambda b: (b, 0, 0))
    ctx_spec128 = pl.BlockSpec((None, CTX_LEN, LANES), lambda b: (b, 0, 0))
    y_ctx = pl.pallas_call(
        functools.partial(_attn_a_kernel, local=False),
        grid=(BATCH,),
        in_specs=[smem, ctx_spec384, ctx_spec128, ctx_spec128],
        out_specs=ctx_spec384,
        out_shape=jax.ShapeDtypeStruct((BATCH, CTX_LEN, 384), BF16),
        compiler_params=_cparams(("parallel",)),
        name=f"attn_a_ctx_l{li}",
    )(sink, qa, ka, va)
    return y_lat, y_ctx


def _attn_c_kernel(lq_ref, lk_ref, nw_ref, *refs, lam_init):
    q_refs, (k_ref, v_ref, o_ref) = refs[:-3], refs[-3:]
    tq = o_ref.shape[0]
    e = jnp.exp(jnp.sum(lq_ref[...] * lk_ref[...], axis=-1, keepdims=True))
    lam = e[0:1] - e[1:2] + lam_init
    q = jnp.concatenate([r[...] for r in q_refs], axis=0).astype(F32)
    k = k_ref[...]
    lane = lax.broadcasted_iota(jnp.int32, (tq, D_C), 1)
    lane2 = lax.broadcasted_iota(jnp.int32, (2 * tq, D_C), 1)
    acc = jnp.zeros((tq, D_C), F32)
    for h in range(N_HC):
        base = h * 2 * HD_C
        in0 = jnp.where(lane >= base, lane - base, 2 * HD_C) < HD_C
        in1 = jnp.where(lane >= base + HD_C, lane - base - HD_C, HD_C) < HD_C
        qst = jnp.concatenate([jnp.where(in0, q, 0.0), jnp.where(in1, q, 0.0)], axis=0).astype(BF16)
        s = _dot_nt(qst, k)
        p = jnp.exp2(s - jnp.max(s, axis=-1, keepdims=True)).astype(BF16)
        pv = _dot(p, v_ref[h])
        den = jnp.sum(jnp.where(lane2 == (base + 2 * HD_C) % D_C, pv, 0.0), axis=-1, keepdims=True)
        pv = pv * (1.0 / den)
        o = pv[:tq] - lam * pv[tq:]
        inh = jnp.where(lane >= base, lane - base, 2 * HD_C) < 2 * HD_C
        oh = jnp.where(inh, o, 0.0)
        ms = jnp.sum(oh * oh, axis=-1, keepdims=True) * (1.0 / (2 * HD_C))
        acc = acc + oh * lax.rsqrt(ms + EPS)
    o_ref[...] = (acc * nw_ref[...] * (1.0 - lam_init)).astype(BF16)


def _attn_c(li, qc, kc, vc, lam_q, lam_k, nw_row, lam_init, with_ctx):
    parts = 2
    tq = parts * TILE
    off = CTX_LEN // TILE
    q_specs = [pl.BlockSpec((None, TILE, D_C), lambda b, n, i=i: (b, off + parts * n + i, 0))
               for i in range(parts)]
    small = [_layer_spec((2, HD_C), li), _layer_spec((2, HD_C), li), _layer_spec((1, D_C), li)]
    kern = functools.partial(_attn_c_kernel, lam_init=lam_init)
    all_kv = pl.BlockSpec((None, ROWS, D_C), lambda b, n: (b, 0, 0))
    all_v = pl.BlockSpec((None, N_HC, ROWS, D_C), lambda b, n: (b, 0, 0, 0))
    y_lat = pl.pallas_call(
        kern,
        grid=(BATCH, SEQ // tq),
        in_specs=small + q_specs + [all_kv, all_v],
        out_specs=pl.BlockSpec((None, tq, D_C), lambda b, n: (b, n, 0)),
        out_shape=jax.ShapeDtypeStruct((BATCH, SEQ, D_C), BF16),
        compiler_params=_cparams(("parallel", "parallel"), VMEM_LIMIT),
        name=f"attn_c_l{li}",
    )(lam_q, lam_k, nw_row, *([qc] * parts), kc, vc)
    if not with_ctx:
        return y_lat, None
    ctx_spec = pl.BlockSpec((None, CTX_LEN, D_C), lambda b: (b, 0, 0))
    y_ctx = pl.pallas_call(
        kern,
        grid=(BATCH,),
        in_specs=small + [ctx_spec, ctx_spec,
                           pl.BlockSpec((None, N_HC, CTX_LEN, D_C), lambda b: (b, 0, 0, 0))],
        out_specs=ctx_spec,
        out_shape=jax.ShapeDtypeStruct((BATCH, CTX_LEN, D_C), BF16),
        compiler_params=_cparams(("parallel",)),
        name=f"attn_c_ctx_l{li}",
    )(lam_q, lam_k, nw_row, qc, kc, vc)
    return y_lat, y_ctx


def _ssd_kernel(xbc_ref, z_ref, dt_ref, cw_ref, cb_ref, dtb_ref, alog_ref, dsk_ref, nw_ref,
                ef_ref, eb_ref, y_ref, u_scr, yacc_scr, st_scr):
    cw = cw_ref[...]
    cb = cb_ref[...]
    dsk = dsk_ref[...]
    pad = 8
    for c in range(N_CHUNKS):
        r0 = CHUNK * c
        zeros = jnp.zeros((pad, CONV_DIM), F32)
        if c in (0, N_CTX_CHUNKS):
            blk = jnp.concatenate([zeros, xbc_ref[r0:r0 + CHUNK + pad, :]], axis=0)
        elif c in (N_CTX_CHUNKS - 1, N_CHUNKS - 1):
            blk = jnp.concatenate([xbc_ref[r0 - pad:r0 + CHUNK, :], zeros], axis=0)
        else:
            blk = xbc_ref[r0 - pad:r0 + CHUNK + pad, :]
        acc = None
        for j in range(CONV_K):
            sh = (CONV_K // 2 - j) % (CHUNK + 2 * pad)
            rolled = blk if sh == 0 else pltpu.roll(blk, sh, 0)
            term = rolled[pad:pad + CHUNK, :] * cw[j:j + 1, :]
            acc = term if acc is None else acc + term
        u = _silu(acc + cb)
        u_scr[r0:r0 + CHUNK, :] = u
        yacc_scr[r0:r0 + CHUNK, :] = u[:, :D_INNER] * dsk

    st_scr[...] = jnp.zeros(st_scr.shape, F32)
    ri = lax.broadcasted_iota(jnp.int32, (CHUNK, CHUNK), 0)
    ci = lax.broadcasted_iota(jnp.int32, (CHUNK, CHUNK), 1)
    lower = ri >= ci
    upper = ri <= ci
    tri = (jnp.where(lower, 1.0, 0.0).astype(BF16), jnp.where(upper, 1.0, 0.0).astype(BF16))
    a_row = -jnp.exp(alog_ref[...])
    dtb = dtb_ref[...]
    lane_lo = ci < HD_S

    def chunk_dir(c, d):
        r0 = pl.multiple_of(c * CHUNK, CHUNK)
        u = u_scr[pl.ds(r0, CHUNK), :]
        xs = u[:, :D_INNER]
        bm = u[:, D_INNER:D_INNER + 2 * D_STATE]
        cm = u[:, D_INNER + 2 * D_STATE:]
        vv = dt_ref[pl.ds(r0, CHUNK), :] + dtb
        dt = jnp.maximum(vv, 0.0) + jnp.log(1.0 + jnp.exp(-jnp.abs(vv)))
        a = dt * a_row
        cum = sum(_dot(tri[d], piece) for piece in _split_bf16(a, 3))
        cum_t = cum.T
        tot = cum[CHUNK - 1:CHUNK, :] if d == 0 else cum[0:1, :]
        expand = ef_ref[...] if d == 0 else eb_ref[...]
        per_head = jnp.concatenate([dt, jnp.exp(cum), jnp.exp(tot - cum)], axis=0)
        per_lane = sum(_dot(piece, expand) for piece in _split_bf16(per_head, 2))
        dt_x = per_lane[:CHUNK]
        dec_out_x = per_lane[CHUNK:2 * CHUNK]
        dec_st_x = per_lane[2 * CHUNK:]
        etot_x = dec_out_x[CHUNK - 1:CHUNK, :] if d == 0 else dec_out_x[0:1, :]
        xdt = xs * dt_x
        xdt_b = xdt.astype(BF16)
        xst_b = (xdt * dec_st_x).astype(BF16)
        cb_ = [cm[:, g * D_STATE:(g + 1) * D_STATE].astype(BF16) for g in range(2)]
        bb_ = [bm[:, g * D_STATE:(g + 1) * D_STATE].astype(BF16) for g in range(2)]
        bt_ = [bm[:, g * D_STATE:(g + 1) * D_STATE].T.astype(BF16) for g in range(2)]
        cbm = [_dot_nt(cb_[g], bb_[g]) for g in range(2)]
        mask = lower if d == 0 else upper
        for j in range(3):
            sl = slice(j * LANES, (j + 1) * LANES)
            parts = []
            for h in (2 * j, 2 * j + 1):
                hl = h + N_HS * d
                seg = jnp.where(mask, cum[:, hl:hl + 1] - cum_t[hl:hl + 1, :], -jnp.inf)
                sc = (cbm[h // 3] * jnp.exp(seg)).astype(BF16)
                parts.append(_dot(sc, xdt_b[:, sl]))
            y_diag = jnp.where(lane_lo, parts[0], parts[1])
            st = st_scr[d, j]
            st_b = st.astype(BF16)
            g0, g1 = (2 * j) // 3, (2 * j + 1) // 3
            if g0 == g1:
                y_off = _dot(cb_[g0], st_b)
                upd = _dot(bt_[g0], xst_b[:, sl])
            else:
                y_off = jnp.where(lane_lo, _dot(cb_[g0], st_b), _dot(cb_[g1], st_b))
                upd = jnp.where(lane_lo, _dot(bt_[g0], xst_b[:, sl]), _dot(bt_[g1], xst_b[:, sl]))
            yacc_scr[pl.ds(r0, CHUNK), sl] += y_diag + y_off * dec_out_x[:, sl]
            st_scr[d, j] = etot_x[:, sl] * st + upd

    def body(i, carry):
        chunk_dir(i, 0)
        chunk_dir(jnp.where(i < N_CTX_CHUNKS, N_CTX_CHUNKS - 1 - i, N_CHUNKS + N_CTX_CHUNKS - 1 - i), 1)
        return carry

    lax.fori_loop(0, N_CHUNKS, body, 0)

    nw = nw_ref[...]
    for t in range(N_TILES):
        rows = slice(t * TILE, (t + 1) * TILE)
        y = yacc_scr[rows, :] * _silu(z_ref[rows, :])
        ms = jnp.mean(y * y, axis=-1, keepdims=True)
        y_ref[rows, :] = (y * lax.rsqrt(ms + EPS) * nw).astype(BF16)


def _ssd(li, xbc, z, dt, cw, cb, dtb, alog, dsk, nw, ef, eb):
    def seq(width):
        return pl.BlockSpec((None, ROWS, width), lambda b: (b, 0, 0))

    return pl.pallas_call(
        _ssd_kernel,
        grid=(BATCH,),
        in_specs=[seq(CONV_DIM), seq(D_INNER), seq(LANES),
                  _layer_spec((8, CONV_DIM), li), _layer_spec((1, CONV_DIM), li),
                  _layer_spec((1, LANES), li), _layer_spec((1, LANES), li),
                  _layer_spec((1, D_INNER), li), _layer_spec((1, D_INNER), li),
                  _full_spec((LANES, D_INNER)), _full_spec((LANES, D_INNER))],
        out_specs=seq(D_INNER),
        out_shape=jax.ShapeDtypeStruct((BATCH, ROWS, D_INNER), BF16),
        scratch_shapes=[pltpu.VMEM((ROWS, CONV_DIM), F32), pltpu.VMEM((ROWS, D_INNER), F32),
                        pltpu.VMEM((2, 3, D_STATE, LANES), F32)],
        compiler_params=_cparams(("parallel",), VMEM_LIMIT),
        name=f"ssd_l{li}",
    )(xbc, z, dt, cw, cb, dtb, alog, dsk, nw, ef, eb)


def _top2_gates(logits):
    lane = lax.broadcasted_iota(jnp.int32, logits.shape, 1).astype(F32)
    neg = -jnp.inf
    lg = jnp.where(lane < N_EXP, logits, neg)
    m1 = jnp.max(lg, axis=-1, keepdims=True)
    i1 = jnp.min(jnp.where(lg == m1, lane, float(LANES)), axis=-1, keepdims=True)
    lg2 = jnp.where(lane == i1, neg, lg)
    m2 = jnp.max(lg2, axis=-1, keepdims=True)
    i2 = jnp.min(jnp.where(lg2 == m2, lane, float(LANES)), axis=-1, keepdims=True)
    e2 = jnp.exp(m2 - m1)
    return lane, i1, i2, 1.0 / (1.0 + e2), e2 / (1.0 + e2)


R_E1, R_E2, R_W1, R_W2, R_RANK1, R_RANK2 = range(6)


def _route_record(logits, cnt_scr):
    lane, i1, i2, w1, w2 = _top2_gates(logits)
    rows = logits.shape[0]
    hit1, hit2 = lane == i1, lane == i2
    onehot = jnp.where(hit1, 1.0, 0.0) + jnp.where(hit2, 1.0, 0.0)
    ri = lax.broadcasted_iota(jnp.int32, (rows, rows), 0)
    ci = lax.broadcasted_iota(jnp.int32, (rows, rows), 1)
    earlier = jnp.where(ri > ci, 1.0, 0.0).astype(BF16)
    before = _dot(earlier, onehot.astype(BF16)) + cnt_scr[...]
    rank1 = jnp.sum(jnp.where(hit1, before, 0.0), axis=-1, keepdims=True)
    rank2 = jnp.sum(jnp.where(hit2, before, 0.0), axis=-1, keepdims=True)
    cnt_scr[...] += jnp.sum(onehot, axis=0, keepdims=True)
    rec = jnp.zeros(logits.shape, F32)
    for idx, val in ((R_E1, i1), (R_E2, i2), (R_W1, w1), (R_W2, w2), (R_RANK1, rank1), (R_RANK2, rank2)):
        rec = jnp.where(lane == float(idx), val, rec)
    return rec


def _outproj_kernel(*refs, moe, ctx_split):
    if ctx_split:
        ya_l, ya_c, yb_ref, yd_l, yd_c = refs[:5]
        refs = refs[5:]
    else:
        ya_l, yb_ref, yd_l = refs[:3]
        refs = refs[3:]
    x_ref, g1_ref, lg_ref, lb_ref, sh_ref, sc_ref, wo_ref = refs[:7]
    refs = refs[7:]
    if moe:
        wr_ref, br_ref, x1_ref, h2_ref, route_ref, counts_ref, cnt_scr = refs

        @pl.when((pl.program_id(0) == 0) & (pl.program_id(1) == 0))
        def _():
            cnt_scr[...] = jnp.zeros(cnt_scr.shape, F32)
    else:
        x1_ref, h2_ref = refs
    if ctx_split:
        is_ctx = pl.program_id(1) == 0
        ya = jnp.where(is_ctx, ya_c[...], ya_l[...])
        yd = jnp.where(is_ctx, yd_c[...], yd_l[...])
    else:
        ya, yd = ya_l[...], yd_l[...]
    mix = (_dot(ya, wo_ref[0:384, :]) + _dot(yb_ref[...], wo_ref[384:768, :])
           + _dot(yd, wo_ref[768:1024, :]))
    x1 = _ln(ALPHA * x_ref[...] + g1_ref[...] * mix) * lg_ref[...] + lb_ref[...]
    x1_ref[...] = x1
    h2 = _ln(x1) * (1.0 + sc_ref[...]) + sh_ref[...]
    h2_ref[...] = h2.astype(h2_ref.dtype)
    if moe:
        h_hi, h_mid = _split_bf16(h2, 2)
        both = _dot(h_hi, wr_ref[...])
        logits = (both[:, :LANES] + both[:, LANES:] + _dot(h_mid, wr_ref[:, :LANES])) + br_ref[...]
        route_ref[...] = _route_record(logits, cnt_scr)
        counts_ref[...] = cnt_scr[...]


def _outproj(li, ya_lat, ya_ctx, yb, yd_lat, yd_ctx, x_all, mod5, lg, lb, wo, moe, wr, br, j0):
    ctx_split = j0 == 0
    nt = N_TILES - j0
    off = CTX_LEN // TILE
    lat = lambda w: pl.BlockSpec((None, TILE, w), lambda b, j: (b, jnp.maximum(j + j0 - off, 0), 0))
    ctx = lambda w: pl.BlockSpec((None, TILE, w), lambda b, j: (b, 0, 0))
    row = _layer_spec((1, D_MODEL), li)
    if ctx_split:
        ins = [ya_lat, ya_ctx, yb, yd_lat, yd_ctx]
        specs = [lat(384), ctx(384), _tok_spec(D_INNER, j0), lat(D_C), ctx(D_C)]
    else:
        ins = [ya_lat, yb, yd_lat]
        specs = [lat(384), _tok_spec(D_INNER, j0), lat(D_C)]
    ins += [x_all, mod5, lg, lb, mod5, mod5, wo]
    specs += [_tok_spec(D_MODEL, j0), _mod_spec(li, 2, j0), row, row,
              _mod_spec(li, 3, j0), _mod_spec(li, 4, j0), _layer_spec((D_MODEL, D_MODEL), li)]
    outs = [jax.ShapeDtypeStruct((BATCH, nt * TILE, D_MODEL), F32),
            jax.ShapeDtypeStruct((BATCH, nt * TILE, D_MODEL), F32 if moe else BF16)]
    out_specs = [_tok_spec(D_MODEL), _tok_spec(D_MODEL)]
    scratch = []
    if moe:
        ins += [wr, br]
        specs += [_layer_spec((D_MODEL, 2 * LANES), li // 2), _layer_spec((1, LANES), li // 2)]
        outs += [jax.ShapeDtypeStruct((BATCH, nt * TILE, LANES), F32),
                 jax.ShapeDtypeStruct((1, LANES), F32)]
        out_specs += [_tok_spec(LANES), _full_spec((1, LANES))]
        scratch = [pltpu.VMEM((1, LANES), F32)]
    sem = ("arbitrary", "arbitrary") if moe else ("parallel", "parallel")
    return pl.pallas_call(
        functools.partial(_outproj_kernel, moe=moe, ctx_split=ctx_split),
        grid=(BATCH, nt),
        in_specs=specs,
        out_specs=out_specs,
        out_shape=outs,
        scratch_shapes=scratch,
        compiler_params=_cparams(sem, VMEM_LIMIT),
        name=f"outproj_l{li}",
    )(*ins)


def _swiglu_acc(h2, wgu_ref, wd_ref):
    acc = jnp.zeros((h2.shape[0], D_MODEL), F32)
    for c in range(N_FF_CHUNKS):
        cols = slice(c * FF_CHUNK, (c + 1) * FF_CHUNK)
        up_cols = slice(D_FF + c * FF_CHUNK, D_FF + (c + 1) * FF_CHUNK)
        act = (_silu(_dot(h2, wgu_ref[:, cols])) * _dot(h2, wgu_ref[:, up_cols])).astype(BF16)
        acc = acc + _dot(act, wd_ref[cols, :])
    return acc


W_STAGE = 256
GU_STAGE = 128


def _stage_weights(lead, wgu_hbm, wd_hbm, wgu_bf, wd_bf, stage_gu, stage_d, sems):
    def gu_src(c):
        return wgu_hbm.at[lead + (pl.ds(c * GU_STAGE, GU_STAGE), slice(None))]

    def d_src(c):
        return wd_hbm.at[lead + (pl.ds(c * W_STAGE, W_STAGE), slice(None))]

    n_gu, n_d = D_MODEL // GU_STAGE, D_FF // W_STAGE
    copies = [pltpu.make_async_copy(gu_src(c), stage_gu.at[c % 2], sems.at[c % 2]) for c in range(n_gu)]
    copies += [pltpu.make_async_copy(d_src(c), stage_d.at[c % 2], sems.at[2 + c % 2]) for c in range(n_d)]
    copies[0].start()
    for i, copy in enumerate(copies):
        if i + 1 < len(copies):
            copies[i + 1].start()
        copy.wait()
        if i < n_gu:
            wgu_bf[i * GU_STAGE:(i + 1) * GU_STAGE, :] = stage_gu[i % 2].astype(BF16)
        else:
            c = i - n_gu
            wd_bf[c * W_STAGE:(c + 1) * W_STAGE, :] = stage_d[c % 2].astype(BF16)


_WEIGHT_SCRATCH = [pltpu.VMEM((D_MODEL, 2 * D_FF), BF16), pltpu.VMEM((D_FF, D_MODEL), BF16),
                   pltpu.VMEM((2, GU_STAGE, 2 * D_FF), F32), pltpu.VMEM((2, W_STAGE, D_MODEL), F32),
                   pltpu.SemaphoreType.DMA((4,))]


FFN_TILE = WIDE_TILE


def _ffn_dense_kernel(x1_ref, h2_ref, g2_head_ref, g2_ref, lg_ref, lb_ref, wgu_hbm, wd_hbm, o_ref,
                      wgu_bf, wd_bf, stage_gu, stage_d, sems, *, fj):
    @pl.when((pl.program_id(0) == 0) & (pl.program_id(1) == 0))
    def _():
        _stage_weights((fj,), wgu_hbm, wd_hbm, wgu_bf, wd_bf, stage_gu, stage_d, sems)

    y = _swiglu_acc(h2_ref[...], wgu_bf, wd_bf)
    for rows, gate in ((slice(0, TILE), g2_head_ref), (slice(TILE, FFN_TILE), g2_ref)):
        o_ref[rows, :] = (_ln(ALPHA * x1_ref[rows, :] + gate[...] * y[rows]) * lg_ref[...] + lb_ref[...])


def _ffn_dense(li, x1, h2, mod5, lg, lb, wgu, wd):
    row = _layer_spec((1, D_MODEL), li)
    tok = pl.BlockSpec((None, FFN_TILE, D_MODEL), lambda b, j: (b, j, 0))
    batch_mod = pl.BlockSpec((None, None, None, 1, D_MODEL), lambda b, j: (li, b, 5, 0, 0))
    return pl.pallas_call(
        functools.partial(_ffn_dense_kernel, fj=li // 2),
        grid=(BATCH, ROWS // FFN_TILE),
        in_specs=[tok, tok, _mod_spec(li, 5, 0), batch_mod, row, row,
                  pl.BlockSpec(memory_space=pl.ANY), pl.BlockSpec(memory_space=pl.ANY)],
        out_specs=tok,
        out_shape=jax.ShapeDtypeStruct((BATCH, ROWS, D_MODEL), F32),
        scratch_shapes=_WEIGHT_SCRATCH,
        compiler_params=_cparams(("arbitrary", "arbitrary"), VMEM_LIMIT),
        name=f"ffn_dense_l{li}",
    )(x1, h2, mod5, mod5, lg, lb, wgu, wd)


EXP_TILE = 512


def _slot_row_copy(src_ref, src_row, dst_ref, dst_row, sem):
    return pltpu.make_async_copy(src_ref.at[pl.ds(src_row, 1), :], dst_ref.at[pl.ds(dst_row, 1), :], sem)


def _wait_rows(src_ref, dst_ref, sem):
    pltpu.make_async_copy(src_ref, dst_ref, sem).wait()


def _dispatch_kernel(pos_ref, fill_start_ref, fill_rows_ref, h2_ref, xs_ref, zero_scr, sem, zsem):
    first = (pl.program_id(0) == 0) & (pl.program_id(1) == 0)
    sub = 8
    pad_chunks = [1 << s for s in reversed(range(sub.bit_length() - 1, EXP_TILE.bit_length() - 1))]

    def pad_copies(e):
        start, n = fill_start_ref[e], fill_rows_ref[e]
        head = (-start) & (sub - 1)
        for i in range(sub - 1):
            yield i < jnp.minimum(head, n), _slot_row_copy(zero_scr, i, xs_ref, start + i, zsem)
        body = jnp.maximum(n - head, 0)
        for rows in pad_chunks:
            off = pl.multiple_of(start + head + (body & ~(2 * rows - 1)), sub)
            dst = xs_ref.at[pl.ds(off, rows), :]
            yield (body & rows) != 0, pltpu.make_async_copy(zero_scr.at[pl.ds(0, rows), :], dst, zsem)

    @pl.when(first)
    def _():
        zero_scr[...] = jnp.zeros(zero_scr.shape, F32)
        for e in range(fill_rows_ref.shape[0]):
            for cond, copy in pad_copies(e):
                pl.when(cond)(copy.start)

    base = (pl.program_id(0) * pl.num_programs(1) + pl.program_id(1)) * (2 * TILE)

    for r in range(TILE):
        for k in range(2):
            _slot_row_copy(h2_ref, r, xs_ref, pos_ref[base + 2 * r + k], sem).start(priority=k)
    for k in range(2):
        _wait_rows(h2_ref, xs_ref.at[pl.ds(0, TILE), :], sem)

    @pl.when(first)
    def _():
        for e in range(fill_rows_ref.shape[0]):
            for cond, copy in pad_copies(e):
                pl.when(cond)(copy.wait)


def _grouped_kernel(tile_e_ref, n_used_ref, xs_ref, wgu_hbm, wd_hbm, ys_ref,
                    wgu_bf, wd_bf, stage_gu, stage_d, sems, *, fj):
    i = pl.program_id(0)
    live = i < n_used_ref[0]
    e = tile_e_ref[i]

    @pl.when(live & ((i == 0) | (e != tile_e_ref[jnp.maximum(i - 1, 0)])))
    def _():
        _stage_weights((fj, e), wgu_hbm, wd_hbm, wgu_bf, wd_bf, stage_gu, stage_d, sems)

    @pl.when(live)
    def _():
        ys_ref[...] = _swiglu_acc(xs_ref[...].astype(BF16), wgu_bf, wd_bf)

    @pl.when(jnp.logical_not(live))
    def _():
        ys_ref[...] = jnp.zeros(ys_ref.shape, F32)


def _combine_kernel(pos_ref, ys_ref, route_ref, x1_ref, g2_ref, lg_ref, lb_ref, o_ref, ybuf, sem):
    base = (pl.program_id(0) * pl.num_programs(1) + pl.program_id(1)) * (2 * TILE)

    for r in range(TILE):
        for k in range(2):
            _slot_row_copy(ys_ref, pos_ref[base + 2 * r + k], ybuf.at[k], r, sem).start(priority=k)
    for k in range(2):
        _wait_rows(ys_ref.at[pl.ds(0, TILE), :], ybuf.at[k], sem)
    route = route_ref[...]
    lane = lax.broadcasted_iota(jnp.int32, route.shape, 1)
    w1 = jnp.sum(jnp.where(lane == R_W1, route, 0.0), axis=-1, keepdims=True)
    w2 = jnp.sum(jnp.where(lane == R_W2, route, 0.0), axis=-1, keepdims=True)
    y = w1 * ybuf[0] + w2 * ybuf[1]
    o_ref[...] = _ln(ALPHA * x1_ref[...] + g2_ref[...] * y) * lg_ref[...] + lb_ref[...]


def _ffn_routed(li, x1, h2, route, counts, mod5, lg, lb, wgu, wd, j0):
    nt = N_TILES - j0
    n_tok = BATCH * nt * TILE
    n_tiles = 2 * n_tok // EXP_TILE + N_EXP
    n_slots = n_tiles * EXP_TILE
    cnt = counts[0, :N_EXP].astype(jnp.int32)
    padded = (cnt + EXP_TILE - 1) // EXP_TILE * EXP_TILE
    ends = jnp.cumsum(padded)
    offs = ends - padded
    n_used = ends[-1:] // EXP_TILE
    tile_start = jnp.minimum(jnp.arange(n_tiles), n_used - 1) * EXP_TILE
    tile_e = jnp.sum((tile_start[:, None] >= ends[None, :]).astype(jnp.int32), axis=1)
    rec = route.reshape(n_tok, LANES)
    experts = rec[:, R_E1:R_E2 + 1].astype(jnp.int32)
    ranks = rec[:, R_RANK1:R_RANK2 + 1].astype(jnp.int32)
    off_of = jnp.sum(jnp.where(experts[..., None] == jnp.arange(N_EXP), offs, 0), axis=-1)
    pos = (off_of + ranks).reshape(2 * n_tok)

    h2_flat = h2.reshape(n_tok, D_MODEL)
    half = EXP_TILE // 2
    tail_start = n_used * EXP_TILE + jnp.arange(2 * N_EXP) * half
    tail_rows = jnp.where(tail_start < n_slots, half, 0)
    tail_start = jnp.minimum(tail_start, n_slots - half)
    any_spec = pl.BlockSpec(memory_space=pl.ANY)
    xs = pl.pallas_call(
        _dispatch_kernel,
        grid_spec=pltpu.PrefetchScalarGridSpec(
            num_scalar_prefetch=3,
            grid=(BATCH, nt),
            in_specs=[pl.BlockSpec((TILE, D_MODEL), lambda b, j, *_: (b * nt + j, 0))],
            out_specs=any_spec,
            scratch_shapes=[pltpu.VMEM((EXP_TILE // 2, D_MODEL), F32),
                            pltpu.SemaphoreType.DMA(()), pltpu.SemaphoreType.DMA(())]),
        out_shape=jax.ShapeDtypeStruct((n_slots, D_MODEL), F32),
        compiler_params=_cparams(("arbitrary", "arbitrary")),
        name=f"moe_dispatch_l{li}",
    )(pos, jnp.concatenate([offs + cnt, tail_start]), jnp.concatenate([padded - cnt, tail_rows]), h2_flat)

    used = lambda i, te, nu: jnp.minimum(i, nu[0] - 1)
    ys = pl.pallas_call(
        functools.partial(_grouped_kernel, fj=li // 2),
        grid_spec=pltpu.PrefetchScalarGridSpec(
            num_scalar_prefetch=2,
            grid=(n_tiles,),
            in_specs=[
                pl.BlockSpec((EXP_TILE, D_MODEL), lambda i, te, nu: (used(i, te, nu), 0)),
                pl.BlockSpec(memory_space=pl.ANY), pl.BlockSpec(memory_space=pl.ANY)],
            out_specs=pl.BlockSpec((EXP_TILE, D_MODEL), lambda i, te, nu: (i, 0)),
            scratch_shapes=_WEIGHT_SCRATCH),
        out_shape=jax.ShapeDtypeStruct((n_slots, D_MODEL), F32),
        compiler_params=_cparams(("arbitrary",), VMEM_LIMIT),
        name=f"moe_grouped_l{li}",
    )(tile_e, n_used, xs, wgu, wd)

    row = pl.BlockSpec((None, 1, D_MODEL), lambda b, j, pos: (li, 0, 0))
    tok = lambda w: pl.BlockSpec((None, TILE, w), lambda b, j, pos: (b, j, 0))
    mod = pl.BlockSpec((None, None, None, 1, D_MODEL),
                       lambda b, j, pos: (li, jnp.where(j + j0 == 0, CTX_MOD_ROW, b), 5, 0, 0))
    return pl.pallas_call(
        _combine_kernel,
        grid_spec=pltpu.PrefetchScalarGridSpec(
            num_scalar_prefetch=1,
            grid=(BATCH, nt),
            in_specs=[any_spec, tok(LANES), tok(D_MODEL), mod, row, row],
            out_specs=tok(D_MODEL),
            scratch_shapes=[pltpu.VMEM((2, TILE, D_MODEL), F32), pltpu.SemaphoreType.DMA(())]),
        out_shape=jax.ShapeDtypeStruct((BATCH, nt * TILE, D_MODEL), F32),
        compiler_params=_cparams(("arbitrary", "arbitrary"), VMEM_LIMIT),
        name=f"moe_combine_l{li}",
    )(pos, ys, route, x1, mod5, lg, lb)


def _rope_tables(dim):
    rows_n = SEQ // GRID_W
    t_row = jnp.repeat(jnp.arange(rows_n, dtype=F32), GRID_W)
    t_col = jnp.tile(jnp.arange(GRID_W, dtype=F32), rows_n)
    quarter = dim // 4
    inv = ROPE_BASE ** (-jnp.arange(quarter, dtype=F32) / quarter)
    ang = jnp.concatenate([t_row[:, None] * inv, t_col[:, None] * inv], axis=-1)
    cos, sin = jnp.cos(ang), jnp.sin(ang)
    reps = LANES // dim
    cos_f = jnp.tile(jnp.concatenate([cos, cos], axis=-1), (1, reps))
    sin_s = jnp.tile(jnp.concatenate([-sin, sin], axis=-1), (1, reps))
    cos_f = jnp.concatenate([jnp.ones((CTX_LEN, LANES), F32), cos_f], axis=0)
    sin_s = jnp.concatenate([jnp.zeros((CTX_LEN, LANES), F32), sin_s], axis=0)
    return cos_f, sin_s


def _head_expand(offset):
    rows = jnp.arange(LANES)[:, None]
    cols = jnp.arange(D_INNER)[None, :]
    return (rows == cols // HD_S + offset).astype(BF16)


def kernel(x, c, ctx, c_ctx, w_ada, b_ada, w_in, attn_sink, conv_w, conv_b, dt_bias, a_log, d_skip,
           ssm_norm_w, lam_q, lam_k, diff_norm_w, w_out, ln1_g, ln1_b, ln2_g, ln2_b, ffn_w_gu,
           ffn_w_down, router_w, router_b, exp_w_gu, exp_w_down):
    x_all = jnp.concatenate([ctx, x], axis=1)
    c_all = jnp.concatenate(
        [c, c_ctx[None, :], jnp.zeros((MOD_ROWS - BATCH - 1, D_MODEL), F32)], axis=0)
    mod5 = _ada_all(c_all, w_ada, b_ada).reshape(DEPTH, MOD_ROWS, 6, 1, D_MODEL)
    ropes = _rope_tables(HD_A) + _rope_tables(HD_C)
    ef, eb = _head_expand(0), _head_expand(N_HS)

    w_q = (w_in[:, :, :384].reshape(DEPTH, D_MODEL, N_KVA, 3, HD_A).transpose(0, 1, 3, 2, 4)
           .reshape(DEPTH, D_MODEL, 384))
    w_re = jnp.concatenate(
        [w_q, w_in[:, :, 384:1932], jnp.zeros((DEPTH, D_MODEL, D_IN_PAD - 2700), F32), w_in[:, :, 1932:]],
        axis=2).astype(BF16)
    wo_a = (w_out[:, :384].reshape(DEPTH, N_KVA, 3, HD_A, D_MODEL).transpose(0, 2, 1, 3, 4)
            .reshape(DEPTH, 384, D_MODEL))
    wo_re = jnp.concatenate([wo_a, w_out[:, 384:]], axis=1).astype(BF16)
    nw_c = jnp.tile(diff_norm_w, (1, N_HC))[:, None, :]
    cw = jnp.pad(conv_w, ((0, 0), (0, 8 - CONV_K), (0, 0)))
    pad12 = lambda a: jnp.pad(a.reshape(DEPTH, 1, 2 * N_HS), ((0, 0), (0, 0), (0, LANES - 2 * N_HS)))
    dtb, alog = pad12(dt_bias), pad12(a_log)
    dsk = jnp.repeat(d_skip, HD_S, axis=1)[:, None, :]
    lg1, lb1, lg2, lb2 = (a[:, None, :] for a in (ln1_g, ln1_b, ln2_g, ln2_b))
    wr_f = jnp.pad(router_w, ((0, 0), (0, 0), (0, LANES - N_EXP)))
    wr_hi = lax.bitcast_convert_type(lax.bitcast_convert_type(wr_f, jnp.uint32) & jnp.uint32(0xFFFF0000), F32)
    wr = jnp.concatenate([wr_hi.astype(BF16), (wr_f - wr_hi).astype(BF16)], axis=2)
    br = jnp.pad(router_b, ((0, 0), (0, LANES - N_EXP)))[:, None, :]

    for li in range(DEPTH):
        ctx_out = li < DEPTH - 1
        j0 = 0 if ctx_out else CTX_LEN // TILE
        lam_init = 0.8 - 0.6 * math.exp(-0.3 * li)
        qa, ka, va, z, xbc, dt, qc, kc, vc = _inproj(li, x_all, mod5, w_re, ropes)
        ya_lat, ya_ctx = _attn_a(li, qa, ka, va, attn_sink[li], ctx_out)
        yd_lat, yd_ctx = _attn_c(li, qc, kc, vc, lam_q, lam_k, nw_c, lam_init, ctx_out)
        yb = _ssd(li, xbc, z, dt, cw, conv_b[:, None, :], dtb, alog, dsk, ssm_norm_w[:, None, :], ef, eb)
        moe = li % 2 == 1
        res = _outproj(li, ya_lat, ya_ctx, yb, yd_lat, yd_ctx, x_all, mod5, lg1, lb1, wo_re, moe, wr, br, j0)
        if moe:
            x1, h2, route, counts = res
            x_all = _ffn_routed(li, x1, h2, route, counts, mod5, lg2, lb2, exp_w_gu, exp_w_down, j0)
        else:
            x1, h2 = res
            x_all = _ffn_dense(li, x1, h2, mod5, lg2, lb2, ffn_w_gu, ffn_w_down)
    return x_all
```

```python
import functools
import math

import jax
import jax.numpy as jnp
from jax import lax
from jax.experimental import pallas as pl
from jax.experimental.pallas import tpu as pltpu

F32 = jnp.float32
BF16 = jnp.bfloat16

D_MODEL = 1024
BATCH = 8
SEQ = 2048
DEPTH = 4
GRID_W = 64
CTX_LEN = 256
BLK = 128
WINDOW = 128
ROPE_BASE = 10000.0
EPS = 1e-6
LOG2E = 1.4426950408889634
HD_A = 64
N_HA = 6
N_KVA = 2
D_INNER = 384
HD_S = 64
N_HS = 6
D_STATE = 128
CONV_K = 5
CONV_DIM = 896
CHUNK = 128
N_HC = 4
HD_C = 32
D_C = 256
D_FF = 2816
N_EXP = 8
ALPHA = (2 * DEPTH) ** 0.25

LANES = 128
TILE = 256
WIDE_TILE = 3 * TILE
ROWS = CTX_LEN + SEQ
N_TILES = ROWS // TILE
N_CHUNKS = ROWS // CHUNK
N_CTX_CHUNKS = CTX_LEN // CHUNK
FF_CHUNK = 256
N_FF_CHUNKS = D_FF // FF_CHUNK
D_IN_PAD = 2816
MOD_ROWS = 16
CTX_MOD_ROW = BATCH
VMEM_LIMIT = 56 * 1024 * 1024


def _cparams(sem, vmem=None):
    return pltpu.CompilerParams(dimension_semantics=sem, vmem_limit_bytes=vmem)


def _sigmoid(x):
    return 1.0 / (1.0 + jnp.exp(-x))


def _silu(x):
    return x * _sigmoid(x)


def _ln(x):
    mu = jnp.mean(x, axis=-1, keepdims=True)
    xc = x - mu
    var = jnp.mean(xc * xc, axis=-1, keepdims=True)
    return xc * lax.rsqrt(var + EPS)


def _dot(a, b, precision=None):
    return jnp.dot(a, b, preferred_element_type=F32, precision=precision)


def _split_bf16(x, parts):
    out = []
    for _ in range(parts - 1):
        hi = x.astype(BF16)
        out.append(hi)
        x = x - hi.astype(F32)
    out.append(x.astype(BF16))
    return out


def _dot_nt(a, b):
    return lax.dot_general(a, b, (((1,), (1,)), ((), ())), preferred_element_type=F32)


def _ada_kernel(c_ref, w_ref, b_ref, o_ref):
    act = _silu(c_ref[...])
    o_ref[...] = _dot(act.astype(BF16), w_ref[...].astype(BF16)) + b_ref[...]


def _ada_all(c_all, w_ada, b_ada):
    tn = 1536
    return pl.pallas_call(
        _ada_kernel,
        grid=(DEPTH, 6 * D_MODEL // tn),
        in_specs=[
            pl.BlockSpec((MOD_ROWS, D_MODEL), lambda l, j: (0, 0)),
            pl.BlockSpec((None, D_MODEL, tn), lambda l, j: (l, 0, j)),
            pl.BlockSpec((None, 1, tn), lambda l, j: (l, 0, j)),
        ],
        out_specs=pl.BlockSpec((None, MOD_ROWS, tn), lambda l, j: (l, 0, j)),
        out_shape=jax.ShapeDtypeStruct((DEPTH, MOD_ROWS, 6 * D_MODEL), F32),
        compiler_params=_cparams(("parallel", "parallel")),
        name="ada_ln",
    )(c_all, w_ada, b_ada.reshape(DEPTH, 1, 6 * D_MODEL))


def _mod_spec(li, piece, j0):
    return pl.BlockSpec(
        (None, None, None, 1, D_MODEL),
        lambda b, j: (li, jnp.where(j + j0 == 0, CTX_MOD_ROW, b), piece, 0, 0))


def _tok_spec(width, j0=0):
    return pl.BlockSpec((None, TILE, width), lambda b, j: (b, j + j0, 0))


def _full_spec(shape):
    n = len(shape)
    return pl.BlockSpec(shape, lambda *_: (0,) * n)


def _layer_spec(tail, li):
    n = len(tail)
    return pl.BlockSpec((None,) + tuple(tail), lambda *_: (li,) + (0,) * n)


def _rope(v, cos, sin_signed, half):
    lane = lax.broadcasted_iota(jnp.int32, (v.shape[0], LANES), 1)
    first = (lane % (2 * half)) < half
    outs = []
    for s in range(v.shape[1] // LANES):
        blk = v[:, s * LANES:(s + 1) * LANES]
        swapped = jnp.where(first, pltpu.roll(blk, LANES - half, 1), pltpu.roll(blk, half, 1))
        outs.append(blk * cos + swapped * sin_signed)
    return jnp.concatenate(outs, axis=1)


def _inproj_kernel(x_ref, sh_head_ref, sc_head_ref, sh_ref, sc_ref, w_ref,
                   cosa_ref, sina_ref, cosc_ref, sinc_ref,
                   qa_ref, ka_ref, va_ref, z_ref, xbc_ref, dt_ref, qc_ref, kc_ref, vc_ref):
    xn = _ln(x_ref[...])
    hb = jnp.concatenate([xn[:TILE] * (1.0 + sc_head_ref[...]) + sh_head_ref[...],
                          xn[TILE:] * (1.0 + sc_ref[...]) + sh_ref[...]], axis=0).astype(BF16)

    def proj(lo, hi):
        return _dot(hb, w_ref[:, lo:hi])

    cosa, sina = cosa_ref[...], sina_ref[...]
    cosc, sinc = cosc_ref[...], sinc_ref[...]
    qk_a = proj(0, 512)
    qa_ref[...] = (_rope(qk_a[:, :384], cosa, sina, HD_A // 2) * (HD_A ** -0.5 * LOG2E)).astype(BF16)
    ka_ref[...] = _rope(qk_a[:, 384:], cosa, sina, HD_A // 2).astype(BF16)
    vz = proj(512, 1024)
    va_ref[...] = vz[:, :LANES].astype(BF16)
    z_ref[...] = vz[:, LANES:]
    xbc_dt = proj(1024, 2048)
    xbc_ref[...] = xbc_dt[:, :CONV_DIM]
    dt_ref[...] = xbc_dt[:, CONV_DIM:]
    qc_ref[...] = (_rope(proj(2048, 2304), cosc, sinc, HD_C // 2) * (HD_C ** -0.5 * LOG2E)).astype(BF16)
    kc_ref[...] = _rope(proj(2304, 2560), cosc, sinc, HD_C // 2).astype(BF16)
    vc = proj(2560, 2816)
    lane = lax.broadcasted_iota(jnp.int32, vc.shape, 1)
    for h in range(N_HC):
        in_head = jnp.where(lane >= h * 2 * HD_C, lane - h * 2 * HD_C, 2 * HD_C) < 2 * HD_C
        vc_ref[h] = jnp.where(in_head, vc, 1.0).astype(BF16)


def _inproj(li, x_all, mod5, w_re, ropes):
    widths = (384, 128, 128, 384, CONV_DIM, LANES, D_C, D_C, D_C)
    dtypes = (BF16, BF16, BF16, F32, F32, F32, BF16, BF16, BF16)
    rope_spec = pl.BlockSpec((WIDE_TILE, LANES), lambda b, j: (j, 0))
    tok = lambda w: pl.BlockSpec((None, WIDE_TILE, w), lambda b, j: (b, j, 0))
    batch_mod = lambda piece: pl.BlockSpec(
        (None, None, None, 1, D_MODEL), lambda b, j: (li, b, piece, 0, 0))
    return pl.pallas_call(
        _inproj_kernel,
        grid=(BATCH, ROWS // WIDE_TILE),
        in_specs=[tok(D_MODEL), _mod_spec(li, 0, 0), _mod_spec(li, 1, 0), batch_mod(0), batch_mod(1),
                  _layer_spec((D_MODEL, D_IN_PAD), li), rope_spec, rope_spec, rope_spec, rope_spec],
        out_specs=[tok(w) for w in widths[:-1]]
        + [pl.BlockSpec((None, N_HC, WIDE_TILE, D_C), lambda b, j: (b, 0, j, 0))],
        out_shape=[jax.ShapeDtypeStruct((BATCH, ROWS, w), d) for w, d in zip(widths[:-1], dtypes)]
        + [jax.ShapeDtypeStruct((BATCH, N_HC, ROWS, D_C), BF16)],
        compiler_params=_cparams(("parallel", "parallel"), VMEM_LIMIT),
        name=f"inproj_l{li}",
    )(x_all, mod5, mod5, mod5, mod5, w_re, *ropes)


def _attn_a_blocks(sink_ref, blocks):
    tq = blocks[0][1].shape[0]
    lane_lo = lax.broadcasted_iota(jnp.int32, (tq, LANES), 1) < HD_A
    scores = []
    for _, q, k, _ in blocks:
        q = q.astype(F32)
        slabs = [q[:, s * LANES:(s + 1) * LANES] for s in range(3)]
        qst = jnp.concatenate([jnp.where(lane_lo, blk, 0.0) for blk in slabs]
                              + [jnp.where(lane_lo, 0.0, blk) for blk in slabs], axis=0).astype(BF16)
        scores.append(_dot_nt(qst, k))
    probs = [[] for _ in blocks]
    sink_terms = [[] for _ in blocks]
    for rb in range(N_HA):
        sink = sink_ref[rb] * LOG2E
        for i, (bias, _, _, _) in enumerate(blocks):
            sh = scores[i][rb * tq:(rb + 1) * tq]
            if bias is not None:
                sh = jnp.concatenate([sh[:, :3 * BLK] + bias, sh[:, 3 * BLK:]], axis=1)
            m = jnp.maximum(jnp.max(sh, axis=-1, keepdims=True), sink)
            probs[i].append(jnp.exp2(sh - m).astype(BF16))
            sink_terms[i].append(jnp.exp2(sink - m))
    results = []
    for i, (_, _, _, v) in enumerate(blocks):
        vf = v.astype(F32)
        key_lo = lax.broadcasted_iota(jnp.int32, vf.shape, 1) < HD_A
        v_heads = (jnp.where(key_lo, vf, 1.0).astype(BF16), jnp.where(key_lo, 1.0, vf).astype(BF16))
        outs = []
        for g in range(N_KVA):
            o = _dot(jnp.concatenate(probs[i][3 * g:3 * g + 3], axis=0), v_heads[g])
            den = pltpu.roll(o, HD_A, 1) + jnp.concatenate(sink_terms[i][3 * g:3 * g + 3], axis=0)
            outs.append(o * (1.0 / den))
        results.append(jnp.concatenate(
            [jnp.where(lane_lo, outs[0][s * tq:(s + 1) * tq], outs[1][s * tq:(s + 1) * tq])
             for s in range(3)], axis=1).astype(BF16))
    return results


def _attn_a_kernel(*refs, local):
    if not local:
        sink_ref, q_ref, kx_ref, vx_ref, o_ref = refs
        o_ref[...] = _attn_a_blocks(sink_ref, [(None, q_ref[...], kx_ref[...], vx_ref[...])])[0]
        return
    sink_ref, bias_a, bias_b, q_ref, k_refs, v_refs, o_ref = (*refs[:4], refs[4:9], refs[9:14], refs[14])
    k_blocks = [r[...] for r in k_refs]
    v_blocks = [r[...] for r in v_refs]
    blocks = []
    for half, bias_ref in enumerate((bias_a, bias_b)):
        k = jnp.concatenate(k_blocks[half:half + 3] + k_blocks[4:], axis=0)
        v = jnp.concatenate(v_blocks[half:half + 3] + v_blocks[4:], axis=0)
        blocks.append((bias_ref[...], q_ref[half * BLK:(half + 1) * BLK, :], k, v))
    for half, out in enumerate(_attn_a_blocks(sink_ref, blocks)):
        o_ref[half * BLK:(half + 1) * BLK, :] = out


def _band_bias():
    i = jnp.arange(BLK)[:, None]
    j = jnp.arange(3 * BLK)[None, :] - BLK
    in_band = jnp.abs(i - j) <= WINDOW
    first = in_band & (j >= 0)
    last = in_band & (j < BLK)
    return jnp.where(jnp.stack([first, in_band, last]), 0.0, -jnp.inf).astype(F32)


def _attn_a(li, qa, ka, va, sink, with_ctx):
    nb = SEQ // BLK
    off = CTX_LEN // BLK
    smem = pl.BlockSpec(memory_space=pltpu.SMEM)
    bias_a = pl.BlockSpec((None, BLK, 3 * BLK), lambda b, n: (jnp.where(n == 0, 0, 1), 0, 0))
    bias_b = pl.BlockSpec((None, BLK, 3 * BLK), lambda b, n: (jnp.where(n == nb // 2 - 1, 2, 1), 0, 0))

    def kv(shift):
        return pl.BlockSpec(
            (None, BLK, LANES), lambda b, n: (b, off + jnp.clip(2 * n + shift, 0, nb - 1), 0))

    ctx_kv = pl.BlockSpec((None, CTX_LEN, LANES), lambda b, n: (b, 0, 0))
    q_spec = pl.BlockSpec((None, 2 * BLK, 384), lambda b, n: (b, CTX_LEN // (2 * BLK) + n, 0))
    kvs = [kv(-1), kv(0), kv(1), kv(2), ctx_kv]
    y_lat = pl.pallas_call(
        functools.partial(_attn_a_kernel, local=True),
        grid=(BATCH, nb // 2),
        in_specs=[smem, bias_a, bias_b, q_spec] + kvs + kvs,
        out_specs=pl.BlockSpec((None, 2 * BLK, 384), lambda b, n: (b, n, 0)),
        out_shape=jax.ShapeDtypeStruct((BATCH, SEQ, 384), BF16),
        compiler_params=_cparams(("parallel", "parallel")),
        name=f"attn_a_l{li}",
    )(sink, _band_bias(), _band_bias(), qa, *([ka] * 5), *([va] * 5))
    if not with_ctx:
        return y_lat, None
    ctx_spec384 = pl.BlockSpec((None, CTX_LEN, 384), lambda b: (b, 0, 0))
    ctx_spec128 = pl.BlockSpec((None, CTX_LEN, LANES), lambda b: (b, 0, 0))
    y_ctx = pl.pallas_call(
        functools.partial(_attn_a_kernel, local=False),
        grid=(BATCH,),
        in_specs=[smem, ctx_spec384, ctx_spec128, ctx_spec128],
        out_specs=ctx_spec384,
        out_shape=jax.ShapeDtypeStruct((BATCH, CTX_LEN, 384), BF16),
        compiler_params=_cparams(("parallel",)),
        name=f"attn_a_ctx_l{li}",
    )(sink, qa, ka, va)
    return y_lat, y_ctx


def _attn_c_kernel(lq_ref, lk_ref, nw_ref, *refs, lam_init):
    q_refs, (k_ref, v_ref, o_ref) = refs[:-3], refs[-3:]
    tq = o_ref.shape[0]
    e = jnp.exp(jnp.sum(lq_ref[...] * lk_ref[...], axis=-1, keepdims=True))
    lam = e[0:1] - e[1:2] + lam_init
    q = jnp.concatenate([r[...] for r in q_refs], axis=0).astype(F32)
    k = k_ref[...]
    lane = lax.broadcasted_iota(jnp.int32, (tq, D_C), 1)
    lane2 = lax.broadcasted_iota(jnp.int32, (2 * tq, D_C), 1)
    acc = jnp.zeros((tq, D_C), F32)
    for h in range(N_HC):
        base = h * 2 * HD_C
        in0 = jnp.where(lane >= base, lane - base, 2 * HD_C) < HD_C
        in1 = jnp.where(lane >= base + HD_C, lane - base - HD_C, HD_C) < HD_C
        qst = jnp.concatenate([jnp.where(in0, q, 0.0), jnp.where(in1, q, 0.0)], axis=0).astype(BF16)
        s = _dot_nt(qst, k)
        p = jnp.exp2(s - jnp.max(s, axis=-1, keepdims=True)).astype(BF16)
        pv = _dot(p, v_ref[h])
        den = jnp.sum(jnp.where(lane2 == (base + 2 * HD_C) % D_C, pv, 0.0), axis=-1, keepdims=True)
        pv = pv * (1.0 / den)
        o = pv[:tq] - lam * pv[tq:]
        inh = jnp.where(lane >= base, lane - base, 2 * HD_C) < 2 * HD_C
        oh = jnp.where(inh, o, 0.0)
        ms = jnp.sum(oh * oh, axis=-1, keepdims=True) * (1.0 / (2 * HD_C))
        acc = acc + oh * lax.rsqrt(ms + EPS)
    o_ref[...] = (acc * nw_ref[...] * (1.0 - lam_init)).astype(BF16)


def _attn_c(li, qc, kc, vc, lam_q, lam_k, nw_row, lam_init, with_ctx):
    parts = 2
    tq = parts * TILE
    off = CTX_LEN // TILE
    q_specs = [pl.BlockSpec((None, TILE, D_C), lambda b, n, i=i: (b, off + parts * n + i, 0))
               for i in range(parts)]
    small = [_layer_spec((2, HD_C), li), _layer_spec((2, HD_C), li), _layer_spec((1, D_C), li)]
    kern = functools.partial(_attn_c_kernel, lam_init=lam_init)
    all_kv = pl.BlockSpec((None, ROWS, D_C), lambda b, n: (b, 0, 0))
    all_v = pl.BlockSpec((None, N_HC, ROWS, D_C), lambda b, n: (b, 0, 0, 0))
    y_lat = pl.pallas_call(
        kern,
        grid=(BATCH, SEQ // tq),
        in_specs=small + q_specs + [all_kv, all_v],
        out_specs=pl.BlockSpec((None, tq, D_C), lambda b, n: (b, n, 0)),
        out_shape=jax.ShapeDtypeStruct((BATCH, SEQ, D_C), BF16),
        compiler_params=_cparams(("parallel", "parallel"), VMEM_LIMIT),
        name=f"attn_c_l{li}",
    )(lam_q, lam_k, nw_row, *([qc] * parts), kc, vc)
    if not with_ctx:
        return y_lat, None
    ctx_spec = pl.BlockSpec((None, CTX_LEN, D_C), lambda b: (b, 0, 0))
    y_ctx = pl.pallas_call(
        kern,
        grid=(BATCH,),
        in_specs=small + [ctx_spec, ctx_spec,
                           pl.BlockSpec((None, N_HC, CTX_LEN, D_C), lambda b: (b, 0, 0, 0))],
        out_specs=ctx_spec,
        out_shape=jax.ShapeDtypeStruct((BATCH, CTX_LEN, D_C), BF16),
        compiler_params=_cparams(("parallel",)),
        name=f"attn_c_ctx_l{li}",
    )(lam_q, lam_k, nw_row, qc, kc, vc)
    return y_lat, y_ctx


def _ssd_kernel(xbc_ref, z_ref, dt_ref, cw_ref, cb_ref, dtb_ref, alog_ref, dsk_ref, nw_ref,
                ef_ref, eb_ref, y_ref, u_scr, yacc_scr, st_scr):
    cw = cw_ref[...]
    cb = cb_ref[...]
    dsk = dsk_ref[...]
    pad = 8
    for c in range(N_CHUNKS):
        r0 = CHUNK * c
        zeros = jnp.zeros((pad, CONV_DIM), F32)
        if c in (0, N_CTX_CHUNKS):
            blk = jnp.concatenate([zeros, xbc_ref[r0:r0 + CHUNK + pad, :]], axis=0)
        elif c in (N_CTX_CHUNKS - 1, N_CHUNKS - 1):
            blk = jnp.concatenate([xbc_ref[r0 - pad:r0 + CHUNK, :], zeros], axis=0)
        else:
            blk = xbc_ref[r0 - pad:r0 + CHUNK + pad, :]
        acc = None
        for j in range(CONV_K):
            sh = (CONV_K // 2 - j) % (CHUNK + 2 * pad)
            rolled = blk if sh == 0 else pltpu.roll(blk, sh, 0)
            term = rolled[pad:pad + CHUNK, :] * cw[j:j + 1, :]
            acc = term if acc is None else acc + term
        u = _silu(acc + cb)
        u_scr[r0:r0 + CHUNK, :] = u
        yacc_scr[r0:r0 + CHUNK, :] = u[:, :D_INNER] * dsk

    st_scr[...] = jnp.zeros(st_scr.shape, F32)
    ri = lax.broadcasted_iota(jnp.int32, (CHUNK, CHUNK), 0)
    ci = lax.broadcasted_iota(jnp.int32, (CHUNK, CHUNK), 1)
    lower = ri >= ci
    upper = ri <= ci
    tri = (jnp.where(lower, 1.0, 0.0).astype(BF16), jnp.where(upper, 1.0, 0.0).astype(BF16))
    a_row = -jnp.exp(alog_ref[...])
    dtb = dtb_ref[...]
    lane_lo = ci < HD_S

    def chunk_dir(c, d):
        r0 = pl.multiple_of(c * CHUNK, CHUNK)
        u = u_scr[pl.ds(r0, CHUNK), :]
        xs = u[:, :D_INNER]
        bm = u[:, D_INNER:D_INNER + 2 * D_STATE]
        cm = u[:, D_INNER + 2 * D_STATE:]
        vv = dt_ref[pl.ds(r0, CHUNK), :] + dtb
        dt = jnp.maximum(vv, 0.0) + jnp.log(1.0 + jnp.exp(-jnp.abs(vv)))
        a = dt * a_row
        cum = sum(_dot(tri[d], piece) for piece in _split_bf16(a, 3))
        cum_t = cum.T
        tot = cum[CHUNK - 1:CHUNK, :] if d == 0 else cum[0:1, :]
        expand = ef_ref[...] if d == 0 else eb_ref[...]
        per_head = jnp.concatenate([dt, jnp.exp(cum), jnp.exp(tot - cum)], axis=0)
        per_lane = sum(_dot(piece, expand) for piece in _split_bf16(per_head, 2))
        dt_x = per_lane[:CHUNK]
        dec_out_x = per_lane[CHUNK:2 * CHUNK]
        dec_st_x = per_lane[2 * CHUNK:]
        etot_x = dec_out_x[CHUNK - 1:CHUNK, :] if d == 0 else dec_out_x[0:1, :]
        xdt = xs * dt_x
        xdt_b = xdt.astype(BF16)
        xst_b = (xdt * dec_st_x).astype(BF16)
        cb_ = [cm[:, g * D_STATE:(g + 1) * D_STATE].astype(BF16) for g in range(2)]
        bb_ = [bm[:, g * D_STATE:(g + 1) * D_STATE].astype(BF16) for g in range(2)]
        bt_ = [bm[:, g * D_STATE:(g + 1) * D_STATE].T.astype(BF16) for g in range(2)]
        cbm = [_dot_nt(cb_[g], bb_[g]) for g in range(2)]
        mask = lower if d == 0 else upper
        for j in range(3):
            sl = slice(j * LANES, (j + 1) * LANES)
            parts = []
            for h in (2 * j, 2 * j + 1):
                hl = h + N_HS * d
                seg = jnp.where(mask, cum[:, hl:hl + 1] - cum_t[hl:hl + 1, :], -jnp.inf)
                sc = (cbm[h // 3] * jnp.exp(seg)).astype(BF16)
                parts.append(_dot(sc, xdt_b[:, sl]))
            y_diag = jnp.where(lane_lo, parts[0], parts[1])
            st = st_scr[d, j]
            st_b = st.astype(BF16)
            g0, g1 = (2 * j) // 3, (2 * j + 1) // 3
            if g0 == g1:
                y_off = _dot(cb_[g0], st_b)
                upd = _dot(bt_[g0], xst_b[:, sl])
            else:
                y_off = jnp.where(lane_lo, _dot(cb_[g0], st_b), _dot(cb_[g1], st_b))
                upd = jnp.where(lane_lo, _dot(bt_[g0], xst_b[:, sl]), _dot(bt_[g1], xst_b[:, sl]))
            yacc_scr[pl.ds(r0, CHUNK), sl] += y_diag + y_off * dec_out_x[:, sl]
            st_scr[d, j] = etot_x[:, sl] * st + upd

    def body(i, carry):
        chunk_dir(i, 0)
        chunk_dir(jnp.where(i < N_CTX_CHUNKS, N_CTX_CHUNKS - 1 - i, N_CHUNKS + N_CTX_CHUNKS - 1 - i), 1)
        return carry

    lax.fori_loop(0, N_CHUNKS, body, 0, unroll=3)

    nw = nw_ref[...]
    for t in range(N_TILES):
        rows = slice(t * TILE, (t + 1) * TILE)
        y = yacc_scr[rows, :] * _silu(z_ref[rows, :])
        ms = jnp.mean(y * y, axis=-1, keepdims=True)
        y_ref[rows, :] = (y * lax.rsqrt(ms + EPS) * nw).astype(BF16)


def _ssd(li, xbc, z, dt, cw, cb, dtb, alog, dsk, nw, ef, eb):
    def seq(width):
        return pl.BlockSpec((None, ROWS, width), lambda b: (b, 0, 0))

    return pl.pallas_call(
        _ssd_kernel,
        grid=(BATCH,),
        in_specs=[seq(CONV_DIM), seq(D_INNER), seq(LANES),
                  _layer_spec((8, CONV_DIM), li), _layer_spec((1, CONV_DIM), li),
                  _layer_spec((1, LANES), li), _layer_spec((1, LANES), li),
                  _layer_spec((1, D_INNER), li), _layer_spec((1, D_INNER), li),
                  _full_spec((LANES, D_INNER)), _full_spec((LANES, D_INNER))],
        out_specs=seq(D_INNER),
        out_shape=jax.ShapeDtypeStruct((BATCH, ROWS, D_INNER), BF16),
        scratch_shapes=[pltpu.VMEM((ROWS, CONV_DIM), F32), pltpu.VMEM((ROWS, D_INNER), F32),
                        pltpu.VMEM((2, 3, D_STATE, LANES), F32)],
        compiler_params=_cparams(("parallel",), VMEM_LIMIT),
        name=f"ssd_l{li}",
    )(xbc, z, dt, cw, cb, dtb, alog, dsk, nw, ef, eb)


def _top2_gates(logits):
    lane = lax.broadcasted_iota(jnp.int32, logits.shape, 1).astype(F32)
    neg = -jnp.inf
    lg = jnp.where(lane < N_EXP, logits, neg)
    m1 = jnp.max(lg, axis=-1, keepdims=True)
    i1 = jnp.min(jnp.where(lg == m1, lane, float(LANES)), axis=-1, keepdims=True)
    lg2 = jnp.where(lane == i1, neg, lg)
    m2 = jnp.max(lg2, axis=-1, keepdims=True)
    i2 = jnp.min(jnp.where(lg2 == m2, lane, float(LANES)), axis=-1, keepdims=True)
    e2 = jnp.exp(m2 - m1)
    return lane, i1, i2, 1.0 / (1.0 + e2), e2 / (1.0 + e2)


R_E1, R_E2, R_W1, R_W2, R_RANK1, R_RANK2 = range(6)


def _route_record(logits, cnt_scr):
    lane, i1, i2, w1, w2 = _top2_gates(logits)
    rows = logits.shape[0]
    hit1, hit2 = lane == i1, lane == i2
    onehot = jnp.where(hit1, 1.0, 0.0) + jnp.where(hit2, 1.0, 0.0)
    ri = lax.broadcasted_iota(jnp.int32, (rows, rows), 0)
    ci = lax.broadcasted_iota(jnp.int32, (rows, rows), 1)
    earlier = jnp.where(ri > ci, 1.0, 0.0).astype(BF16)
    before = _dot(earlier, onehot.astype(BF16)) + cnt_scr[...]
    rank1 = jnp.sum(jnp.where(hit1, before, 0.0), axis=-1, keepdims=True)
    rank2 = jnp.sum(jnp.where(hit2, before, 0.0), axis=-1, keepdims=True)
    cnt_scr[...] += jnp.sum(onehot, axis=0, keepdims=True)
    rec = jnp.zeros(logits.shape, F32)
    for idx, val in ((R_E1, i1), (R_E2, i2), (R_W1, w1), (R_W2, w2), (R_RANK1, rank1), (R_RANK2, rank2)):
        rec = jnp.where(lane == float(idx), val, rec)
    return rec


def _outproj_kernel(*refs, moe, ctx_split):
    if ctx_split:
        ya_l, ya_c, yb_ref, yd_l, yd_c = refs[:5]
        refs = refs[5:]
    else:
        ya_l, yb_ref, yd_l = refs[:3]
        refs = refs[3:]
    x_ref, g1_ref, lg_ref, lb_ref, sh_ref, sc_ref, wo_ref = refs[:7]
    refs = refs[7:]
    if moe:
        wr_ref, br_ref, x1_ref, h2_ref, route_ref, counts_ref, cnt_scr = refs

        @pl.when((pl.program_id(0) == 0) & (pl.program_id(1) == 0))
        def _():
            cnt_scr[...] = jnp.zeros(cnt_scr.shape, F32)
    else:
        x1_ref, h2_ref = refs
    if ctx_split:
        is_ctx = pl.program_id(1) == 0
        ya = jnp.where(is_ctx, ya_c[...], ya_l[...])
        yd = jnp.where(is_ctx, yd_c[...], yd_l[...])
    else:
        ya, yd = ya_l[...], yd_l[...]
    mix = (_dot(ya, wo_ref[0:384, :]) + _dot(yb_ref[...], wo_ref[384:768, :])
           + _dot(yd, wo_ref[768:1024, :]))
    x1 = _ln(ALPHA * x_ref[...] + g1_ref[...] * mix) * lg_ref[...] + lb_ref[...]
    x1_ref[...] = x1
    h2 = _ln(x1) * (1.0 + sc_ref[...]) + sh_ref[...]
    h2_ref[...] = h2.astype(h2_ref.dtype)
    if moe:
        h_hi, h_mid = _split_bf16(h2, 2)
        both = _dot(h_hi, wr_ref[...])
        logits = (both[:, :LANES] + both[:, LANES:] + _dot(h_mid, wr_ref[:, :LANES])) + br_ref[...]
        route_ref[...] = _route_record(logits, cnt_scr)
        counts_ref[...] = cnt_scr[...]


def _outproj(li, ya_lat, ya_ctx, yb, yd_lat, yd_ctx, x_all, mod5, lg, lb, wo, moe, wr, br, j0):
    ctx_split = j0 == 0
    nt = N_TILES - j0
    off = CTX_LEN // TILE
    lat = lambda w: pl.BlockSpec((None, TILE, w), lambda b, j: (b, jnp.maximum(j + j0 - off, 0), 0))
    ctx = lambda w: pl.BlockSpec((None, TILE, w), lambda b, j: (b, 0, 0))
    row = _layer_spec((1, D_MODEL), li)
    if ctx_split:
        ins = [ya_lat, ya_ctx, yb, yd_lat, yd_ctx]
        specs = [lat(384), ctx(384), _tok_spec(D_INNER, j0), lat(D_C), ctx(D_C)]
    else:
        ins = [ya_lat, yb, yd_lat]
        specs = [lat(384), _tok_spec(D_INNER, j0), lat(D_C)]
    ins += [x_all, mod5, lg, lb, mod5, mod5, wo]
    specs += [_tok_spec(D_MODEL, j0), _mod_spec(li, 2, j0), row, row,
              _mod_spec(li, 3, j0), _mod_spec(li, 4, j0), _layer_spec((D_MODEL, D_MODEL), li)]
    outs = [jax.ShapeDtypeStruct((BATCH, nt * TILE, D_MODEL), F32),
            jax.ShapeDtypeStruct((BATCH, nt * TILE, D_MODEL), F32 if moe else BF16)]
    out_specs = [_tok_spec(D_MODEL), _tok_spec(D_MODEL)]
    scratch = []
    if moe:
        ins += [wr, br]
        specs += [_layer_spec((D_MODEL, 2 * LANES), li // 2), _layer_spec((1, LANES), li // 2)]
        outs += [jax.ShapeDtypeStruct((BATCH, nt * TILE, LANES), F32),
                 jax.ShapeDtypeStruct((1, LANES), F32)]
        out_specs += [_tok_spec(LANES), _full_spec((1, LANES))]
        scratch = [pltpu.VMEM((1, LANES), F32)]
    sem = ("arbitrary", "arbitrary") if moe else ("parallel", "parallel")
    return pl.pallas_call(
        functools.partial(_outproj_kernel, moe=moe, ctx_split=ctx_split),
        grid=(BATCH, nt),
        in_specs=specs,
        out_specs=out_specs,
        out_shape=outs,
        scratch_shapes=scratch,
        compiler_params=_cparams(sem, VMEM_LIMIT),
        name=f"outproj_l{li}",
    )(*ins)


def _swiglu_acc(h2, wgu_ref, wd_ref):
    acc = jnp.zeros((h2.shape[0], D_MODEL), F32)
    for c in range(N_FF_CHUNKS):
        cols = slice(c * FF_CHUNK, (c + 1) * FF_CHUNK)
        up_cols = slice(D_FF + c * FF_CHUNK, D_FF + (c + 1) * FF_CHUNK)
        act = (_silu(_dot(h2, wgu_ref[:, cols])) * _dot(h2, wgu_ref[:, up_cols])).astype(BF16)
        acc = acc + _dot(act, wd_ref[cols, :])
    return acc


W_STAGE = 256
GU_STAGE = 128


def _stage_weights(lead, wgu_hbm, wd_hbm, wgu_bf, wd_bf, stage_gu, stage_d, sems):
    def gu_src(c):
        return wgu_hbm.at[lead + (pl.ds(c * GU_STAGE, GU_STAGE), slice(None))]

    def d_src(c):
        return wd_hbm.at[lead + (pl.ds(c * W_STAGE, W_STAGE), slice(None))]

    n_gu, n_d = D_MODEL // GU_STAGE, D_FF // W_STAGE
    copies = [pltpu.make_async_copy(gu_src(c), stage_gu.at[c % 2], sems.at[c % 2]) for c in range(n_gu)]
    copies += [pltpu.make_async_copy(d_src(c), stage_d.at[c % 2], sems.at[2 + c % 2]) for c in range(n_d)]
    copies[0].start()
    for i, copy in enumerate(copies):
        if i + 1 < len(copies):
            copies[i + 1].start()
        copy.wait()
        if i < n_gu:
            wgu_bf[i * GU_STAGE:(i + 1) * GU_STAGE, :] = stage_gu[i % 2].astype(BF16)
        else:
            c = i - n_gu
            wd_bf[c * W_STAGE:(c + 1) * W_STAGE, :] = stage_d[c % 2].astype(BF16)


_WEIGHT_SCRATCH = [pltpu.VMEM((D_MODEL, 2 * D_FF), BF16), pltpu.VMEM((D_FF, D_MODEL), BF16),
                   pltpu.VMEM((2, GU_STAGE, 2 * D_FF), F32), pltpu.VMEM((2, W_STAGE, D_MODEL), F32),
                   pltpu.SemaphoreType.DMA((4,))]


FFN_TILE = WIDE_TILE


def _ffn_dense_kernel(x1_ref, h2_ref, g2_head_ref, g2_ref, lg_ref, lb_ref, wgu_hbm, wd_hbm, o_ref,
                      wgu_bf, wd_bf, stage_gu, stage_d, sems, *, fj):
    @pl.when((pl.program_id(0) == 0) & (pl.program_id(1) == 0))
    def _():
        _stage_weights((fj,), wgu_hbm, wd_hbm, wgu_bf, wd_bf, stage_gu, stage_d, sems)

    y = _swiglu_acc(h2_ref[...], wgu_bf, wd_bf)
    for rows, gate in ((slice(0, TILE), g2_head_ref), (slice(TILE, FFN_TILE), g2_ref)):
        o_ref[rows, :] = (_ln(ALPHA * x1_ref[rows, :] + gate[...] * y[rows]) * lg_ref[...] + lb_ref[...])


def _ffn_dense(li, x1, h2, mod5, lg, lb, wgu, wd):
    row = _layer_spec((1, D_MODEL), li)
    tok = pl.BlockSpec((None, FFN_TILE, D_MODEL), lambda b, j: (b, j, 0))
    batch_mod = pl.BlockSpec((None, None, None, 1, D_MODEL), lambda b, j: (li, b, 5, 0, 0))
    return pl.pallas_call(
        functools.partial(_ffn_dense_kernel, fj=li // 2),
        grid=(BATCH, ROWS // FFN_TILE),
        in_specs=[tok, tok, _mod_spec(li, 5, 0), batch_mod, row, row,
                  pl.BlockSpec(memory_space=pl.ANY), pl.BlockSpec(memory_space=pl.ANY)],
        out_specs=tok,
        out_shape=jax.ShapeDtypeStruct((BATCH, ROWS, D_MODEL), F32),
        scratch_shapes=_WEIGHT_SCRATCH,
        compiler_params=_cparams(("arbitrary", "arbitrary"), VMEM_LIMIT),
        name=f"ffn_dense_l{li}",
    )(x1, h2, mod5, mod5, lg, lb, wgu, wd)


EXP_TILE = 512


def _slot_row_copy(src_ref, src_row, dst_ref, dst_row, sem):
    return pltpu.make_async_copy(src_ref.at[pl.ds(src_row, 1), :], dst_ref.at[pl.ds(dst_row, 1), :], sem)


def _wait_rows(src_ref, dst_ref, sem):
    pltpu.make_async_copy(src_ref, dst_ref, sem).wait()


def _dispatch_kernel(pos_ref, fill_start_ref, fill_rows_ref, h2_ref, xs_ref, zero_scr, sem, zsem):
    first = (pl.program_id(0) == 0) & (pl.program_id(1) == 0)
    sub = 8
    pad_chunks = [1 << s for s in reversed(range(sub.bit_length() - 1, EXP_TILE.bit_length() - 1))]

    def pad_copies(e):
        start, n = fill_start_ref[e], fill_rows_ref[e]
        head = (-start) & (sub - 1)
        for i in range(sub - 1):
            yield i < jnp.minimum(head, n), _slot_row_copy(zero_scr, i, xs_ref, start + i, zsem)
        body = jnp.maximum(n - head, 0)
        for rows in pad_chunks:
            off = pl.multiple_of(start + head + (body & ~(2 * rows - 1)), sub)
            dst = xs_ref.at[pl.ds(off, rows), :]
            yield (body & rows) != 0, pltpu.make_async_copy(zero_scr.at[pl.ds(0, rows), :], dst, zsem)

    @pl.when(first)
    def _():
        zero_scr[...] = jnp.zeros(zero_scr.shape, F32)
        for e in range(fill_rows_ref.shape[0]):
            for cond, copy in pad_copies(e):
                pl.when(cond)(copy.start)

    base = (pl.program_id(0) * pl.num_programs(1) + pl.program_id(1)) * (2 * TILE)

    for r in range(TILE):
        for k in range(2):
            _slot_row_copy(h2_ref, r, xs_ref, pos_ref[base + 2 * r + k], sem).start(priority=k)
    for k in range(2):
        _wait_rows(h2_ref, xs_ref.at[pl.ds(0, TILE), :], sem)

    @pl.when(first)
    def _():
        for e in range(fill_rows_ref.shape[0]):
            for cond, copy in pad_copies(e):
                pl.when(cond)(copy.wait)


def _grouped_kernel(tile_e_ref, n_used_ref, xs_ref, wgu_hbm, wd_hbm, ys_ref,
                    wgu_bf, wd_bf, stage_gu, stage_d, sems, *, fj):
    i = pl.program_id(0)
    live = i < n_used_ref[0]
    e = tile_e_ref[i]

    @pl.when(live & ((i == 0) | (e != tile_e_ref[jnp.maximum(i - 1, 0)])))
    def _():
        _stage_weights((fj, e), wgu_hbm, wd_hbm, wgu_bf, wd_bf, stage_gu, stage_d, sems)

    @pl.when(live)
    def _():
        ys_ref[...] = _swiglu_acc(xs_ref[...].astype(BF16), wgu_bf, wd_bf)

    @pl.when(jnp.logical_not(live))
    def _():
        ys_ref[...] = jnp.zeros(ys_ref.shape, F32)


def _combine_kernel(pos_ref, ys_ref, route_ref, x1_ref, g2_ref, lg_ref, lb_ref, o_ref, ybuf, sem):
    base = (pl.program_id(0) * pl.num_programs(1) + pl.program_id(1)) * (2 * TILE)

    for r in range(TILE):
        for k in range(2):
            _slot_row_copy(ys_ref, pos_ref[base + 2 * r + k], ybuf.at[k], r, sem).start(priority=k)
    for k in range(2):
        _wait_rows(ys_ref.at[pl.ds(0, TILE), :], ybuf.at[k], sem)
    route = route_ref[...]
    lane = lax.broadcasted_iota(jnp.int32, route.shape, 1)
    w1 = jnp.sum(jnp.where(lane == R_W1, route, 0.0), axis=-1, keepdims=True)
    w2 = jnp.sum(jnp.where(lane == R_W2, route, 0.0), axis=-1, keepdims=True)
    y = w1 * ybuf[0] + w2 * ybuf[1]
    o_ref[...] = _ln(ALPHA * x1_ref[...] + g2_ref[...] * y) * lg_ref[...] + lb_ref[...]


def _ffn_routed(li, x1, h2, route, counts, mod5, lg, lb, wgu, wd, j0):
    nt = N_TILES - j0
    n_tok = BATCH * nt * TILE
    n_tiles = 2 * n_tok // EXP_TILE + N_EXP
    n_slots = n_tiles * EXP_TILE
    cnt = counts[0, :N_EXP].astype(jnp.int32)
    padded = (cnt + EXP_TILE - 1) // EXP_TILE * EXP_TILE
    ends = jnp.cumsum(padded)
    offs = ends - padded
    n_used = ends[-1:] // EXP_TILE
    tile_start = jnp.minimum(jnp.arange(n_tiles), n_used - 1) * EXP_TILE
    tile_e = jnp.sum((tile_start[:, None] >= ends[None, :]).astype(jnp.int32), axis=1)
    rec = route.reshape(n_tok, LANES)
    experts = rec[:, R_E1:R_E2 + 1].astype(jnp.int32)
    ranks = rec[:, R_RANK1:R_RANK2 + 1].astype(jnp.int32)
    off_of = jnp.sum(jnp.where(experts[..., None] == jnp.arange(N_EXP), offs, 0), axis=-1)
    pos = (off_of + ranks).reshape(2 * n_tok)

    h2_flat = h2.reshape(n_tok, D_MODEL)
    half = EXP_TILE // 2
    tail_start = n_used * EXP_TILE + jnp.arange(2 * N_EXP) * half
    tail_rows = jnp.where(tail_start < n_slots, half, 0)
    tail_start = jnp.minimum(tail_start, n_slots - half)
    any_spec = pl.BlockSpec(memory_space=pl.ANY)
    xs = pl.pallas_call(
        _dispatch_kernel,
        grid_spec=pltpu.PrefetchScalarGridSpec(
            num_scalar_prefetch=3,
            grid=(BATCH, nt),
            in_specs=[pl.BlockSpec((TILE, D_MODEL), lambda b, j, *_: (b * nt + j, 0))],
            out_specs=any_spec,
            scratch_shapes=[pltpu.VMEM((EXP_TILE // 2, D_MODEL), F32),
                            pltpu.SemaphoreType.DMA(()), pltpu.SemaphoreType.DMA(())]),
        out_shape=jax.ShapeDtypeStruct((n_slots, D_MODEL), F32),
        compiler_params=_cparams(("arbitrary", "arbitrary")),
        name=f"moe_dispatch_l{li}",
    )(pos, jnp.concatenate([offs + cnt, tail_start]), jnp.concatenate([padded - cnt, tail_rows]), h2_flat)

    used = lambda i, te, nu: jnp.minimum(i, nu[0] - 1)
    ys = pl.pallas_call(
        functools.partial(_grouped_kernel, fj=li // 2),
        grid_spec=pltpu.PrefetchScalarGridSpec(
            num_scalar_prefetch=2,
            grid=(n_tiles,),
            in_specs=[
                pl.BlockSpec((EXP_TILE, D_MODEL), lambda i, te, nu: (used(i, te, nu), 0)),
                pl.BlockSpec(memory_space=pl.ANY), pl.BlockSpec(memory_space=pl.ANY)],
            out_specs=pl.BlockSpec((EXP_TILE, D_MODEL), lambda i, te, nu: (i, 0)),
            scratch_shapes=_WEIGHT_SCRATCH),
        out_shape=jax.ShapeDtypeStruct((n_slots, D_MODEL), F32),
        compiler_params=_cparams(("arbitrary",), VMEM_LIMIT),
        name=f"moe_grouped_l{li}",
    )(tile_e, n_used, xs, wgu, wd)

    row = pl.BlockSpec((None, 1, D_MODEL), lambda b, j, pos: (li, 0, 0))
    tok = lambda w: pl.BlockSpec((None, TILE, w), lambda b, j, pos: (b, j, 0))
    mod = pl.BlockSpec((None, None, None, 1, D_MODEL),
                       lambda b, j, pos: (li, jnp.where(j + j0 == 0, CTX_MOD_ROW, b), 5, 0, 0))
    return pl.pallas_call(
        _combine_kernel,
        grid_spec=pltpu.PrefetchScalarGridSpec(
            num_scalar_prefetch=1,
            grid=(BATCH, nt),
            in_specs=[any_spec, tok(LANES), tok(D_MODEL), mod, row, row],
            out_specs=tok(D_MODEL),
            scratch_shapes=[pltpu.VMEM((2, TILE, D_MODEL), F32), pltpu.SemaphoreType.DMA(())]),
        out_shape=jax.ShapeDtypeStruct((BATCH, nt * TILE, D_MODEL), F32),
        compiler_params=_cparams(("arbitrary", "arbitrary"), VMEM_LIMIT),
        name=f"moe_combine_l{li}",
    )(pos, ys, route, x1, mod5, lg, lb)


def _rope_tables(dim):
    rows_n = SEQ // GRID_W
    t_row = jnp.repeat(jnp.arange(rows_n, dtype=F32), GRID_W)
    t_col = jnp.tile(jnp.arange(GRID_W, dtype=F32), rows_n)
    quarter = dim // 4
    inv = ROPE_BASE ** (-jnp.arange(quarter, dtype=F32) / quarter)
    ang = jnp.concatenate([t_row[:, None] * inv, t_col[:, None] * inv], axis=-1)
    cos, sin = jnp.cos(ang), jnp.sin(ang)
    reps = LANES // dim
    cos_f = jnp.tile(jnp.concatenate([cos, cos], axis=-1), (1, reps))
    sin_s = jnp.tile(jnp.concatenate([-sin, sin], axis=-1), (1, reps))
    cos_f = jnp.concatenate([jnp.ones((CTX_LEN, LANES), F32), cos_f], axis=0)
    sin_s = jnp.concatenate([jnp.zeros((CTX_LEN, LANES), F32), sin_s], axis=0)
    return cos_f, sin_s


def _head_expand(offset):
    rows = jnp.arange(LANES)[:, None]
    cols = jnp.arange(D_INNER)[None, :]
    return (rows == cols // HD_S + offset).astype(BF16)


def kernel(x, c, ctx, c_ctx, w_ada, b_ada, w_in, attn_sink, conv_w, conv_b, dt_bias, a_log, d_skip,
           ssm_norm_w, lam_q, lam_k, diff_norm_w, w_out, ln1_g, ln1_b, ln2_g, ln2_b, ffn_w_gu,
           ffn_w_down, router_w, router_b, exp_w_gu, exp_w_down):
    x_all = jnp.concatenate([ctx, x], axis=1)
    c_all = jnp.concatenate(
        [c, c_ctx[None, :], jnp.zeros((MOD_ROWS - BATCH - 1, D_MODEL), F32)], axis=0)
    mod5 = _ada_all(c_all, w_ada, b_ada).reshape(DEPTH, MOD_ROWS, 6, 1, D_MODEL)
    ropes = _rope_tables(HD_A) + _rope_tables(HD_C)
    ef, eb = _head_expand(0), _head_expand(N_HS)

    w_q = (w_in[:, :, :384].reshape(DEPTH, D_MODEL, N_KVA, 3, HD_A).transpose(0, 1, 3, 2, 4)
           .reshape(DEPTH, D_MODEL, 384))
    w_re = jnp.concatenate(
        [w_q, w_in[:, :, 384:1932], jnp.zeros((DEPTH, D_MODEL, D_IN_PAD - 2700), F32), w_in[:, :, 1932:]],
        axis=2).astype(BF16)
    wo_a = (w_out[:, :384].reshape(DEPTH, N_KVA, 3, HD_A, D_MODEL).transpose(0, 2, 1, 3, 4)
            .reshape(DEPTH, 384, D_MODEL))
    wo_re = jnp.concatenate([wo_a, w_out[:, 384:]], axis=1).astype(BF16)
    nw_c = jnp.tile(diff_norm_w, (1, N_HC))[:, None, :]
    cw = jnp.pad(conv_w, ((0, 0), (0, 8 - CONV_K), (0, 0)))
    pad12 = lambda a: jnp.pad(a.reshape(DEPTH, 1, 2 * N_HS), ((0, 0), (0, 0), (0, LANES - 2 * N_HS)))
    dtb, alog = pad12(dt_bias), pad12(a_log)
    dsk = jnp.repeat(d_skip, HD_S, axis=1)[:, None, :]
    lg1, lb1, lg2, lb2 = (a[:, None, :] for a in (ln1_g, ln1_b, ln2_g, ln2_b))
    wr_f = jnp.pad(router_w, ((0, 0), (0, 0), (0, LANES - N_EXP)))
    wr_hi = lax.bitcast_convert_type(lax.bitcast_convert_type(wr_f, jnp.uint32) & jnp.uint32(0xFFFF0000), F32)
    wr = jnp.concatenate([wr_hi.astype(BF16), (wr_f - wr_hi).astype(BF16)], axis=2)
    br = jnp.pad(router_b, ((0, 0), (0, LANES - N_EXP)))[:, None, :]

    for li in range(DEPTH):
        ctx_out = li < DEPTH - 1
        j0 = 0 if ctx_out else CTX_LEN // TILE
        lam_init = 0.8 - 0.6 * math.exp(-0.3 * li)
        qa, ka, va, z, xbc, dt, qc, kc, vc = _inproj(li, x_all, mod5, w_re, ropes)
        ya_lat, ya_ctx = _attn_a(li, qa, ka, va, attn_sink[li], ctx_out)
        yd_lat, yd_ctx = _attn_c(li, qc, kc, vc, lam_q, lam_k, nw_c, lam_init, ctx_out)
        yb = _ssd(li, xbc, z, dt, cw, conv_b[:, None, :], dtb, alog, dsk, ssm_norm_w[:, None, :], ef, eb)
        moe = li % 2 == 1
        res = _outproj(li, ya_lat, ya_ctx, yb, yd_lat, yd_ctx, x_all, mod5, lg1, lb1, wo_re, moe, wr, br, j0)
        if moe:
            x1, h2, route, counts = res
            x_all = _ffn_routed(li, x1, h2, route, counts, mod5, lg2, lb2, exp_w_gu, exp_w_down, j0)
        else:
            x1, h2 = res
            x_all = _ffn_dense(li, x1, h2, mod5, lg2, lb2, ffn_w_gu, ffn_w_down)
    return x_all
```

```python
import functools
import math

import jax
import jax.numpy as jnp
from jax import lax
from jax.experimental import pallas as pl
from jax.experimental.pallas import tpu as pltpu

F32 = jnp.float32
BF16 = jnp.bfloat16

D_MODEL = 1024
BATCH = 8
SEQ = 2048
DEPTH = 4
GRID_W = 64
CTX_LEN = 256
BLK = 128
WINDOW = 128
ROPE_BASE = 10000.0
EPS = 1e-6
LOG2E = 1.4426950408889634
HD_A = 64
N_HA = 6
N_KVA = 2
D_INNER = 384
HD_S = 64
N_HS = 6
D_STATE = 128
CONV_K = 5
CONV_DIM = 896
CHUNK = 128
N_HC = 4
HD_C = 32
D_C = 256
D_FF = 2816
N_EXP = 8
ALPHA = (2 * DEPTH) ** 0.25

LANES = 128
TILE = 256
WIDE_TILE = 3 * TILE
ROWS = CTX_LEN + SEQ
N_TILES = ROWS // TILE
N_CHUNKS = ROWS // CHUNK
N_CTX_CHUNKS = CTX_LEN // CHUNK
FF_CHUNK = 256
N_FF_CHUNKS = D_FF // FF_CHUNK
D_IN_PAD = 2816
MOD_ROWS = 16
CTX_MOD_ROW = BATCH
VMEM_LIMIT = 56 * 1024 * 1024


def _cparams(sem, vmem=None):
    return pltpu.CompilerParams(dimension_semantics=sem, vmem_limit_bytes=vmem)


def _sigmoid(x):
    return 1.0 / (1.0 + jnp.exp(-x))


def _silu(x):
    return x * _sigmoid(x)


def _ln(x):
    mu = jnp.mean(x, axis=-1, keepdims=True)
    xc = x - mu
    var = jnp.mean(xc * xc, axis=-1, keepdims=True)
    return xc * lax.rsqrt(var + EPS)


def _dot(a, b, precision=None):
    return jnp.dot(a, b, preferred_element_type=F32, precision=precision)


def _split_bf16(x, parts):
    out = []
    for _ in range(parts - 1):
        hi = x.astype(BF16)
        out.append(hi)
        x = x - hi.astype(F32)
    out.append(x.astype(BF16))
    return out


def _dot_nt(a, b):
    return lax.dot_general(a, b, (((1,), (1,)), ((), ())), preferred_element_type=F32)


def _ada_kernel(c_ref, w_ref, b_ref, o_ref):
    act = _silu(c_ref[...])
    o_ref[...] = _dot(act.astype(BF16), w_ref[...].astype(BF16)) + b_ref[...]


def _ada_all(c_all, w_ada, b_ada):
    tn = 1536
    return pl.pallas_call(
        _ada_kernel,
        grid=(DEPTH, 6 * D_MODEL // tn),
        in_specs=[
            pl.BlockSpec((MOD_ROWS, D_MODEL), lambda l, j: (0, 0)),
            pl.BlockSpec((None, D_MODEL, tn), lambda l, j: (l, 0, j)),
            pl.BlockSpec((None, 1, tn), lambda l, j: (l, 0, j)),
        ],
        out_specs=pl.BlockSpec((None, MOD_ROWS, tn), lambda l, j: (l, 0, j)),
        out_shape=jax.ShapeDtypeStruct((DEPTH, MOD_ROWS, 6 * D_MODEL), F32),
        compiler_params=_cparams(("parallel", "parallel")),
        name="ada_ln",
    )(c_all, w_ada, b_ada.reshape(DEPTH, 1, 6 * D_MODEL))


def _mod_spec(li, piece, j0):
    return pl.BlockSpec(
        (None, None, None, 1, D_MODEL),
        lambda b, j: (li, jnp.where(j + j0 == 0, CTX_MOD_ROW, b), piece, 0, 0))


def _tok_spec(width, j0=0):
    return pl.BlockSpec((None, TILE, width), lambda b, j: (b, j + j0, 0))


def _full_spec(shape):
    n = len(shape)
    return pl.BlockSpec(shape, lambda *_: (0,) * n)


def _layer_spec(tail, li):
    n = len(tail)
    return pl.BlockSpec((None,) + tuple(tail), lambda *_: (li,) + (0,) * n)


def _rope(v, cos, sin_signed, half):
    lane = lax.broadcasted_iota(jnp.int32, (v.shape[0], LANES), 1)
    first = (lane % (2 * half)) < half
    outs = []
    for s in range(v.shape[1] // LANES):
        blk = v[:, s * LANES:(s + 1) * LANES]
        swapped = jnp.where(first, pltpu.roll(blk, LANES - half, 1), pltpu.roll(blk, half, 1))
        outs.append(blk * cos + swapped * sin_signed)
    return jnp.concatenate(outs, axis=1)


def _inproj_kernel(x_ref, sh_head_ref, sc_head_ref, sh_ref, sc_ref, w_ref,
                   cosa_ref, sina_ref, cosc_ref, sinc_ref,
                   qa_ref, ka_ref, va_ref, z_ref, xbc_ref, dt_ref, qc_ref, kc_ref, vc_ref):
    xn = _ln(x_ref[...])
    hb = jnp.concatenate([xn[:TILE] * (1.0 + sc_head_ref[...]) + sh_head_ref[...],
                          xn[TILE:] * (1.0 + sc_ref[...]) + sh_ref[...]], axis=0).astype(BF16)

    def proj(lo, hi):
        return _dot(hb, w_ref[:, lo:hi])

    cosa, sina = cosa_ref[...], sina_ref[...]
    cosc, sinc = cosc_ref[...], sinc_ref[...]
    qk_a = proj(0, 512)
    qa_ref[...] = (_rope(qk_a[:, :384], cosa, sina, HD_A // 2) * (HD_A ** -0.5 * LOG2E)).astype(BF16)
    ka_ref[...] = _rope(qk_a[:, 384:], cosa, sina, HD_A // 2).astype(BF16)
    vz = proj(512, 1024)
    va_ref[...] = vz[:, :LANES].astype(BF16)
    z_ref[...] = vz[:, LANES:]
    xbc_dt = proj(1024, 2048)
    xbc_ref[...] = xbc_dt[:, :CONV_DIM]
    dt_ref[...] = xbc_dt[:, CONV_DIM:]
    qc_ref[...] = (_rope(proj(2048, 2304), cosc, sinc, HD_C // 2) * (HD_C ** -0.5 * LOG2E)).astype(BF16)
    kc_ref[...] = _rope(proj(2304, 2560), cosc, sinc, HD_C // 2).astype(BF16)
    vc = proj(2560, 2816)
    lane = lax.broadcasted_iota(jnp.int32, vc.shape, 1)
    for h in range(N_HC):
        in_head = jnp.where(lane >= h * 2 * HD_C, lane - h * 2 * HD_C, 2 * HD_C) < 2 * HD_C
        vc_ref[h] = jnp.where(in_head, vc, 1.0).astype(BF16)


def _inproj(li, x_all, mod5, w_re, ropes):
    widths = (384, 128, 128, 384, CONV_DIM, LANES, D_C, D_C, D_C)
    dtypes = (BF16, BF16, BF16, F32, F32, F32, BF16, BF16, BF16)
    rope_spec = pl.BlockSpec((WIDE_TILE, LANES), lambda b, j: (j, 0))
    tok = lambda w: pl.BlockSpec((None, WIDE_TILE, w), lambda b, j: (b, j, 0))
    batch_mod = lambda piece: pl.BlockSpec(
        (None, None, None, 1, D_MODEL), lambda b, j: (li, b, piece, 0, 0))
    return pl.pallas_call(
        _inproj_kernel,
        grid=(BATCH, ROWS // WIDE_TILE),
        in_specs=[tok(D_MODEL), _mod_spec(li, 0, 0), _mod_spec(li, 1, 0), batch_mod(0), batch_mod(1),
                  _layer_spec((D_MODEL, D_IN_PAD), li), rope_spec, rope_spec, rope_spec, rope_spec],
        out_specs=[tok(w) for w in widths[:-1]]
        + [pl.BlockSpec((None, N_HC, WIDE_TILE, D_C), lambda b, j: (b, 0, j, 0))],
        out_shape=[jax.ShapeDtypeStruct((BATCH, ROWS, w), d) for w, d in zip(widths[:-1], dtypes)]
        + [jax.ShapeDtypeStruct((BATCH, N_HC, ROWS, D_C), BF16)],
        compiler_params=_cparams(("parallel", "parallel"), VMEM_LIMIT),
        name=f"inproj_l{li}",
    )(x_all, mod5, mod5, mod5, mod5, w_re, *ropes)


def _attn_a_blocks(sink_ref, blocks):
    tq = blocks[0][1].shape[0]
    lane_lo = lax.broadcasted_iota(jnp.int32, (tq, LANES), 1) < HD_A
    scores = []
    for _, q, k, _ in blocks:
        q = q.astype(F32)
        slabs = [q[:, s * LANES:(s + 1) * LANES] for s in range(3)]
        qst = jnp.concatenate([jnp.where(lane_lo, blk, 0.0) for blk in slabs]
                              + [jnp.where(lane_lo, 0.0, blk) for blk in slabs], axis=0).astype(BF16)
        scores.append(_dot_nt(qst, k))
    probs = [[] for _ in blocks]
    sink_terms = [[] for _ in blocks]
    for rb in range(N_HA):
        sink = sink_ref[rb] * LOG2E
        for i, (bias, _, _, _) in enumerate(blocks):
            sh = scores[i][rb * tq:(rb + 1) * tq]
            if bias is not None:
                sh = jnp.concatenate([sh[:, :3 * BLK] + bias, sh[:, 3 * BLK:]], axis=1)
            m = jnp.maximum(jnp.max(sh, axis=-1, keepdims=True), sink)
            probs[i].append(jnp.exp2(sh - m).astype(BF16))
            sink_terms[i].append(jnp.exp2(sink - m))
    results = []
    for i, (_, _, _, v) in enumerate(blocks):
        vf = v.astype(F32)
        key_lo = lax.broadcasted_iota(jnp.int32, vf.shape, 1) < HD_A
        v_heads = (jnp.where(key_lo, vf, 1.0).astype(BF16), jnp.where(key_lo, 1.0, vf).astype(BF16))
        outs = []
        for g in range(N_KVA):
            o = _dot(jnp.concatenate(probs[i][3 * g:3 * g + 3], axis=0), v_heads[g])
            den = pltpu.roll(o, HD_A, 1) + jnp.concatenate(sink_terms[i][3 * g:3 * g + 3], axis=0)
            outs.append(o * (1.0 / den))
        results.append(jnp.concatenate(
            [jnp.where(lane_lo, outs[0][s * tq:(s + 1) * tq], outs[1][s * tq:(s + 1) * tq])
             for s in range(3)], axis=1).astype(BF16))
    return results


def _attn_a_kernel(*refs, local):
    if not local:
        sink_ref, q_ref, kx_ref, vx_ref, o_ref = refs
        o_ref[...] = _attn_a_blocks(sink_ref, [(None, q_ref[...], kx_ref[...], vx_ref[...])])[0]
        return
    sink_ref, bias_first, bias_mid, bias_last = refs[:4]
    q_refs = refs[4:4 + A_BLOCKS // 2]
    rest = refs[4 + A_BLOCKS // 2:]
    n_kv = A_BLOCKS + 3
    k_blocks = [r[...] for r in rest[:n_kv]]
    v_blocks = [r[...] for r in rest[n_kv:2 * n_kv]]
    o_ref = rest[2 * n_kv]
    blocks = []
    for i in range(A_BLOCKS):
        bias_ref = bias_first if i == 0 else bias_last if i == A_BLOCKS - 1 else bias_mid
        k = jnp.concatenate(k_blocks[i:i + 3] + k_blocks[-1:], axis=0)
        v = jnp.concatenate(v_blocks[i:i + 3] + v_blocks[-1:], axis=0)
        q = q_refs[i // 2][(i % 2) * BLK:(i % 2 + 1) * BLK, :]
        blocks.append((bias_ref[...], q, k, v))
    for i, out in enumerate(_attn_a_blocks(sink_ref, blocks)):
        o_ref[i * BLK:(i + 1) * BLK, :] = out


A_BLOCKS = 4


def _band_bias():
    i = jnp.arange(BLK)[:, None]
    j = jnp.arange(3 * BLK)[None, :] - BLK
    in_band = jnp.abs(i - j) <= WINDOW
    first = in_band & (j >= 0)
    last = in_band & (j < BLK)
    return jnp.where(jnp.stack([first, in_band, last]), 0.0, -jnp.inf).astype(F32)


def _attn_a(li, qa, ka, va, sink, with_ctx):
    nb = SEQ // BLK
    off = CTX_LEN // BLK
    smem = pl.BlockSpec(memory_space=pltpu.SMEM)
    steps = nb // A_BLOCKS
    bias_first = pl.BlockSpec((None, BLK, 3 * BLK), lambda b, n: (jnp.where(n == 0, 0, 1), 0, 0))
    bias_mid = pl.BlockSpec((None, BLK, 3 * BLK), lambda b, n: (1, 0, 0))
    bias_last = pl.BlockSpec((None, BLK, 3 * BLK), lambda b, n: (jnp.where(n == steps - 1, 2, 1), 0, 0))

    def kv(shift):
        return pl.BlockSpec(
            (None, BLK, LANES), lambda b, n: (b, off + jnp.clip(A_BLOCKS * n + shift, 0, nb - 1), 0))

    ctx_kv = pl.BlockSpec((None, CTX_LEN, LANES), lambda b, n: (b, 0, 0))
    q_specs = [pl.BlockSpec((None, 2 * BLK, 384),
                            lambda b, n, i=i: (b, CTX_LEN // (2 * BLK) + (A_BLOCKS // 2) * n + i, 0))
               for i in range(A_BLOCKS // 2)]
    kvs = [kv(s) for s in range(-1, A_BLOCKS + 1)] + [ctx_kv]
    y_lat = pl.pallas_call(
        functools.partial(_attn_a_kernel, local=True),
        grid=(BATCH, steps),
        in_specs=[smem, bias_first, bias_mid, bias_last] + q_specs + kvs + kvs,
        out_specs=pl.BlockSpec((None, A_BLOCKS * BLK, 384), lambda b, n: (b, n, 0)),
        out_shape=jax.ShapeDtypeStruct((BATCH, SEQ, 384), BF16),
        compiler_params=_cparams(("parallel", "parallel")),
        name=f"attn_a_l{li}",
    )(sink, *([_band_bias()] * 3), *([qa] * (A_BLOCKS // 2)), *([ka] * len(kvs)), *([va] * len(kvs)))
    if not with_ctx:
        return y_lat, None
    ctx_spec384 = pl.BlockSpec((None, CTX_LEN, 384), lambda b: (b, 0, 0))
    ctx_spec128 = pl.BlockSpec((None, CTX_LEN, LANES), lambda b: (b, 0, 0))
    y_ctx = pl.pallas_call(
        functools.partial(_attn_a_kernel, local=False),
        grid=(BATCH,),
        in_specs=[smem, ctx_spec384, ctx_spec128, ctx_spec128],
        out_specs=ctx_spec384,
        out_shape=jax.ShapeDtypeStruct((BATCH, CTX_LEN, 384), BF16),
        compiler_params=_cparams(("parallel",)),
        name=f"attn_a_ctx_l{li}",
    )(sink, qa, ka, va)
    return y_lat, y_ctx


def _attn_c_kernel(lq_ref, lk_ref, nw_ref, *refs, lam_init):
    q_refs, (k_ref, v_ref, o_ref) = refs[:-3], refs[-3:]
    tq = o_ref.shape[0]
    e = jnp.exp(jnp.sum(lq_ref[...] * lk_ref[...], axis=-1, keepdims=True))
    lam = e[0:1] - e[1:2] + lam_init
    q = jnp.concatenate([r[...] for r in q_refs], axis=0).astype(F32)
    k = k_ref[...]
    lane = lax.broadcasted_iota(jnp.int32, (tq, D_C), 1)
    lane2 = lax.broadcasted_iota(jnp.int32, (2 * tq, D_C), 1)
    acc = jnp.zeros((tq, D_C), F32)
    for h in range(N_HC):
        base = h * 2 * HD_C
        in0 = jnp.where(lane >= base, lane - base, 2 * HD_C) < HD_C
        in1 = jnp.where(lane >= base + HD_C, lane - base - HD_C, HD_C) < HD_C
        qst = jnp.concatenate([jnp.where(in0, q, 0.0), jnp.where(in1, q, 0.0)], axis=0).astype(BF16)
        s = _dot_nt(qst, k)
        p = jnp.exp2(s - jnp.max(s, axis=-1, keepdims=True)).astype(BF16)
        pv = _dot(p, v_ref[h])
        den = jnp.sum(jnp.where(lane2 == (base + 2 * HD_C) % D_C, pv, 0.0), axis=-1, keepdims=True)
        pv = pv * (1.0 / den)
        o = pv[:tq] - lam * pv[tq:]
        inh = jnp.where(lane >= base, lane - base, 2 * HD_C) < 2 * HD_C
        oh = jnp.where(inh, o, 0.0)
        ms = jnp.sum(oh * oh, axis=-1, keepdims=True) * (1.0 / (2 * HD_C))
        acc = acc + oh * lax.rsqrt(ms + EPS)
    o_ref[...] = (acc * nw_ref[...] * (1.0 - lam_init)).astype(BF16)


def _attn_c(li, qc, kc, vc, lam_q, lam_k, nw_row, lam_init, with_ctx):
    parts = 2
    tq = parts * TILE
    off = CTX_LEN // TILE
    q_specs = [pl.BlockSpec((None, TILE, D_C), lambda b, n, i=i: (b, off + parts * n + i, 0))
               for i in range(parts)]
    small = [_layer_spec((2, HD_C), li), _layer_spec((2, HD_C), li), _layer_spec((1, D_C), li)]
    kern = functools.partial(_attn_c_kernel, lam_init=lam_init)
    all_kv = pl.BlockSpec((None, ROWS, D_C), lambda b, n: (b, 0, 0))
    all_v = pl.BlockSpec((None, N_HC, ROWS, D_C), lambda b, n: (b, 0, 0, 0))
    y_lat = pl.pallas_call(
        kern,
        grid=(BATCH, SEQ // tq),
        in_specs=small + q_specs + [all_kv, all_v],
        out_specs=pl.BlockSpec((None, tq, D_C), lambda b, n: (b, n, 0)),
        out_shape=jax.ShapeDtypeStruct((BATCH, SEQ, D_C), BF16),
        compiler_params=_cparams(("parallel", "parallel"), VMEM_LIMIT),
        name=f"attn_c_l{li}",
    )(lam_q, lam_k, nw_row, *([qc] * parts), kc, vc)
    if not with_ctx:
        return y_lat, None
    ctx_spec = pl.BlockSpec((None, CTX_LEN, D_C), lambda b: (b, 0, 0))
    y_ctx = pl.pallas_call(
        kern,
        grid=(BATCH,),
        in_specs=small + [ctx_spec, ctx_spec,
                           pl.BlockSpec((None, N_HC, CTX_LEN, D_C), lambda b: (b, 0, 0, 0))],
        out_specs=ctx_spec,
        out_shape=jax.ShapeDtypeStruct((BATCH, CTX_LEN, D_C), BF16),
        compiler_params=_cparams(("parallel",)),
        name=f"attn_c_ctx_l{li}",
    )(lam_q, lam_k, nw_row, qc, kc, vc)
    return y_lat, y_ctx


def _ssd_kernel(xbc_ref, z_ref, dt_ref, cw_ref, cb_ref, dtb_ref, alog_ref, dsk_ref, nw_ref,
                ef_ref, eb_ref, y_ref, u_scr, yacc_scr, st_scr):
    cw = cw_ref[...]
    cb = cb_ref[...]
    dsk = dsk_ref[...]
    pad = 8
    for c in range(N_CHUNKS):
        r0 = CHUNK * c
        zeros = jnp.zeros((pad, CONV_DIM), F32)
        if c in (0, N_CTX_CHUNKS):
            blk = jnp.concatenate([zeros, xbc_ref[r0:r0 + CHUNK + pad, :]], axis=0)
        elif c in (N_CTX_CHUNKS - 1, N_CHUNKS - 1):
            blk = jnp.concatenate([xbc_ref[r0 - pad:r0 + CHUNK, :], zeros], axis=0)
        else:
            blk = xbc_ref[r0 - pad:r0 + CHUNK + pad, :]
        acc = None
        for j in range(CONV_K):
            sh = (CONV_K // 2 - j) % (CHUNK + 2 * pad)
            rolled = blk if sh == 0 else pltpu.roll(blk, sh, 0)
            term = rolled[pad:pad + CHUNK, :] * cw[j:j + 1, :]
            acc = term if acc is None else acc + term
        u = _silu(acc + cb)
        u_scr[r0:r0 + CHUNK, :] = u
        yacc_scr[r0:r0 + CHUNK, :] = u[:, :D_INNER] * dsk

    st_scr[...] = jnp.zeros(st_scr.shape, F32)
    ri = lax.broadcasted_iota(jnp.int32, (CHUNK, CHUNK), 0)
    ci = lax.broadcasted_iota(jnp.int32, (CHUNK, CHUNK), 1)
    lower = ri >= ci
    upper = ri <= ci
    tri = (jnp.where(lower, 1.0, 0.0).astype(BF16), jnp.where(upper, 1.0, 0.0).astype(BF16))
    a_row = -jnp.exp(alog_ref[...])
    dtb = dtb_ref[...]
    lane_lo = ci < HD_S

    def chunk_dir(c, d):
        r0 = pl.multiple_of(c * CHUNK, CHUNK)
        u = u_scr[pl.ds(r0, CHUNK), :]
        xs = u[:, :D_INNER]
        bm = u[:, D_INNER:D_INNER + 2 * D_STATE]
        cm = u[:, D_INNER + 2 * D_STATE:]
        vv = dt_ref[pl.ds(r0, CHUNK), :] + dtb
        dt = jnp.maximum(vv, 0.0) + jnp.log(1.0 + jnp.exp(-jnp.abs(vv)))
        a = dt * a_row
        cum = sum(_dot(tri[d], piece) for piece in _split_bf16(a, 3))
        cum_t = cum.T
        tot = cum[CHUNK - 1:CHUNK, :] if d == 0 else cum[0:1, :]
        expand = ef_ref[...] if d == 0 else eb_ref[...]
        per_head = jnp.concatenate([dt, jnp.exp(cum), jnp.exp(tot - cum)], axis=0)
        per_lane = sum(_dot(piece, expand) for piece in _split_bf16(per_head, 2))
        dt_x = per_lane[:CHUNK]
        dec_out_x = per_lane[CHUNK:2 * CHUNK]
        dec_st_x = per_lane[2 * CHUNK:]
        etot_x = dec_out_x[CHUNK - 1:CHUNK, :] if d == 0 else dec_out_x[0:1, :]
        xdt = xs * dt_x
        xdt_b = xdt.astype(BF16)
        xst_b = (xdt * dec_st_x).astype(BF16)
        cb_ = [cm[:, g * D_STATE:(g + 1) * D_STATE].astype(BF16) for g in range(2)]
        bb_ = [bm[:, g * D_STATE:(g + 1) * D_STATE].astype(BF16) for g in range(2)]
        bt_ = [bm[:, g * D_STATE:(g + 1) * D_STATE].T.astype(BF16) for g in range(2)]
        cbm = [_dot_nt(cb_[g], bb_[g]) for g in range(2)]
        mask = lower if d == 0 else upper
        for j in range(3):
            sl = slice(j * LANES, (j + 1) * LANES)
            parts = []
            for h in (2 * j, 2 * j + 1):
                hl = h + N_HS * d
                seg = jnp.where(mask, cum[:, hl:hl + 1] - cum_t[hl:hl + 1, :], -jnp.inf)
                sc = (cbm[h // 3] * jnp.exp(seg)).astype(BF16)
                parts.append(_dot(sc, xdt_b[:, sl]))
            y_diag = jnp.where(lane_lo, parts[0], parts[1])
            st = st_scr[d, j]
            st_b = st.astype(BF16)
            g0, g1 = (2 * j) // 3, (2 * j + 1) // 3
            if g0 == g1:
                y_off = _dot(cb_[g0], st_b)
                upd = _dot(bt_[g0], xst_b[:, sl])
            else:
                y_off = jnp.where(lane_lo, _dot(cb_[g0], st_b), _dot(cb_[g1], st_b))
                upd = jnp.where(lane_lo, _dot(bt_[g0], xst_b[:, sl]), _dot(bt_[g1], xst_b[:, sl]))
            yacc_scr[pl.ds(r0, CHUNK), sl] += y_diag + y_off * dec_out_x[:, sl]
            st_scr[d, j] = etot_x[:, sl] * st + upd

    def body(i, carry):
        chunk_dir(i, 0)
        chunk_dir(jnp.where(i < N_CTX_CHUNKS, N_CTX_CHUNKS - 1 - i, N_CHUNKS + N_CTX_CHUNKS - 1 - i), 1)
        return carry

    lax.fori_loop(0, N_CHUNKS, body, 0, unroll=6)

    nw = nw_ref[...]
    for t in range(N_TILES):
        rows = slice(t * TILE, (t + 1) * TILE)
        y = yacc_scr[rows, :] * _silu(z_ref[rows, :])
        ms = jnp.mean(y * y, axis=-1, keepdims=True)
        y_ref[rows, :] = (y * lax.rsqrt(ms + EPS) * nw).astype(BF16)


def _ssd(li, xbc, z, dt, cw, cb, dtb, alog, dsk, nw, ef, eb):
    def seq(width):
        return pl.BlockSpec((None, ROWS, width), lambda b: (b, 0, 0))

    return pl.pallas_call(
        _ssd_kernel,
        grid=(BATCH,),
        in_specs=[seq(CONV_DIM), seq(D_INNER), seq(LANES),
                  _layer_spec((8, CONV_DIM), li), _layer_spec((1, CONV_DIM), li),
                  _layer_spec((1, LANES), li), _layer_spec((1, LANES), li),
                  _layer_spec((1, D_INNER), li), _layer_spec((1, D_INNER), li),
                  _full_spec((LANES, D_INNER)), _full_spec((LANES, D_INNER))],
        out_specs=seq(D_INNER),
        out_shape=jax.ShapeDtypeStruct((BATCH, ROWS, D_INNER), BF16),
        scratch_shapes=[pltpu.VMEM((ROWS, CONV_DIM), F32), pltpu.VMEM((ROWS, D_INNER), F32),
                        pltpu.VMEM((2, 3, D_STATE, LANES), F32)],
        compiler_params=_cparams(("parallel",), VMEM_LIMIT),
        name=f"ssd_l{li}",
    )(xbc, z, dt, cw, cb, dtb, alog, dsk, nw, ef, eb)


def _top2_gates(logits):
    lane = lax.broadcasted_iota(jnp.int32, logits.shape, 1).astype(F32)
    neg = -jnp.inf
    lg = jnp.where(lane < N_EXP, logits, neg)
    m1 = jnp.max(lg, axis=-1, keepdims=True)
    i1 = jnp.min(jnp.where(lg == m1, lane, float(LANES)), axis=-1, keepdims=True)
    lg2 = jnp.where(lane == i1, neg, lg)
    m2 = jnp.max(lg2, axis=-1, keepdims=True)
    i2 = jnp.min(jnp.where(lg2 == m2, lane, float(LANES)), axis=-1, keepdims=True)
    e2 = jnp.exp(m2 - m1)
    return lane, i1, i2, 1.0 / (1.0 + e2), e2 / (1.0 + e2)


R_E1, R_E2, R_W1, R_W2, R_RANK1, R_RANK2 = range(6)


def _route_record(logits, cnt_scr):
    lane, i1, i2, w1, w2 = _top2_gates(logits)
    rows = logits.shape[0]
    hit1, hit2 = lane == i1, lane == i2
    onehot = jnp.where(hit1, 1.0, 0.0) + jnp.where(hit2, 1.0, 0.0)
    ri = lax.broadcasted_iota(jnp.int32, (rows, rows), 0)
    ci = lax.broadcasted_iota(jnp.int32, (rows, rows), 1)
    earlier = jnp.where(ri > ci, 1.0, 0.0).astype(BF16)
    before = _dot(earlier, onehot.astype(BF16)) + cnt_scr[...]
    rank1 = jnp.sum(jnp.where(hit1, before, 0.0), axis=-1, keepdims=True)
    rank2 = jnp.sum(jnp.where(hit2, before, 0.0), axis=-1, keepdims=True)
    cnt_scr[...] += jnp.sum(onehot, axis=0, keepdims=True)
    rec = jnp.zeros(logits.shape, F32)
    for idx, val in ((R_E1, i1), (R_E2, i2), (R_W1, w1), (R_W2, w2), (R_RANK1, rank1), (R_RANK2, rank2)):
        rec = jnp.where(lane == float(idx), val, rec)
    return rec


def _outproj_kernel(*refs, moe, ctx_split):
    if ctx_split:
        ya_l, ya_c, yb_ref, yd_l, yd_c = refs[:5]
        refs = refs[5:]
    else:
        ya_l, yb_ref, yd_l = refs[:3]
        refs = refs[3:]
    x_ref, g1_ref, lg_ref, lb_ref, sh_ref, sc_ref, wo_ref = refs[:7]
    refs = refs[7:]
    if moe:
        wr_ref, br_ref, x1_ref, h2_ref, route_ref, counts_ref, cnt_scr = refs

        @pl.when((pl.program_id(0) == 0) & (pl.program_id(1) == 0))
        def _():
            cnt_scr[...] = jnp.zeros(cnt_scr.shape, F32)
    else:
        x1_ref, h2_ref = refs
    if ctx_split:
        is_ctx = pl.program_id(1) == 0
        ya = jnp.where(is_ctx, ya_c[...], ya_l[...])
        yd = jnp.where(is_ctx, yd_c[...], yd_l[...])
    else:
        ya, yd = ya_l[...], yd_l[...]
    mix = (_dot(ya, wo_ref[0:384, :]) + _dot(yb_ref[...], wo_ref[384:768, :])
           + _dot(yd, wo_ref[768:1024, :]))
    x1 = _ln(ALPHA * x_ref[...] + g1_ref[...] * mix) * lg_ref[...] + lb_ref[...]
    x1_ref[...] = x1
    h2 = _ln(x1) * (1.0 + sc_ref[...]) + sh_ref[...]
    h2_ref[...] = h2.astype(h2_ref.dtype)
    if moe:
        h_hi, h_mid = _split_bf16(h2, 2)
        both = _dot(h_hi, wr_ref[...])
        logits = (both[:, :LANES] + both[:, LANES:] + _dot(h_mid, wr_ref[:, :LANES])) + br_ref[...]
        route_ref[...] = _route_record(logits, cnt_scr)
        counts_ref[...] = cnt_scr[...]


def _outproj(li, ya_lat, ya_ctx, yb, yd_lat, yd_ctx, x_all, mod5, lg, lb, wo, moe, wr, br, j0):
    ctx_split = j0 == 0
    nt = N_TILES - j0
    off = CTX_LEN // TILE
    lat = lambda w: pl.BlockSpec((None, TILE, w), lambda b, j: (b, jnp.maximum(j + j0 - off, 0), 0))
    ctx = lambda w: pl.BlockSpec((None, TILE, w), lambda b, j: (b, 0, 0))
    row = _layer_spec((1, D_MODEL), li)
    if ctx_split:
        ins = [ya_lat, ya_ctx, yb, yd_lat, yd_ctx]
        specs = [lat(384), ctx(384), _tok_spec(D_INNER, j0), lat(D_C), ctx(D_C)]
    else:
        ins = [ya_lat, yb, yd_lat]
        specs = [lat(384), _tok_spec(D_INNER, j0), lat(D_C)]
    ins += [x_all, mod5, lg, lb, mod5, mod5, wo]
    specs += [_tok_spec(D_MODEL, j0), _mod_spec(li, 2, j0), row, row,
              _mod_spec(li, 3, j0), _mod_spec(li, 4, j0), _layer_spec((D_MODEL, D_MODEL), li)]
    outs = [jax.ShapeDtypeStruct((BATCH, nt * TILE, D_MODEL), F32),
            jax.ShapeDtypeStruct((BATCH, nt * TILE, D_MODEL), F32 if moe else BF16)]
    out_specs = [_tok_spec(D_MODEL), _tok_spec(D_MODEL)]
    scratch = []
    if moe:
        ins += [wr, br]
        specs += [_layer_spec((D_MODEL, 2 * LANES), li // 2), _layer_spec((1, LANES), li // 2)]
        outs += [jax.ShapeDtypeStruct((BATCH, nt * TILE, LANES), F32),
                 jax.ShapeDtypeStruct((1, LANES), F32)]
        out_specs += [_tok_spec(LANES), _full_spec((1, LANES))]
        scratch = [pltpu.VMEM((1, LANES), F32)]
    sem = ("arbitrary", "arbitrary") if moe else ("parallel", "parallel")
    return pl.pallas_call(
        functools.partial(_outproj_kernel, moe=moe, ctx_split=ctx_split),
        grid=(BATCH, nt),
        in_specs=specs,
        out_specs=out_specs,
        out_shape=outs,
        scratch_shapes=scratch,
        compiler_params=_cparams(sem, VMEM_LIMIT),
        name=f"outproj_l{li}",
    )(*ins)


def _swiglu_acc(h2, wgu_ref, wd_ref):
    acc = jnp.zeros((h2.shape[0], D_MODEL), F32)
    for c in range(N_FF_CHUNKS):
        cols = slice(c * FF_CHUNK, (c + 1) * FF_CHUNK)
        up_cols = slice(D_FF + c * FF_CHUNK, D_FF + (c + 1) * FF_CHUNK)
        act = (_silu(_dot(h2, wgu_ref[:, cols])) * _dot(h2, wgu_ref[:, up_cols])).astype(BF16)
        acc = acc + _dot(act, wd_ref[cols, :])
    return acc


W_STAGE = 256
GU_STAGE = 128


def _stage_weights(lead, wgu_hbm, wd_hbm, wgu_bf, wd_bf, stage_gu, stage_d, sems):
    def gu_src(c):
        return wgu_hbm.at[lead + (pl.ds(c * GU_STAGE, GU_STAGE), slice(None))]

    def d_src(c):
        return wd_hbm.at[lead + (pl.ds(c * W_STAGE, W_STAGE), slice(None))]

    n_gu, n_d = D_MODEL // GU_STAGE, D_FF // W_STAGE
    copies = [pltpu.make_async_copy(gu_src(c), stage_gu.at[c % 2], sems.at[c % 2]) for c in range(n_gu)]
    copies += [pltpu.make_async_copy(d_src(c), stage_d.at[c % 2], sems.at[2 + c % 2]) for c in range(n_d)]
    copies[0].start()
    for i, copy in enumerate(copies):
        if i + 1 < len(copies):
            copies[i + 1].start()
        copy.wait()
        if i < n_gu:
            wgu_bf[i * GU_STAGE:(i + 1) * GU_STAGE, :] = stage_gu[i % 2].astype(BF16)
        else:
            c = i - n_gu
            wd_bf[c * W_STAGE:(c + 1) * W_STAGE, :] = stage_d[c % 2].astype(BF16)


_WEIGHT_SCRATCH = [pltpu.VMEM((D_MODEL, 2 * D_FF), BF16), pltpu.VMEM((D_FF, D_MODEL), BF16),
                   pltpu.VMEM((2, GU_STAGE, 2 * D_FF), F32), pltpu.VMEM((2, W_STAGE, D_MODEL), F32),
                   pltpu.SemaphoreType.DMA((4,))]


FFN_TILE = WIDE_TILE


def _ffn_dense_kernel(x1_ref, h2_ref, g2_head_ref, g2_ref, lg_ref, lb_ref, wgu_hbm, wd_hbm, o_ref,
                      wgu_bf, wd_bf, stage_gu, stage_d, sems, *, fj):
    @pl.when((pl.program_id(0) == 0) & (pl.program_id(1) == 0))
    def _():
        _stage_weights((fj,), wgu_hbm, wd_hbm, wgu_bf, wd_bf, stage_gu, stage_d, sems)

    y = _swiglu_acc(h2_ref[...], wgu_bf, wd_bf)
    for rows, gate in ((slice(0, TILE), g2_head_ref), (slice(TILE, FFN_TILE), g2_ref)):
        o_ref[rows, :] = (_ln(ALPHA * x1_ref[rows, :] + gate[...] * y[rows]) * lg_ref[...] + lb_ref[...])


def _ffn_dense(li, x1, h2, mod5, lg, lb, wgu, wd):
    row = _layer_spec((1, D_MODEL), li)
    tok = pl.BlockSpec((None, FFN_TILE, D_MODEL), lambda b, j: (b, j, 0))
    batch_mod = pl.BlockSpec((None, None, None, 1, D_MODEL), lambda b, j: (li, b, 5, 0, 0))
    return pl.pallas_call(
        functools.partial(_ffn_dense_kernel, fj=li // 2),
        grid=(BATCH, ROWS // FFN_TILE),
        in_specs=[tok, tok, _mod_spec(li, 5, 0), batch_mod, row, row,
                  pl.BlockSpec(memory_space=pl.ANY), pl.BlockSpec(memory_space=pl.ANY)],
        out_specs=tok,
        out_shape=jax.ShapeDtypeStruct((BATCH, ROWS, D_MODEL), F32),
        scratch_shapes=_WEIGHT_SCRATCH,
        compiler_params=_cparams(("arbitrary", "arbitrary"), VMEM_LIMIT),
        name=f"ffn_dense_l{li}",
    )(x1, h2, mod5, mod5, lg, lb, wgu, wd)


EXP_TILE = 512


def _slot_row_copy(src_ref, src_row, dst_ref, dst_row, sem):
    return pltpu.make_async_copy(src_ref.at[pl.ds(src_row, 1), :], dst_ref.at[pl.ds(dst_row, 1), :], sem)


def _wait_rows(src_ref, dst_ref, sem):
    pltpu.make_async_copy(src_ref, dst_ref, sem).wait()


def _dispatch_kernel(pos_ref, fill_start_ref, fill_rows_ref, h2_ref, xs_ref, zero_scr, sem, zsem):
    first = (pl.program_id(0) == 0) & (pl.program_id(1) == 0)
    sub = 8
    pad_chunks = [1 << s for s in reversed(range(sub.bit_length() - 1, EXP_TILE.bit_length() - 1))]

    def pad_copies(e):
        start, n = fill_start_ref[e], fill_rows_ref[e]
        head = (-start) & (sub - 1)
        for i in range(sub - 1):
            yield i < jnp.minimum(head, n), _slot_row_copy(zero_scr, i, xs_ref, start + i, zsem)
        body = jnp.maximum(n - head, 0)
        for rows in pad_chunks:
            off = pl.multiple_of(start + head + (body & ~(2 * rows - 1)), sub)
            dst = xs_ref.at[pl.ds(off, rows), :]
            yield (body & rows) != 0, pltpu.make_async_copy(zero_scr.at[pl.ds(0, rows), :], dst, zsem)

    @pl.when(first)
    def _():
        zero_scr[...] = jnp.zeros(zero_scr.shape, F32)
        for e in range(fill_rows_ref.shape[0]):
            for cond, copy in pad_copies(e):
                pl.when(cond)(copy.start)

    base = (pl.program_id(0) * pl.num_programs(1) + pl.program_id(1)) * (2 * TILE)

    for r in range(TILE):
        for k in range(2):
            _slot_row_copy(h2_ref, r, xs_ref, pos_ref[base + 2 * r + k], sem).start(priority=k)
    for k in range(2):
        _wait_rows(h2_ref, xs_ref.at[pl.ds(0, TILE), :], sem)

    @pl.when(first)
    def _():
        for e in range(fill_rows_ref.shape[0]):
            for cond, copy in pad_copies(e):
                pl.when(cond)(copy.wait)


def _grouped_kernel(tile_e_ref, n_used_ref, xs_ref, wgu_hbm, wd_hbm, ys_ref,
                    wgu_bf, wd_bf, stage_gu, stage_d, sems, *, fj):
    i = pl.program_id(0)
    live = i < n_used_ref[0]
    e = tile_e_ref[i]

    @pl.when(live & ((i == 0) | (e != tile_e_ref[jnp.maximum(i - 1, 0)])))
    def _():
        _stage_weights((fj, e), wgu_hbm, wd_hbm, wgu_bf, wd_bf, stage_gu, stage_d, sems)

    @pl.when(live)
    def _():
        ys_ref[...] = _swiglu_acc(xs_ref[...].astype(BF16), wgu_bf, wd_bf)

    @pl.when(jnp.logical_not(live))
    def _():
        ys_ref[...] = jnp.zeros(ys_ref.shape, F32)


def _combine_kernel(pos_ref, ys_ref, route_ref, x1_ref, g2_ref, lg_ref, lb_ref, o_ref, ybuf, sem):
    base = (pl.program_id(0) * pl.num_programs(1) + pl.program_id(1)) * (2 * TILE)

    for r in range(TILE):
        for k in range(2):
            _slot_row_copy(ys_ref, pos_ref[base + 2 * r + k], ybuf.at[k], r, sem).start(priority=k)
    for k in range(2):
        _wait_rows(ys_ref.at[pl.ds(0, TILE), :], ybuf.at[k], sem)
    route = route_ref[...]
    lane = lax.broadcasted_iota(jnp.int32, route.shape, 1)
    w1 = jnp.sum(jnp.where(lane == R_W1, route, 0.0), axis=-1, keepdims=True)
    w2 = jnp.sum(jnp.where(lane == R_W2, route, 0.0), axis=-1, keepdims=True)
    y = w1 * ybuf[0] + w2 * ybuf[1]
    o_ref[...] = _ln(ALPHA * x1_ref[...] + g2_ref[...] * y) * lg_ref[...] + lb_ref[...]


def _ffn_routed(li, x1, h2, route, counts, mod5, lg, lb, wgu, wd, j0):
    nt = N_TILES - j0
    n_tok = BATCH * nt * TILE
    n_tiles = 2 * n_tok // EXP_TILE + N_EXP
    n_slots = n_tiles * EXP_TILE
    cnt = counts[0, :N_EXP].astype(jnp.int32)
    padded = (cnt + EXP_TILE - 1) // EXP_TILE * EXP_TILE
    ends = jnp.cumsum(padded)
    offs = ends - padded
    n_used = ends[-1:] // EXP_TILE
    tile_start = jnp.minimum(jnp.arange(n_tiles), n_used - 1) * EXP_TILE
    tile_e = jnp.sum((tile_start[:, None] >= ends[None, :]).astype(jnp.int32), axis=1)
    rec = route.reshape(n_tok, LANES)
    experts = rec[:, R_E1:R_E2 + 1].astype(jnp.int32)
    ranks = rec[:, R_RANK1:R_RANK2 + 1].astype(jnp.int32)
    off_of = jnp.sum(jnp.where(experts[..., None] == jnp.arange(N_EXP), offs, 0), axis=-1)
    pos = (off_of + ranks).reshape(2 * n_tok)

    h2_flat = h2.reshape(n_tok, D_MODEL)
    half = EXP_TILE // 2
    tail_start = n_used * EXP_TILE + jnp.arange(2 * N_EXP) * half
    tail_rows = jnp.where(tail_start < n_slots, half, 0)
    tail_start = jnp.minimum(tail_start, n_slots - half)
    any_spec = pl.BlockSpec(memory_space=pl.ANY)
    xs = pl.pallas_call(
        _dispatch_kernel,
        grid_spec=pltpu.PrefetchScalarGridSpec(
            num_scalar_prefetch=3,
            grid=(BATCH, nt),
            in_specs=[pl.BlockSpec((TILE, D_MODEL), lambda b, j, *_: (b * nt + j, 0))],
            out_specs=any_spec,
            scratch_shapes=[pltpu.VMEM((EXP_TILE // 2, D_MODEL), F32),
                            pltpu.SemaphoreType.DMA(()), pltpu.SemaphoreType.DMA(())]),
        out_shape=jax.ShapeDtypeStruct((n_slots, D_MODEL), F32),
        compiler_params=_cparams(("arbitrary", "arbitrary")),
        name=f"moe_dispatch_l{li}",
    )(pos, jnp.concatenate([offs + cnt, tail_start]), jnp.concatenate([padded - cnt, tail_rows]), h2_flat)

    used = lambda i, te, nu: jnp.minimum(i, nu[0] - 1)
    ys = pl.pallas_call(
        functools.partial(_grouped_kernel, fj=li // 2),
        grid_spec=pltpu.PrefetchScalarGridSpec(
            num_scalar_prefetch=2,
            grid=(n_tiles,),
            in_specs=[
                pl.BlockSpec((EXP_TILE, D_MODEL), lambda i, te, nu: (used(i, te, nu), 0)),
                pl.BlockSpec(memory_space=pl.ANY), pl.BlockSpec(memory_space=pl.ANY)],
            out_specs=pl.BlockSpec((EXP_TILE, D_MODEL), lambda i, te, nu: (i, 0)),
            scratch_shapes=_WEIGHT_SCRATCH),
        out_shape=jax.ShapeDtypeStruct((n_slots, D_MODEL), F32),
        compiler_params=_cparams(("arbitrary",), VMEM_LIMIT),
        name=f"moe_grouped_l{li}",
    )(tile_e, n_used, xs, wgu, wd)

    row = pl.BlockSpec((None, 1, D_MODEL), lambda b, j, pos: (li, 0, 0))
    tok = lambda w: pl.BlockSpec((None, TILE, w), lambda b, j, pos: (b, j, 0))
    mod = pl.BlockSpec((None, None, None, 1, D_MODEL),
                       lambda b, j, pos: (li, jnp.where(j + j0 == 0, CTX_MOD_ROW, b), 5, 0, 0))
    return pl.pallas_call(
        _combine_kernel,
        grid_spec=pltpu.PrefetchScalarGridSpec(
            num_scalar_prefetch=1,
            grid=(BATCH, nt),
            in_specs=[any_spec, tok(LANES), tok(D_MODEL), mod, row, row],
            out_specs=tok(D_MODEL),
            scratch_shapes=[pltpu.VMEM((2, TILE, D_MODEL), F32), pltpu.SemaphoreType.DMA(())]),
        out_shape=jax.ShapeDtypeStruct((BATCH, nt * TILE, D_MODEL), F32),
        compiler_params=_cparams(("arbitrary", "arbitrary"), VMEM_LIMIT),
        name=f"moe_combine_l{li}",
    )(pos, ys, route, x1, mod5, lg, lb)


def _rope_tables(dim):
    rows_n = SEQ // GRID_W
    t_row = jnp.repeat(jnp.arange(rows_n, dtype=F32), GRID_W)
    t_col = jnp.tile(jnp.arange(GRID_W, dtype=F32), rows_n)
    quarter = dim // 4
    inv = ROPE_BASE ** (-jnp.arange(quarter, dtype=F32) / quarter)
    ang = jnp.concatenate([t_row[:, None] * inv, t_col[:, None] * inv], axis=-1)
    cos, sin = jnp.cos(ang), jnp.sin(ang)
    reps = LANES // dim
    cos_f = jnp.tile(jnp.concatenate([cos, cos], axis=-1), (1, reps))
    sin_s = jnp.tile(jnp.concatenate([-sin, sin], axis=-1), (1, reps))
    cos_f = jnp.concatenate([jnp.ones((CTX_LEN, LANES), F32), cos_f], axis=0)
    sin_s = jnp.concatenate([jnp.zeros((CTX_LEN, LANES), F32), sin_s], axis=0)
    return cos_f, sin_s


def _head_expand(offset):
    rows = jnp.arange(LANES)[:, None]
    cols = jnp.arange(D_INNER)[None, :]
    return (rows == cols // HD_S + offset).astype(BF16)


def kernel(x, c, ctx, c_ctx, w_ada, b_ada, w_in, attn_sink, conv_w, conv_b, dt_bias, a_log, d_skip,
           ssm_norm_w, lam_q, lam_k, diff_norm_w, w_out, ln1_g, ln1_b, ln2_g, ln2_b, ffn_w_gu,
           ffn_w_down, router_w, router_b, exp_w_gu, exp_w_down):
    x_all = jnp.concatenate([ctx, x], axis=1)
    c_all = jnp.concatenate(
        [c, c_ctx[None, :], jnp.zeros((MOD_ROWS - BATCH - 1, D_MODEL), F32)], axis=0)
    mod5 = _ada_all(c_all, w_ada, b_ada).reshape(DEPTH, MOD_ROWS, 6, 1, D_MODEL)
    ropes = _rope_tables(HD_A) + _rope_tables(HD_C)
    ef, eb = _head_expand(0), _head_expand(N_HS)

    w_q = (w_in[:, :, :384].reshape(DEPTH, D_MODEL, N_KVA, 3, HD_A).transpose(0, 1, 3, 2, 4)
           .reshape(DEPTH, D_MODEL, 384))
    w_re = jnp.concatenate(
        [w_q, w_in[:, :, 384:1932], jnp.zeros((DEPTH, D_MODEL, D_IN_PAD - 2700), F32), w_in[:, :, 1932:]],
        axis=2).astype(BF16)
    wo_a = (w_out[:, :384].reshape(DEPTH, N_KVA, 3, HD_A, D_MODEL).transpose(0, 2, 1, 3, 4)
            .reshape(DEPTH, 384, D_MODEL))
    wo_re = jnp.concatenate([wo_a, w_out[:, 384:]], axis=1).astype(BF16)
    nw_c = jnp.tile(diff_norm_w, (1, N_HC))[:, None, :]
    cw = jnp.pad(conv_w, ((0, 0), (0, 8 - CONV_K), (0, 0)))
    pad12 = lambda a: jnp.pad(a.reshape(DEPTH, 1, 2 * N_HS), ((0, 0), (0, 0), (0, LANES - 2 * N_HS)))
    dtb, alog = pad12(dt_bias), pad12(a_log)
    dsk = jnp.repeat(d_skip, HD_S, axis=1)[:, None, :]
    lg1, lb1, lg2, lb2 = (a[:, None, :] for a in (ln1_g, ln1_b, ln2_g, ln2_b))
    wr_f = jnp.pad(router_w, ((0, 0), (0, 0), (0, LANES - N_EXP)))
    wr_hi = lax.bitcast_convert_type(lax.bitcast_convert_type(wr_f, jnp.uint32) & jnp.uint32(0xFFFF0000), F32)
    wr = jnp.concatenate([wr_hi.astype(BF16), (wr_f - wr_hi).astype(BF16)], axis=2)
    br = jnp.pad(router_b, ((0, 0), (0, LANES - N_EXP)))[:, None, :]

    for li in range(DEPTH):
        ctx_out = li < DEPTH - 1
        j0 = 0 if ctx_out else CTX_LEN // TILE
        lam_init = 0.8 - 0.6 * math.exp(-0.3 * li)
        qa, ka, va, z, xbc, dt, qc, kc, vc = _inproj(li, x_all, mod5, w_re, ropes)
        ya_lat, ya_ctx = _attn_a(li, qa, ka, va, attn_sink[li], ctx_out)
        yd_lat, yd_ctx = _attn_c(li, qc, kc, vc, lam_q, lam_k, nw_c, lam_init, ctx_out)
        yb = _ssd(li, xbc, z, dt, cw, conv_b[:, None, :], dtb, alog, dsk, ssm_norm_w[:, None, :], ef, eb)
        moe = li % 2 == 1
        res = _outproj(li, ya_lat, ya_ctx, yb, yd_lat, yd_ctx, x_all, mod5, lg1, lb1, wo_re, moe, wr, br, j0)
        if moe:
            x1, h2, route, counts = res
            x_all = _ffn_routed(li, x1, h2, route, counts, mod5, lg2, lb2, exp_w_gu, exp_w_down, j0)
        else:
            x1, h2 = res
            x_all = _ffn_dense(li, x1, h2, mod5, lg2, lb2, ffn_w_gu, ffn_w_down)
    return x_all
```

```python
import functools
import math

import jax
import jax.numpy as jnp
from jax import lax
from jax.experimental import pallas as pl
from jax.experimental.pallas import tpu as pltpu

F32 = jnp.float32
BF16 = jnp.bfloat16

D_MODEL = 1024
BATCH = 8
SEQ = 2048
DEPTH = 4
GRID_W = 64
CTX_LEN = 256
BLK = 128
WINDOW = 128
ROPE_BASE = 10000.0
EPS = 1e-6
LOG2E = 1.4426950408889634
HD_A = 64
N_HA = 6
N_KVA = 2
D_INNER = 384
HD_S = 64
N_HS = 6
D_STATE = 128
CONV_K = 5
CONV_DIM = 896
CHUNK = 128
N_HC = 4
HD_C = 32
D_C = 256
D_FF = 2816
N_EXP = 8
ALPHA = (2 * DEPTH) ** 0.25

LANES = 128
TILE = 256
WIDE_TILE = 3 * TILE
ROWS = CTX_LEN + SEQ
N_TILES = ROWS // TILE
N_CHUNKS = ROWS // CHUNK
N_CTX_CHUNKS = CTX_LEN // CHUNK
FF_CHUNK = 256
N_FF_CHUNKS = D_FF // FF_CHUNK
D_IN_PAD = 2816
MOD_ROWS = 16
CTX_MOD_ROW = BATCH
VMEM_LIMIT = 56 * 1024 * 1024


def _cparams(sem, vmem=None):
    return pltpu.CompilerParams(dimension_semantics=sem, vmem_limit_bytes=vmem)


def _sigmoid(x):
    return 1.0 / (1.0 + jnp.exp(-x))


def _silu(x):
    return x * _sigmoid(x)


def _ln(x):
    mu = jnp.mean(x, axis=-1, keepdims=True)
    xc = x - mu
    var = jnp.mean(xc * xc, axis=-1, keepdims=True)
    return xc * lax.rsqrt(var + EPS)


def _dot(a, b, precision=None):
    return jnp.dot(a, b, preferred_element_type=F32, precision=precision)


def _split_bf16(x, parts):
    out = []
    for _ in range(parts - 1):
        hi = x.astype(BF16)
        out.append(hi)
        x = x - hi.astype(F32)
    out.append(x.astype(BF16))
    return out


def _dot_nt(a, b):
    return lax.dot_general(a, b, (((1,), (1,)), ((), ())), preferred_element_type=F32)


def _ada_kernel(c_ref, w_ref, b_ref, o_ref):
    act = _silu(c_ref[...])
    o_ref[...] = _dot(act.astype(BF16), w_ref[...].astype(BF16)) + b_ref[...]


def _ada_all(c_all, w_ada, b_ada):
    tn = 1536
    return pl.pallas_call(
        _ada_kernel,
        grid=(DEPTH, 6 * D_MODEL // tn),
        in_specs=[
            pl.BlockSpec((MOD_ROWS, D_MODEL), lambda l, j: (0, 0)),
            pl.BlockSpec((None, D_MODEL, tn), lambda l, j: (l, 0, j)),
            pl.BlockSpec((None, 1, tn), lambda l, j: (l, 0, j)),
        ],
        out_specs=pl.BlockSpec((None, MOD_ROWS, tn), lambda l, j: (l, 0, j)),
        out_shape=jax.ShapeDtypeStruct((DEPTH, MOD_ROWS, 6 * D_MODEL), F32),
        compiler_params=_cparams(("parallel", "parallel")),
        name="ada_ln",
    )(c_all, w_ada, b_ada.reshape(DEPTH, 1, 6 * D_MODEL))


def _mod_spec(li, piece, j0):
    return pl.BlockSpec(
        (None, None, None, 1, D_MODEL),
        lambda b, j: (li, jnp.where(j + j0 == 0, CTX_MOD_ROW, b), piece, 0, 0))


def _tok_spec(width, j0=0):
    return pl.BlockSpec((None, TILE, width), lambda b, j: (b, j + j0, 0))


def _full_spec(shape):
    n = len(shape)
    return pl.BlockSpec(shape, lambda *_: (0,) * n)


def _layer_spec(tail, li):
    n = len(tail)
    return pl.BlockSpec((None,) + tuple(tail), lambda *_: (li,) + (0,) * n)


def _rope(v, cos, sin_signed, half):
    lane = lax.broadcasted_iota(jnp.int32, (v.shape[0], LANES), 1)
    first = (lane % (2 * half)) < half
    outs = []
    for s in range(v.shape[1] // LANES):
        blk = v[:, s * LANES:(s + 1) * LANES]
        swapped = jnp.where(first, pltpu.roll(blk, LANES - half, 1), pltpu.roll(blk, half, 1))
        outs.append(blk * cos + swapped * sin_signed)
    return jnp.concatenate(outs, axis=1)


def _inproj_kernel(x_ref, sh_head_ref, sc_head_ref, sh_ref, sc_ref, w_ref,
                   cosa_ref, sina_ref, cosc_ref, sinc_ref,
                   qa_ref, ka_ref, va_ref, z_ref, xbc_ref, dt_ref, qc_ref, kc_ref, vc_ref):
    xn = _ln(x_ref[...])
    hb = jnp.concatenate([xn[:TILE] * (1.0 + sc_head_ref[...]) + sh_head_ref[...],
                          xn[TILE:] * (1.0 + sc_ref[...]) + sh_ref[...]], axis=0).astype(BF16)

    def proj(lo, hi):
        return _dot(hb, w_ref[:, lo:hi])

    cosa, sina = cosa_ref[...], sina_ref[...]
    cosc, sinc = cosc_ref[...], sinc_ref[...]
    qk_a = proj(0, 512)
    qa_ref[...] = (_rope(qk_a[:, :384], cosa, sina, HD_A // 2) * (HD_A ** -0.5 * LOG2E)).astype(BF16)
    ka_ref[...] = _rope(qk_a[:, 384:], cosa, sina, HD_A // 2).astype(BF16)
    vz = proj(512, 1024)
    va_ref[...] = vz[:, :LANES].astype(BF16)
    z_ref[...] = vz[:, LANES:]
    xbc_dt = proj(1024, 2048)
    xbc_ref[...] = xbc_dt[:, :CONV_DIM]
    dt_ref[...] = xbc_dt[:, CONV_DIM:]
    qc_ref[...] = (_rope(proj(2048, 2304), cosc, sinc, HD_C // 2) * (HD_C ** -0.5 * LOG2E)).astype(BF16)
    kc_ref[...] = _rope(proj(2304, 2560), cosc, sinc, HD_C // 2).astype(BF16)
    vc = proj(2560, 2816)
    lane = lax.broadcasted_iota(jnp.int32, vc.shape, 1)
    for h in range(N_HC):
        in_head = jnp.where(lane >= h * 2 * HD_C, lane - h * 2 * HD_C, 2 * HD_C) < 2 * HD_C
        vc_ref[h] = jnp.where(in_head, vc, 1.0).astype(BF16)


def _inproj(li, x_all, mod5, w_re, ropes):
    widths = (384, 128, 128, 384, CONV_DIM, LANES, D_C, D_C, D_C)
    dtypes = (BF16, BF16, BF16, F32, F32, F32, BF16, BF16, BF16)
    rope_spec = pl.BlockSpec((WIDE_TILE, LANES), lambda b, j: (j, 0))
    tok = lambda w: pl.BlockSpec((None, WIDE_TILE, w), lambda b, j: (b, j, 0))
    batch_mod = lambda piece: pl.BlockSpec(
        (None, None, None, 1, D_MODEL), lambda b, j: (li, b, piece, 0, 0))
    return pl.pallas_call(
        _inproj_kernel,
        grid=(BATCH, ROWS // WIDE_TILE),
        in_specs=[tok(D_MODEL), _mod_spec(li, 0, 0), _mod_spec(li, 1, 0), batch_mod(0), batch_mod(1),
                  _layer_spec((D_MODEL, D_IN_PAD), li), rope_spec, rope_spec, rope_spec, rope_spec],
        out_specs=[tok(w) for w in widths[:-1]]
        + [pl.BlockSpec((None, N_HC, WIDE_TILE, D_C), lambda b, j: (b, 0, j, 0))],
        out_shape=[jax.ShapeDtypeStruct((BATCH, ROWS, w), d) for w, d in zip(widths[:-1], dtypes)]
        + [jax.ShapeDtypeStruct((BATCH, N_HC, ROWS, D_C), BF16)],
        compiler_params=_cparams(("parallel", "parallel"), VMEM_LIMIT),
        name=f"inproj_l{li}",
    )(x_all, mod5, mod5, mod5, mod5, w_re, *ropes)


def _attn_a_blocks(sink_ref, blocks):
    tq = blocks[0][1].shape[0]
    lane_lo = lax.broadcasted_iota(jnp.int32, (tq, LANES), 1) < HD_A
    scores = []
    for _, q, k, _ in blocks:
        q = q.astype(F32)
        slabs = [q[:, s * LANES:(s + 1) * LANES] for s in range(3)]
        qst = jnp.concatenate([jnp.where(lane_lo, blk, 0.0) for blk in slabs]
                              + [jnp.where(lane_lo, 0.0, blk) for blk in slabs], axis=0).astype(BF16)
        scores.append(_dot_nt(qst, k))
    probs = [[] for _ in blocks]
    sink_terms = [[] for _ in blocks]
    for rb in range(N_HA):
        sink = sink_ref[rb] * LOG2E
        for i, (bias, _, _, _) in enumerate(blocks):
            sh = scores[i][rb * tq:(rb + 1) * tq]
            if bias is not None:
                sh = jnp.concatenate([sh[:, :3 * BLK] + bias, sh[:, 3 * BLK:]], axis=1)
            m = jnp.maximum(jnp.max(sh, axis=-1, keepdims=True), sink)
            probs[i].append(jnp.exp2(sh - m).astype(BF16))
            sink_terms[i].append(jnp.exp2(sink - m))
    results = []
    for i, (_, _, _, v) in enumerate(blocks):
        vf = v.astype(F32)
        key_lo = lax.broadcasted_iota(jnp.int32, vf.shape, 1) < HD_A
        v_heads = (jnp.where(key_lo, vf, 1.0).astype(BF16), jnp.where(key_lo, 1.0, vf).astype(BF16))
        outs = []
        for g in range(N_KVA):
            o = _dot(jnp.concatenate(probs[i][3 * g:3 * g + 3], axis=0), v_heads[g])
            den = pltpu.roll(o, HD_A, 1) + jnp.concatenate(sink_terms[i][3 * g:3 * g + 3], axis=0)
            outs.append(o * (1.0 / den))
        results.append(jnp.concatenate(
            [jnp.where(lane_lo, outs[0][s * tq:(s + 1) * tq], outs[1][s * tq:(s + 1) * tq])
             for s in range(3)], axis=1).astype(BF16))
    return results


def _attn_a_kernel(*refs, local):
    if not local:
        sink_ref, q_ref, kx_ref, vx_ref, o_ref = refs
        o_ref[...] = _attn_a_blocks(sink_ref, [(None, q_ref[...], kx_ref[...], vx_ref[...])])[0]
        return
    sink_ref, bias_first, bias_mid, bias_last = refs[:4]
    q_refs = refs[4:4 + A_BLOCKS // 2]
    rest = refs[4 + A_BLOCKS // 2:]
    n_kv = A_BLOCKS + 3
    k_blocks = [r[...] for r in rest[:n_kv]]
    v_blocks = [r[...] for r in rest[n_kv:2 * n_kv]]
    o_ref = rest[2 * n_kv]
    blocks = []
    for i in range(A_BLOCKS):
        bias_ref = bias_first if i == 0 else bias_last if i == A_BLOCKS - 1 else bias_mid
        k = jnp.concatenate(k_blocks[i:i + 3] + k_blocks[-1:], axis=0)
        v = jnp.concatenate(v_blocks[i:i + 3] + v_blocks[-1:], axis=0)
        q = q_refs[i // 2][(i % 2) * BLK:(i % 2 + 1) * BLK, :]
        blocks.append((bias_ref[...], q, k, v))
    for i, out in enumerate(_attn_a_blocks(sink_ref, blocks)):
        o_ref[i * BLK:(i + 1) * BLK, :] = out


A_BLOCKS = 8


def _band_bias():
    i = jnp.arange(BLK)[:, None]
    j = jnp.arange(3 * BLK)[None, :] - BLK
    in_band = jnp.abs(i - j) <= WINDOW
    first = in_band & (j >= 0)
    last = in_band & (j < BLK)
    return jnp.where(jnp.stack([first, in_band, last]), 0.0, -jnp.inf).astype(F32)


def _attn_a(li, qa, ka, va, sink, with_ctx):
    nb = SEQ // BLK
    off = CTX_LEN // BLK
    smem = pl.BlockSpec(memory_space=pltpu.SMEM)
    steps = nb // A_BLOCKS
    bias_first = pl.BlockSpec((None, BLK, 3 * BLK), lambda b, n: (jnp.where(n == 0, 0, 1), 0, 0))
    bias_mid = pl.BlockSpec((None, BLK, 3 * BLK), lambda b, n: (1, 0, 0))
    bias_last = pl.BlockSpec((None, BLK, 3 * BLK), lambda b, n: (jnp.where(n == steps - 1, 2, 1), 0, 0))

    def kv(shift):
        return pl.BlockSpec(
            (None, BLK, LANES), lambda b, n: (b, off + jnp.clip(A_BLOCKS * n + shift, 0, nb - 1), 0))

    ctx_kv = pl.BlockSpec((None, CTX_LEN, LANES), lambda b, n: (b, 0, 0))
    q_specs = [pl.BlockSpec((None, 2 * BLK, 384),
                            lambda b, n, i=i: (b, CTX_LEN // (2 * BLK) + (A_BLOCKS // 2) * n + i, 0))
               for i in range(A_BLOCKS // 2)]
    kvs = [kv(s) for s in range(-1, A_BLOCKS + 1)] + [ctx_kv]
    y_lat = pl.pallas_call(
        functools.partial(_attn_a_kernel, local=True),
        grid=(BATCH, steps),
        in_specs=[smem, bias_first, bias_mid, bias_last] + q_specs + kvs + kvs,
        out_specs=pl.BlockSpec((None, A_BLOCKS * BLK, 384), lambda b, n: (b, n, 0)),
        out_shape=jax.ShapeDtypeStruct((BATCH, SEQ, 384), BF16),
        compiler_params=_cparams(("parallel", "parallel")),
        name=f"attn_a_l{li}",
    )(sink, *([_band_bias()] * 3), *([qa] * (A_BLOCKS // 2)), *([ka] * len(kvs)), *([va] * len(kvs)))
    if not with_ctx:
        return y_lat, None
    ctx_spec384 = pl.BlockSpec((None, CTX_LEN, 384), lambda b: (b, 0, 0))
    ctx_spec128 = pl.BlockSpec((None, CTX_LEN, LANES), lambda b: (b, 0, 0))
    y_ctx = pl.pallas_call(
        functools.partial(_attn_a_kernel, local=False),
        grid=(BATCH,),
        in_specs=[smem, ctx_spec384, ctx_spec128, ctx_spec128],
        out_specs=ctx_spec384,
        out_shape=jax.ShapeDtypeStruct((BATCH, CTX_LEN, 384), BF16),
        compiler_params=_cparams(("parallel",)),
        name=f"attn_a_ctx_l{li}",
    )(sink, qa, ka, va)
    return y_lat, y_ctx


def _attn_c_kernel(lq_ref, lk_ref, nw_ref, *refs, lam_init):
    q_refs, (k_ref, v_ref, o_ref) = refs[:-3], refs[-3:]
    tq = o_ref.shape[0]
    e = jnp.exp(jnp.sum(lq_ref[...] * lk_ref[...], axis=-1, keepdims=True))
    lam = e[0:1] - e[1:2] + lam_init
    q = jnp.concatenate([r[...] for r in q_refs], axis=0).astype(F32)
    k = k_ref[...]
    lane = lax.broadcasted_iota(jnp.int32, (tq, D_C), 1)
    lane2 = lax.broadcasted_iota(jnp.int32, (2 * tq, D_C), 1)
    acc = jnp.zeros((tq, D_C), F32)
    for h in range(N_HC):
        base = h * 2 * HD_C
        in0 = jnp.where(lane >= base, lane - base, 2 * HD_C) < HD_C
        in1 = jnp.where(lane >= base + HD_C, lane - base - HD_C, HD_C) < HD_C
        qst = jnp.concatenate([jnp.where(in0, q, 0.0), jnp.where(in1, q, 0.0)], axis=0).astype(BF16)
        s = _dot_nt(qst, k)
        p = jnp.exp2(s - jnp.max(s, axis=-1, keepdims=True)).astype(BF16)
        pv = _dot(p, v_ref[h])
        den = jnp.sum(jnp.where(lane2 == (base + 2 * HD_C) % D_C, pv, 0.0), axis=-1, keepdims=True)
        pv = pv * (1.0 / den)
        o = pv[:tq] - lam * pv[tq:]
        inh = jnp.where(lane >= base, lane - base, 2 * HD_C) < 2 * HD_C
        oh = jnp.where(inh, o, 0.0)
        ms = jnp.sum(oh * oh, axis=-1, keepdims=True) * (1.0 / (2 * HD_C))
        acc = acc + oh * lax.rsqrt(ms + EPS)
    o_ref[...] = (acc * nw_ref[...] * (1.0 - lam_init)).astype(BF16)


def _attn_c(li, qc, kc, vc, lam_q, lam_k, nw_row, lam_init, with_ctx):
    parts = 2
    tq = parts * TILE
    off = CTX_LEN // TILE
    q_specs = [pl.BlockSpec((None, TILE, D_C), lambda b, n, i=i: (b, off + parts * n + i, 0))
               for i in range(parts)]
    small = [_layer_spec((2, HD_C), li), _layer_spec((2, HD_C), li), _layer_spec((1, D_C), li)]
    kern = functools.partial(_attn_c_kernel, lam_init=lam_init)
    all_kv = pl.BlockSpec((None, ROWS, D_C), lambda b, n: (b, 0, 0))
    all_v = pl.BlockSpec((None, N_HC, ROWS, D_C), lambda b, n: (b, 0, 0, 0))
    y_lat = pl.pallas_call(
        kern,
        grid=(BATCH, SEQ // tq),
        in_specs=small + q_specs + [all_kv, all_v],
        out_specs=pl.BlockSpec((None, tq, D_C), lambda b, n: (b, n, 0)),
        out_shape=jax.ShapeDtypeStruct((BATCH, SEQ, D_C), BF16),
        compiler_params=_cparams(("parallel", "parallel"), VMEM_LIMIT),
        name=f"attn_c_l{li}",
    )(lam_q, lam_k, nw_row, *([qc] * parts), kc, vc)
    if not with_ctx:
        return y_lat, None
    ctx_spec = pl.BlockSpec((None, CTX_LEN, D_C), lambda b: (b, 0, 0))
    y_ctx = pl.pallas_call(
        kern,
        grid=(BATCH,),
        in_specs=small + [ctx_spec, ctx_spec,
                           pl.BlockSpec((None, N_HC, CTX_LEN, D_C), lambda b: (b, 0, 0, 0))],
        out_specs=ctx_spec,
        out_shape=jax.ShapeDtypeStruct((BATCH, CTX_LEN, D_C), BF16),
        compiler_params=_cparams(("parallel",)),
        name=f"attn_c_ctx_l{li}",
    )(lam_q, lam_k, nw_row, qc, kc, vc)
    return y_lat, y_ctx


def _ssd_kernel(xbc_ref, z_ref, dt_ref, cw_ref, cb_ref, dtb_ref, alog_ref, dsk_ref, nw_ref,
                ef_ref, eb_ref, y_ref, u_scr, yacc_scr, st_scr):
    cw = cw_ref[...]
    cb = cb_ref[...]
    dsk = dsk_ref[...]
    pad = 8
    for c in range(N_CHUNKS):
        r0 = CHUNK * c
        zeros = jnp.zeros((pad, CONV_DIM), F32)
        if c in (0, N_CTX_CHUNKS):
            blk = jnp.concatenate([zeros, xbc_ref[r0:r0 + CHUNK + pad, :]], axis=0)
        elif c in (N_CTX_CHUNKS - 1, N_CHUNKS - 1):
            blk = jnp.concatenate([xbc_ref[r0 - pad:r0 + CHUNK, :], zeros], axis=0)
        else:
            blk = xbc_ref[r0 - pad:r0 + CHUNK + pad, :]
        acc = None
        for j in range(CONV_K):
            sh = (CONV_K // 2 - j) % (CHUNK + 2 * pad)
            rolled = blk if sh == 0 else pltpu.roll(blk, sh, 0)
            term = rolled[pad:pad + CHUNK, :] * cw[j:j + 1, :]
            acc = term if acc is None else acc + term
        u = _silu(acc + cb)
        u_scr[r0:r0 + CHUNK, :] = u
        yacc_scr[r0:r0 + CHUNK, :] = u[:, :D_INNER] * dsk

    st_scr[...] = jnp.zeros(st_scr.shape, F32)
    ri = lax.broadcasted_iota(jnp.int32, (CHUNK, CHUNK), 0)
    ci = lax.broadcasted_iota(jnp.int32, (CHUNK, CHUNK), 1)
    lower = ri >= ci
    upper = ri <= ci
    tri = (jnp.where(lower, 1.0, 0.0).astype(BF16), jnp.where(upper, 1.0, 0.0).astype(BF16))
    a_row = -jnp.exp(alog_ref[...])
    dtb = dtb_ref[...]
    lane_lo = ci < HD_S

    def chunk_dir(c, d):
        r0 = pl.multiple_of(c * CHUNK, CHUNK)
        u = u_scr[pl.ds(r0, CHUNK), :]
        xs = u[:, :D_INNER]
        bm = u[:, D_INNER:D_INNER + 2 * D_STATE]
        cm = u[:, D_INNER + 2 * D_STATE:]
        vv = dt_ref[pl.ds(r0, CHUNK), :] + dtb
        dt = jnp.maximum(vv, 0.0) + jnp.log(1.0 + jnp.exp(-jnp.abs(vv)))
        a = dt * a_row
        cum = sum(_dot(tri[d], piece) for piece in _split_bf16(a, 3))
        cum_t = cum.T
        tot = cum[CHUNK - 1:CHUNK, :] if d == 0 else cum[0:1, :]
        expand = ef_ref[...] if d == 0 else eb_ref[...]
        per_head = jnp.concatenate([dt, jnp.exp(cum), jnp.exp(tot - cum)], axis=0)
        per_lane = sum(_dot(piece, expand) for piece in _split_bf16(per_head, 2))
        dt_x = per_lane[:CHUNK]
        dec_out_x = per_lane[CHUNK:2 * CHUNK]
        dec_st_x = per_lane[2 * CHUNK:]
        etot_x = dec_out_x[CHUNK - 1:CHUNK, :] if d == 0 else dec_out_x[0:1, :]
        xdt = xs * dt_x
        xdt_b = xdt.astype(BF16)
        xst_b = (xdt * dec_st_x).astype(BF16)
        cb_ = [cm[:, g * D_STATE:(g + 1) * D_STATE].astype(BF16) for g in range(2)]
        bb_ = [bm[:, g * D_STATE:(g + 1) * D_STATE].astype(BF16) for g in range(2)]
        bt_ = [bm[:, g * D_STATE:(g + 1) * D_STATE].T.astype(BF16) for g in range(2)]
        cbm = [_dot_nt(cb_[g], bb_[g]) for g in range(2)]
        mask = lower if d == 0 else upper
        for j in range(3):
            sl = slice(j * LANES, (j + 1) * LANES)
            parts = []
            for h in (2 * j, 2 * j + 1):
                hl = h + N_HS * d
                seg = jnp.where(mask, cum[:, hl:hl + 1] - cum_t[hl:hl + 1, :], -jnp.inf)
                sc = (cbm[h // 3] * jnp.exp(seg)).astype(BF16)
                parts.append(_dot(sc, xdt_b[:, sl]))
            y_diag = jnp.where(lane_lo, parts[0], parts[1])
            st = st_scr[d, j]
            st_b = st.astype(BF16)
            g0, g1 = (2 * j) // 3, (2 * j + 1) // 3
            if g0 == g1:
                y_off = _dot(cb_[g0], st_b)
                upd = _dot(bt_[g0], xst_b[:, sl])
            else:
                y_off = jnp.where(lane_lo, _dot(cb_[g0], st_b), _dot(cb_[g1], st_b))
                upd = jnp.where(lane_lo, _dot(bt_[g0], xst_b[:, sl]), _dot(bt_[g1], xst_b[:, sl]))
            yacc_scr[pl.ds(r0, CHUNK), sl] += y_diag + y_off * dec_out_x[:, sl]
            st_scr[d, j] = etot_x[:, sl] * st + upd

    def body(i, carry):
        chunk_dir(i, 0)
        chunk_dir(jnp.where(i < N_CTX_CHUNKS, N_CTX_CHUNKS - 1 - i, N_CHUNKS + N_CTX_CHUNKS - 1 - i), 1)
        return carry

    lax.fori_loop(0, N_CHUNKS, body, 0, unroll=6)

    nw = nw_ref[...]
    for t in range(N_TILES):
        rows = slice(t * TILE, (t + 1) * TILE)
        y = yacc_scr[rows, :] * _silu(z_ref[rows, :])
        ms = jnp.mean(y * y, axis=-1, keepdims=True)
        y_ref[rows, :] = (y * lax.rsqrt(ms + EPS) * nw).astype(BF16)


def _ssd(li, xbc, z, dt, cw, cb, dtb, alog, dsk, nw, ef, eb):
    def seq(width):
        return pl.BlockSpec((None, ROWS, width), lambda b: (b, 0, 0))

    return pl.pallas_call(
        _ssd_kernel,
        grid=(BATCH,),
        in_specs=[seq(CONV_DIM), seq(D_INNER), seq(LANES),
                  _layer_spec((8, CONV_DIM), li), _layer_spec((1, CONV_DIM), li),
                  _layer_spec((1, LANES), li), _layer_spec((1, LANES), li),
                  _layer_spec((1, D_INNER), li), _layer_spec((1, D_INNER), li),
                  _full_spec((LANES, D_INNER)), _full_spec((LANES, D_INNER))],
        out_specs=seq(D_INNER),
        out_shape=jax.ShapeDtypeStruct((BATCH, ROWS, D_INNER), BF16),
        scratch_shapes=[pltpu.VMEM((ROWS, CONV_DIM), F32), pltpu.VMEM((ROWS, D_INNER), F32),
                        pltpu.VMEM((2, 3, D_STATE, LANES), F32)],
        compiler_params=_cparams(("parallel",), VMEM_LIMIT),
        name=f"ssd_l{li}",
    )(xbc, z, dt, cw, cb, dtb, alog, dsk, nw, ef, eb)


def _top2_gates(logits):
    lane = lax.broadcasted_iota(jnp.int32, logits.shape, 1).astype(F32)
    neg = -jnp.inf
    lg = jnp.where(lane < N_EXP, logits, neg)
    m1 = jnp.max(lg, axis=-1, keepdims=True)
    i1 = jnp.min(jnp.where(lg == m1, lane, float(LANES)), axis=-1, keepdims=True)
    lg2 = jnp.where(lane == i1, neg, lg)
    m2 = jnp.max(lg2, axis=-1, keepdims=True)
    i2 = jnp.min(jnp.where(lg2 == m2, lane, float(LANES)), axis=-1, keepdims=True)
    e2 = jnp.exp(m2 - m1)
    return lane, i1, i2, 1.0 / (1.0 + e2), e2 / (1.0 + e2)


R_E1, R_E2, R_W1, R_W2, R_RANK1, R_RANK2 = range(6)


def _route_record(logits, cnt_scr):
    lane, i1, i2, w1, w2 = _top2_gates(logits)
    rows = logits.shape[0]
    hit1, hit2 = lane == i1, lane == i2
    onehot = jnp.where(hit1, 1.0, 0.0) + jnp.where(hit2, 1.0, 0.0)
    ri = lax.broadcasted_iota(jnp.int32, (rows, rows), 0)
    ci = lax.broadcasted_iota(jnp.int32, (rows, rows), 1)
    earlier = jnp.where(ri > ci, 1.0, 0.0).astype(BF16)
    before = _dot(earlier, onehot.astype(BF16)) + cnt_scr[...]
    rank1 = jnp.sum(jnp.where(hit1, before, 0.0), axis=-1, keepdims=True)
    rank2 = jnp.sum(jnp.where(hit2, before, 0.0), axis=-1, keepdims=True)
    cnt_scr[...] += jnp.sum(onehot, axis=0, keepdims=True)
    rec = jnp.zeros(logits.shape, F32)
    for idx, val in ((R_E1, i1), (R_E2, i2), (R_W1, w1), (R_W2, w2), (R_RANK1, rank1), (R_RANK2, rank2)):
        rec = jnp.where(lane == float(idx), val, rec)
    return rec


def _outproj_kernel(*refs, moe, ctx_split):
    if ctx_split:
        ya_l, ya_c, yb_ref, yd_l, yd_c = refs[:5]
        refs = refs[5:]
    else:
        ya_l, yb_ref, yd_l = refs[:3]
        refs = refs[3:]
    x_ref, g1_ref, lg_ref, lb_ref, sh_ref, sc_ref, wo_ref = refs[:7]
    refs = refs[7:]
    if moe:
        wr_ref, br_ref, x1_ref, h2_ref, route_ref, counts_ref, cnt_scr = refs

        @pl.when((pl.program_id(0) == 0) & (pl.program_id(1) == 0))
        def _():
            cnt_scr[...] = jnp.zeros(cnt_scr.shape, F32)
    else:
        x1_ref, h2_ref = refs
    if ctx_split:
        is_ctx = pl.program_id(1) == 0
        ya = jnp.where(is_ctx, ya_c[...], ya_l[...])
        yd = jnp.where(is_ctx, yd_c[...], yd_l[...])
    else:
        ya, yd = ya_l[...], yd_l[...]
    mix = (_dot(ya, wo_ref[0:384, :]) + _dot(yb_ref[...], wo_ref[384:768, :])
           + _dot(yd, wo_ref[768:1024, :]))
    x1 = _ln(ALPHA * x_ref[...] + g1_ref[...] * mix) * lg_ref[...] + lb_ref[...]
    x1_ref[...] = x1
    h2 = _ln(x1) * (1.0 + sc_ref[...]) + sh_ref[...]
    h2_ref[...] = h2.astype(h2_ref.dtype)
    if moe:
        h_hi, h_mid = _split_bf16(h2, 2)
        both = _dot(h_hi, wr_ref[...])
        logits = (both[:, :LANES] + both[:, LANES:] + _dot(h_mid, wr_ref[:, :LANES])) + br_ref[...]
        route_ref[...] = _route_record(logits, cnt_scr)
        counts_ref[...] = cnt_scr[...]


def _outproj(li, ya_lat, ya_ctx, yb, yd_lat, yd_ctx, x_all, mod5, lg, lb, wo, moe, wr, br, j0):
    ctx_split = j0 == 0
    nt = N_TILES - j0
    off = CTX_LEN // TILE
    lat = lambda w: pl.BlockSpec((None, TILE, w), lambda b, j: (b, jnp.maximum(j + j0 - off, 0), 0))
    ctx = lambda w: pl.BlockSpec((None, TILE, w), lambda b, j: (b, 0, 0))
    row = _layer_spec((1, D_MODEL), li)
    if ctx_split:
        ins = [ya_lat, ya_ctx, yb, yd_lat, yd_ctx]
        specs = [lat(384), ctx(384), _tok_spec(D_INNER, j0), lat(D_C), ctx(D_C)]
    else:
        ins = [ya_lat, yb, yd_lat]
        specs = [lat(384), _tok_spec(D_INNER, j0), lat(D_C)]
    ins += [x_all, mod5, lg, lb, mod5, mod5, wo]
    specs += [_tok_spec(D_MODEL, j0), _mod_spec(li, 2, j0), row, row,
              _mod_spec(li, 3, j0), _mod_spec(li, 4, j0), _layer_spec((D_MODEL, D_MODEL), li)]
    outs = [jax.ShapeDtypeStruct((BATCH, nt * TILE, D_MODEL), F32),
            jax.ShapeDtypeStruct((BATCH, nt * TILE, D_MODEL), F32 if moe else BF16)]
    out_specs = [_tok_spec(D_MODEL), _tok_spec(D_MODEL)]
    scratch = []
    if moe:
        ins += [wr, br]
        specs += [_layer_spec((D_MODEL, 2 * LANES), li // 2), _layer_spec((1, LANES), li // 2)]
        outs += [jax.ShapeDtypeStruct((BATCH, nt * TILE, LANES), F32),
                 jax.ShapeDtypeStruct((1, LANES), F32)]
        out_specs += [_tok_spec(LANES), _full_spec((1, LANES))]
        scratch = [pltpu.VMEM((1, LANES), F32)]
    sem = ("arbitrary", "arbitrary") if moe else ("parallel", "parallel")
    return pl.pallas_call(
        functools.partial(_outproj_kernel, moe=moe, ctx_split=ctx_split),
        grid=(BATCH, nt),
        in_specs=specs,
        out_specs=out_specs,
        out_shape=outs,
        scratch_shapes=scratch,
        compiler_params=_cparams(sem, VMEM_LIMIT),
        name=f"outproj_l{li}",
    )(*ins)


def _swiglu_acc(h2, wgu_ref, wd_ref):
    acc = jnp.zeros((h2.shape[0], D_MODEL), F32)
    for c in range(N_FF_CHUNKS):
        cols = slice(c * FF_CHUNK, (c + 1) * FF_CHUNK)
        up_cols = slice(D_FF + c * FF_CHUNK, D_FF + (c + 1) * FF_CHUNK)
        act = (_silu(_dot(h2, wgu_ref[:, cols])) * _dot(h2, wgu_ref[:, up_cols])).astype(BF16)
        acc = acc + _dot(act, wd_ref[cols, :])
    return acc


W_STAGE = 256
GU_STAGE = 128


def _stage_weights(lead, wgu_hbm, wd_hbm, wgu_bf, wd_bf, stage_gu, stage_d, sems):
    def gu_src(c):
        return wgu_hbm.at[lead + (pl.ds(c * GU_STAGE, GU_STAGE), slice(None))]

    def d_src(c):
        return wd_hbm.at[lead + (pl.ds(c * W_STAGE, W_STAGE), slice(None))]

    n_gu, n_d = D_MODEL // GU_STAGE, D_FF // W_STAGE
    copies = [pltpu.make_async_copy(gu_src(c), stage_gu.at[c % 2], sems.at[c % 2]) for c in range(n_gu)]
    copies += [pltpu.make_async_copy(d_src(c), stage_d.at[c % 2], sems.at[2 + c % 2]) for c in range(n_d)]
    copies[0].start()
    for i, copy in enumerate(copies):
        if i + 1 < len(copies):
            copies[i + 1].start()
        copy.wait()
        if i < n_gu:
            wgu_bf[i * GU_STAGE:(i + 1) * GU_STAGE, :] = stage_gu[i % 2].astype(BF16)
        else:
            c = i - n_gu
            wd_bf[c * W_STAGE:(c + 1) * W_STAGE, :] = stage_d[c % 2].astype(BF16)


_WEIGHT_SCRATCH = [pltpu.VMEM((D_MODEL, 2 * D_FF), BF16), pltpu.VMEM((D_FF, D_MODEL), BF16),
                   pltpu.VMEM((2, GU_STAGE, 2 * D_FF), F32), pltpu.VMEM((2, W_STAGE, D_MODEL), F32),
                   pltpu.SemaphoreType.DMA((4,))]


FFN_TILE = WIDE_TILE


def _ffn_dense_kernel(x1_ref, h2_ref, g2_head_ref, g2_ref, lg_ref, lb_ref, wgu_hbm, wd_hbm, o_ref,
                      wgu_bf, wd_bf, stage_gu, stage_d, sems, *, fj):
    @pl.when((pl.program_id(0) == 0) & (pl.program_id(1) == 0))
    def _():
        _stage_weights((fj,), wgu_hbm, wd_hbm, wgu_bf, wd_bf, stage_gu, stage_d, sems)

    y = _swiglu_acc(h2_ref[...], wgu_bf, wd_bf)
    for rows, gate in ((slice(0, TILE), g2_head_ref), (slice(TILE, FFN_TILE), g2_ref)):
        o_ref[rows, :] = (_ln(ALPHA * x1_ref[rows, :] + gate[...] * y[rows]) * lg_ref[...] + lb_ref[...])


def _ffn_dense(li, x1, h2, mod5, lg, lb, wgu, wd):
    row = _layer_spec((1, D_MODEL), li)
    tok = pl.BlockSpec((None, FFN_TILE, D_MODEL), lambda b, j: (b, j, 0))
    batch_mod = pl.BlockSpec((None, None, None, 1, D_MODEL), lambda b, j: (li, b, 5, 0, 0))
    return pl.pallas_call(
        functools.partial(_ffn_dense_kernel, fj=li // 2),
        grid=(BATCH, ROWS // FFN_TILE),
        in_specs=[tok, tok, _mod_spec(li, 5, 0), batch_mod, row, row,
                  pl.BlockSpec(memory_space=pl.ANY), pl.BlockSpec(memory_space=pl.ANY)],
        out_specs=tok,
        out_shape=jax.ShapeDtypeStruct((BATCH, ROWS, D_MODEL), F32),
        scratch_shapes=_WEIGHT_SCRATCH,
        compiler_params=_cparams(("arbitrary", "arbitrary"), VMEM_LIMIT),
        name=f"ffn_dense_l{li}",
    )(x1, h2, mod5, mod5, lg, lb, wgu, wd)


EXP_TILE = 512


def _slot_row_copy(src_ref, src_row, dst_ref, dst_row, sem):
    return pltpu.make_async_copy(src_ref.at[pl.ds(src_row, 1), :], dst_ref.at[pl.ds(dst_row, 1), :], sem)


def _wait_rows(src_ref, dst_ref, sem):
    pltpu.make_async_copy(src_ref, dst_ref, sem).wait()


def _dispatch_kernel(pos_ref, fill_start_ref, fill_rows_ref, h2_ref, xs_ref, zero_scr, sem, zsem):
    first = (pl.program_id(0) == 0) & (pl.program_id(1) == 0)
    sub = 8
    pad_chunks = [1 << s for s in reversed(range(sub.bit_length() - 1, EXP_TILE.bit_length() - 1))]

    def pad_copies(e):
        start, n = fill_start_ref[e], fill_rows_ref[e]
        head = (-start) & (sub - 1)
        for i in range(sub - 1):
            yield i < jnp.minimum(head, n), _slot_row_copy(zero_scr, i, xs_ref, start + i, zsem)
        body = jnp.maximum(n - head, 0)
        for rows in pad_chunks:
            off = pl.multiple_of(start + head + (body & ~(2 * rows - 1)), sub)
            dst = xs_ref.at[pl.ds(off, rows), :]
            yield (body & rows) != 0, pltpu.make_async_copy(zero_scr.at[pl.ds(0, rows), :], dst, zsem)

    @pl.when(first)
    def _():
        zero_scr[...] = jnp.zeros(zero_scr.shape, F32)
        for e in range(fill_rows_ref.shape[0]):
            for cond, copy in pad_copies(e):
                pl.when(cond)(copy.start)

    base = (pl.program_id(0) * pl.num_programs(1) + pl.program_id(1)) * (2 * TILE)

    for r in range(TILE):
        for k in range(2):
            _slot_row_copy(h2_ref, r, xs_ref, pos_ref[base + 2 * r + k], sem).start(priority=k)
    for k in range(2):
        _wait_rows(h2_ref, xs_ref.at[pl.ds(0, TILE), :], sem)

    @pl.when(first)
    def _():
        for e in range(fill_rows_ref.shape[0]):
            for cond, copy in pad_copies(e):
                pl.when(cond)(copy.wait)


def _grouped_kernel(tile_e_ref, n_used_ref, xs_ref, wgu_hbm, wd_hbm, ys_ref,
                    wgu_bf, wd_bf, stage_gu, stage_d, sems, *, fj):
    i = pl.program_id(0)
    live = i < n_used_ref[0]
    e = tile_e_ref[i]

    @pl.when(live & ((i == 0) | (e != tile_e_ref[jnp.maximum(i - 1, 0)])))
    def _():
        _stage_weights((fj, e), wgu_hbm, wd_hbm, wgu_bf, wd_bf, stage_gu, stage_d, sems)

    @pl.when(live)
    def _():
        ys_ref[...] = _swiglu_acc(xs_ref[...].astype(BF16), wgu_bf, wd_bf)

    @pl.when(jnp.logical_not(live))
    def _():
        ys_ref[...] = jnp.zeros(ys_ref.shape, F32)


def _combine_kernel(pos_ref, ys_ref, route_ref, x1_ref, g2_ref, lg_ref, lb_ref, o_ref, ybuf, sem):
    base = (pl.program_id(0) * pl.num_programs(1) + pl.program_id(1)) * (2 * TILE)

    for r in range(TILE):
        for k in range(2):
            _slot_row_copy(ys_ref, pos_ref[base + 2 * r + k], ybuf.at[k], r, sem).start(priority=k)
    for k in range(2):
        _wait_rows(ys_ref.at[pl.ds(0, TILE), :], ybuf.at[k], sem)
    route = route_ref[...]
    lane = lax.broadcasted_iota(jnp.int32, route.shape, 1)
    w1 = jnp.sum(jnp.where(lane == R_W1, route, 0.0), axis=-1, keepdims=True)
    w2 = jnp.sum(jnp.where(lane == R_W2, route, 0.0), axis=-1, keepdims=True)
    y = w1 * ybuf[0] + w2 * ybuf[1]
    o_ref[...] = _ln(ALPHA * x1_ref[...] + g2_ref[...] * y) * lg_ref[...] + lb_ref[...]


def _ffn_routed(li, x1, h2, route, counts, mod5, lg, lb, wgu, wd, j0):
    nt = N_TILES - j0
    n_tok = BATCH * nt * TILE
    n_tiles = 2 * n_tok // EXP_TILE + N_EXP
    n_slots = n_tiles * EXP_TILE
    cnt = counts[0, :N_EXP].astype(jnp.int32)
    padded = (cnt + EXP_TILE - 1) // EXP_TILE * EXP_TILE
    ends = jnp.cumsum(padded)
    offs = ends - padded
    n_used = ends[-1:] // EXP_TILE
    tile_start = jnp.minimum(jnp.arange(n_tiles), n_used - 1) * EXP_TILE
    tile_e = jnp.sum((tile_start[:, None] >= ends[None, :]).astype(jnp.int32), axis=1)
    rec = route.reshape(n_tok, LANES)
    experts = rec[:, R_E1:R_E2 + 1].astype(jnp.int32)
    ranks = rec[:, R_RANK1:R_RANK2 + 1].astype(jnp.int32)
    off_of = jnp.sum(jnp.where(experts[..., None] == jnp.arange(N_EXP), offs, 0), axis=-1)
    pos = (off_of + ranks).reshape(2 * n_tok)

    h2_flat = h2.reshape(n_tok, D_MODEL)
    half = EXP_TILE // 2
    tail_start = n_used * EXP_TILE + jnp.arange(2 * N_EXP) * half
    tail_rows = jnp.where(tail_start < n_slots, half, 0)
    tail_start = jnp.minimum(tail_start, n_slots - half)
    any_spec = pl.BlockSpec(memory_space=pl.ANY)
    xs = pl.pallas_call(
        _dispatch_kernel,
        grid_spec=pltpu.PrefetchScalarGridSpec(
            num_scalar_prefetch=3,
            grid=(BATCH, nt),
            in_specs=[pl.BlockSpec((TILE, D_MODEL), lambda b, j, *_: (b * nt + j, 0))],
            out_specs=any_spec,
            scratch_shapes=[pltpu.VMEM((EXP_TILE // 2, D_MODEL), F32),
                            pltpu.SemaphoreType.DMA(()), pltpu.SemaphoreType.DMA(())]),
        out_shape=jax.ShapeDtypeStruct((n_slots, D_MODEL), F32),
        compiler_params=_cparams(("arbitrary", "arbitrary")),
        name=f"moe_dispatch_l{li}",
    )(pos, jnp.concatenate([offs + cnt, tail_start]), jnp.concatenate([padded - cnt, tail_rows]), h2_flat)

    used = lambda i, te, nu: jnp.minimum(i, nu[0] - 1)
    ys = pl.pallas_call(
        functools.partial(_grouped_kernel, fj=li // 2),
        grid_spec=pltpu.PrefetchScalarGridSpec(
            num_scalar_prefetch=2,
            grid=(n_tiles,),
            in_specs=[
                pl.BlockSpec((EXP_TILE, D_MODEL), lambda i, te, nu: (used(i, te, nu), 0)),
                pl.BlockSpec(memory_space=pl.ANY), pl.BlockSpec(memory_space=pl.ANY)],
            out_specs=pl.BlockSpec((EXP_TILE, D_MODEL), lambda i, te, nu: (i, 0)),
            scratch_shapes=_WEIGHT_SCRATCH),
        out_shape=jax.ShapeDtypeStruct((n_slots, D_MODEL), F32),
        compiler_params=_cparams(("arbitrary",), VMEM_LIMIT),
        name=f"moe_grouped_l{li}",
    )(tile_e, n_used, xs, wgu, wd)

    row = pl.BlockSpec((None, 1, D_MODEL), lambda b, j, pos: (li, 0, 0))
    tok = lambda w: pl.BlockSpec((None, TILE, w), lambda b, j, pos: (b, j, 0))
    mod = pl.BlockSpec((None, None, None, 1, D_MODEL),
                       lambda b, j, pos: (li, jnp.where(j + j0 == 0, CTX_MOD_ROW, b), 5, 0, 0))
    return pl.pallas_call(
        _combine_kernel,
        grid_spec=pltpu.PrefetchScalarGridSpec(
            num_scalar_prefetch=1,
            grid=(BATCH, nt),
            in_specs=[any_spec, tok(LANES), tok(D_MODEL), mod, row, row],
            out_specs=tok(D_MODEL),
            scratch_shapes=[pltpu.VMEM((2, TILE, D_MODEL), F32), pltpu.SemaphoreType.DMA(())]),
        out_shape=jax.ShapeDtypeStruct((BATCH, nt * TILE, D_MODEL), F32),
        compiler_params=_cparams(("arbitrary", "arbitrary"), VMEM_LIMIT),
        name=f"moe_combine_l{li}",
    )(pos, ys, route, x1, mod5, lg, lb)


def _rope_tables(dim):
    rows_n = SEQ // GRID_W
    t_row = jnp.repeat(jnp.arange(rows_n, dtype=F32), GRID_W)
    t_col = jnp.tile(jnp.arange(GRID_W, dtype=F32), rows_n)
    quarter = dim // 4
    inv = ROPE_BASE ** (-jnp.arange(quarter, dtype=F32) / quarter)
    ang = jnp.concatenate([t_row[:, None] * inv, t_col[:, None] * inv], axis=-1)
    cos, sin = jnp.cos(ang), jnp.sin(ang)
    reps = LANES // dim
    cos_f = jnp.tile(jnp.concatenate([cos, cos], axis=-1), (1, reps))
    sin_s = jnp.tile(jnp.concatenate([-sin, sin], axis=-1), (1, reps))
    cos_f = jnp.concatenate([jnp.ones((CTX_LEN, LANES), F32), cos_f], axis=0)
    sin_s = jnp.concatenate([jnp.zeros((CTX_LEN, LANES), F32), sin_s], axis=0)
    return cos_f, sin_s


def _head_expand(offset):
    rows = jnp.arange(LANES)[:, None]
    cols = jnp.arange(D_INNER)[None, :]
    return (rows == cols // HD_S + offset).astype(BF16)


def kernel(x, c, ctx, c_ctx, w_ada, b_ada, w_in, attn_sink, conv_w, conv_b, dt_bias, a_log, d_skip,
           ssm_norm_w, lam_q, lam_k, diff_norm_w, w_out, ln1_g, ln1_b, ln2_g, ln2_b, ffn_w_gu,
           ffn_w_down, router_w, router_b, exp_w_gu, exp_w_down):
    x_all = jnp.concatenate([ctx, x], axis=1)
    c_all = jnp.concatenate(
        [c, c_ctx[None, :], jnp.zeros((MOD_ROWS - BATCH - 1, D_MODEL), F32)], axis=0)
    mod5 = _ada_all(c_all, w_ada, b_ada).reshape(DEPTH, MOD_ROWS, 6, 1, D_MODEL)
    ropes = _rope_tables(HD_A) + _rope_tables(HD_C)
    ef, eb = _head_expand(0), _head_expand(N_HS)

    w_q = (w_in[:, :, :384].reshape(DEPTH, D_MODEL, N_KVA, 3, HD_A).transpose(0, 1, 3, 2, 4)
           .reshape(DEPTH, D_MODEL, 384))
    w_re = jnp.concatenate(
        [w_q, w_in[:, :, 384:1932], jnp.zeros((DEPTH, D_MODEL, D_IN_PAD - 2700), F32), w_in[:, :, 1932:]],
        axis=2).astype(BF16)
    wo_a = (w_out[:, :384].reshape(DEPTH, N_KVA, 3, HD_A, D_MODEL).transpose(0, 2, 1, 3, 4)
            .reshape(DEPTH, 384, D_MODEL))
    wo_re = jnp.concatenate([wo_a, w_out[:, 384:]], axis=1).astype(BF16)
    nw_c = jnp.tile(diff_norm_w, (1, N_HC))[:, None, :]
    cw = jnp.pad(conv_w, ((0, 0), (0, 8 - CONV_K), (0, 0)))
    pad12 = lambda a: jnp.pad(a.reshape(DEPTH, 1, 2 * N_HS), ((0, 0), (0, 0), (0, LANES - 2 * N_HS)))
    dtb, alog = pad12(dt_bias), pad12(a_log)
    dsk = jnp.repeat(d_skip, HD_S, axis=1)[:, None, :]
    lg1, lb1, lg2, lb2 = (a[:, None, :] for a in (ln1_g, ln1_b, ln2_g, ln2_b))
    wr_f = jnp.pad(router_w, ((0, 0), (0, 0), (0, LANES - N_EXP)))
    wr_hi = lax.bitcast_convert_type(lax.bitcast_convert_type(wr_f, jnp.uint32) & jnp.uint32(0xFFFF0000), F32)
    wr = jnp.concatenate([wr_hi.astype(BF16), (wr_f - wr_hi).astype(BF16)], axis=2)
    br = jnp.pad(router_b, ((0, 0), (0, LANES - N_EXP)))[:, None, :]

    for li in range(DEPTH):
        ctx_out = li < DEPTH - 1
        j0 = 0 if ctx_out else CTX_LEN // TILE
        lam_init = 0.8 - 0.6 * math.exp(-0.3 * li)
        qa, ka, va, z, xbc, dt, qc, kc, vc = _inproj(li, x_all, mod5, w_re, ropes)
        ya_lat, ya_ctx = _attn_a(li, qa, ka, va, attn_sink[li], ctx_out)
        yd_lat, yd_ctx = _attn_c(li, qc, kc, vc, lam_q, lam_k, nw_c, lam_init, ctx_out)
        yb = _ssd(li, xbc, z, dt, cw, conv_b[:, None, :], dtb, alog, dsk, ssm_norm_w[:, None, :], ef, eb)
        moe = li % 2 == 1
        res = _outproj(li, ya_lat, ya_ctx, yb, yd_lat, yd_ctx, x_all, mod5, lg1, lb1, wo_re, moe, wr, br, j0)
        if moe:
            x1, h2, route, counts = res
            x_all = _ffn_routed(li, x1, h2, route, counts, mod5, lg2, lb2, exp_w_gu, exp_w_down, j0)
        else:
            x1, h2 = res
            x_all = _ffn_dense(li, x1, h2, mod5, lg2, lb2, ffn_w_gu, ffn_w_down)
    return x_all
```
